```python
import jax, jax.numpy as jnp
from jax import lax
import numpy as np

D_MODEL = 1024
BATCH = 8
SEQ = 2048
DEPTH = 1
DEC_BATCH = 128
DEC_SEQ = 1
PAST_LEN = 16384
PAGE_SIZE = 128

HEAD_SIZE = 64
N_HEADS = D_MODEL // HEAD_SIZE
DECAY_LORA = 64
ICLR_LORA = 64
GATE_LORA = 128
CONV_WIDTH = 3
D_FF = ((8 * D_MODEL // 3 + 127) // 128) * 128
PLE_DIM = 256
SHIFT_COLS = 3 * D_MODEL + DECAY_LORA + ICLR_LORA + GATE_LORA
CONV_COLS = 3 * D_MODEL
GATE_COLS = 2 * D_MODEL
IN_COLS = SHIFT_COLS + CONV_COLS + GATE_COLS
RMS_EPS = 1e-6
GN_EPS = 64e-5
NORM_EPS = 1e-12

kernel_name = "rwkv7_shortconv_gated_hybrid_step"


def rmsnorm(x, g):
    xf = x.astype(jnp.float32)
    y = xf * lax.rsqrt(jnp.mean(xf * xf, axis=-1, keepdims=True) + RMS_EPS) * g.astype(jnp.float32)
    return y.astype(x.dtype)


def swiglu(h, w_gate, w_up, w_down):
    return (jax.nn.silu(h @ w_gate) * (h @ w_up)) @ w_down


def split_cols(z, sizes):
    idx = np.cumsum(sizes)[:-1].tolist()
    return jnp.split(z, idx, axis=-1)


def wkv_step(S, inp):
    r_t, decay_t, k_t, v_t, kk_t, b_t = inp
    sa = jnp.einsum('bhvk,bhk->bhv', S, kk_t)
    S = (S * decay_t[:, :, None, :]
         - sa[..., None] * b_t[:, :, None, :]
         + v_t[..., None] * k_t[:, :, None, :])
    y = jnp.einsum('bhvk,bhk->bhv', S, r_t)
    return S, y


def rwkv7_branch(zs, S0, w_decay_up, decay_bias, w_iclr_up, iclr_bias, w_gate_up,
                 k_k, k_a, r_k, gn_w, gn_b):
    B, T, _ = zs.shape
    r, k, v, wd, ad, gd = split_cols(zs, [D_MODEL, D_MODEL, D_MODEL, DECAY_LORA, ICLR_LORA, GATE_LORA])
    f32 = jnp.float32
    w = -jax.nn.softplus(-(decay_bias + jnp.tanh(wd) @ w_decay_up).astype(f32)) - 0.5
    decay = jnp.exp(-jnp.exp(w))
    a = jax.nn.sigmoid((iclr_bias + ad @ w_iclr_up).astype(f32))
    g = jax.nn.sigmoid(gd) @ w_gate_up
    heads = lambda t: t.astype(f32).reshape(B, T, N_HEADS, HEAD_SIZE)
    kk = heads(k * k_k)
    kk = kk / jnp.maximum(jnp.linalg.norm(kk, axis=-1, keepdims=True), NORM_EPS)
    k_mod = k.astype(f32) * (1.0 + (a - 1.0) * k_a.astype(f32))
    rh, kh, vh, ah, dh = heads(r), heads(k_mod), heads(v), heads(a), heads(decay)
    bh = kk * ah
    tm = lambda t: jnp.moveaxis(t, 1, 0)
    S_new, ys = lax.scan(wkv_step, S0.astype(f32), (tm(rh), tm(dh), tm(kh), tm(vh), tm(kk), tm(bh)))
    y = jnp.moveaxis(ys, 0, 1)
    mean = jnp.mean(y, axis=-1, keepdims=True)
    var = jnp.mean(jnp.square(y - mean), axis=-1, keepdims=True)
    y = (y - mean) * lax.rsqrt(var + GN_EPS)
    y = y.reshape(B, T, D_MODEL) * gn_w.astype(f32) + gn_b.astype(f32)
    bonus = jnp.sum(rh * kh * r_k.astype(f32), axis=-1, keepdims=True) * vh
    y = (y + bonus.reshape(B, T, D_MODEL)).astype(zs.dtype) * g
    return y, S_new


def short_conv_branch(zc, conv0, conv_w):
    gb, gc, u = split_cols(zc, [D_MODEL, D_MODEL, D_MODEL])
    T = zc.shape[1]
    cu = gc * u
    full = jnp.concatenate([conv0.astype(cu.dtype), cu], axis=1)
    y = full[:, 0:T] * conv_w[0]
    for j in range(1, CONV_WIDTH):
        y = y + full[:, j:j + T] * conv_w[j]
    return gb * y, full[:, T:]


def decoder_layer(x, p, S0, shift0, conv0,
                  norm_ffn1, w_ffn1_gate, w_ffn1_up, w_ffn1_down,
                  norm_mix, w_in, mu_shift, w_decay_up, decay_bias, w_iclr_up, iclr_bias,
                  w_gate_up, k_k, k_a, r_k, gn_w, gn_b, conv_w, w_out,
                  norm_ffn2, w_ffn2_gate, w_ffn2_up, w_ffn2_down,
                  norm_ple, w_ple_gate, w_ple_proj):
    x = x + 0.5 * swiglu(rmsnorm(x, norm_ffn1), w_ffn1_gate, w_ffn1_up, w_ffn1_down)
    h = rmsnorm(x, norm_mix)
    z = h @ w_in
    z_sh, z_conv, z_gate = split_cols(z, [SHIFT_COLS, CONV_COLS, GATE_COLS])
    z_prev = jnp.concatenate([shift0[:, None, :].astype(z_sh.dtype), z_sh[:, :-1]], axis=1)
    zs = z_sh + mu_shift * (z_prev - z_sh)
    new_shift = z_sh[:, -1]
    y_a, S_new = rwkv7_branch(zs, S0, w_decay_up, decay_bias, w_iclr_up, iclr_bias, w_gate_up,
                              k_k, k_a, r_k, gn_w, gn_b)
    y_b, conv_new = short_conv_branch(z_conv, conv0, conv_w)
    g_a, g_b = split_cols(jax.nn.sigmoid(z_gate), [D_MODEL, D_MODEL])
    x = x + (g_a * y_a + g_b * y_b) @ w_out
    x = x + 0.5 * swiglu(rmsnorm(x, norm_ffn2), w_ffn2_gate, w_ffn2_up, w_ffn2_down)
    x = x + jax.nn.sigmoid(rmsnorm(x, norm_ple) @ w_ple_gate) * (p @ w_ple_proj)
    return x, S_new, new_shift, conv_new


def setup_inputs(seed: int = 0) -> dict:
    key = jax.random.key(seed)
    ks = iter(jax.random.split(key, 48))
    f32 = jnp.float32
    nrm = lambda shape, scale: jax.random.normal(next(ks), shape, f32) * scale
    gain = lambda shape: 1.0 + 0.05 * jax.random.normal(next(ks), shape, f32)
    L, D = DEPTH, D_MODEL
    return {
        "x_prompt": nrm((BATCH, SEQ, D), 1.0),
        "x_sample": nrm((DEC_BATCH, DEC_SEQ, D), 1.0),
        "p_prompt": nrm((DEPTH, BATCH, SEQ, PLE_DIM), 1.0),
        "p_sample": nrm((DEPTH, DEC_BATCH, DEC_SEQ, PLE_DIM), 1.0),
        "state_wkv": nrm((L, DEC_BATCH, N_HEADS, HEAD_SIZE, HEAD_SIZE), 0.1),
        "state_shift": nrm((L, DEC_BATCH, SHIFT_COLS), 1.0),
        "state_conv": nrm((L, DEC_BATCH, CONV_WIDTH - 1, D), 0.5),
        "norm_ffn1": gain((L, D)),
        "w_ffn1_gate": nrm((L, D, D_FF), D ** -0.5),
        "w_ffn1_up": nrm((L, D, D_FF), D ** -0.5),
        "w_ffn1_down": nrm((L, D_FF, D), D_FF ** -0.5),
        "norm_mix": gain((L, D)),
        "w_in": nrm((L, D, IN_COLS), D ** -0.5),
        "mu_shift": jax.random.uniform(next(ks), (L, SHIFT_COLS), f32),
        "w_decay_up": nrm((L, DECAY_LORA, D), DECAY_LORA ** -0.5),
        "decay_bias": nrm((L, D), 0.5),
        "w_iclr_up": nrm((L, ICLR_LORA, D), ICLR_LORA ** -0.5),
        "iclr_bias": nrm((L, D), 0.1),
        "w_gate_up": nrm((L, GATE_LORA, D), GATE_LORA ** -0.5),
        "k_k": 0.85 + nrm((L, D), 0.05),
        "k_a": gain((L, D)),
        "r_k": nrm((L, N_HEADS, HEAD_SIZE), 0.1),
        "gn_w": gain((L, D)),
        "gn_b": nrm((L, D), 0.02),
        "conv_w": nrm((L, CONV_WIDTH, D), CONV_WIDTH ** -0.5),
        "w_out": nrm((L, D, D), D ** -0.5),
        "norm_ffn2": gain((L, D)),
        "w_ffn2_gate": nrm((L, D, D_FF), D ** -0.5),
        "w_ffn2_up": nrm((L, D, D_FF), D ** -0.5),
        "w_ffn2_down": nrm((L, D_FF, D), D_FF ** -0.5),
        "norm_ple": gain((L, D)),
        "w_ple_gate": nrm((L, D, D), D ** -0.5),
        "w_ple_proj": nrm((L, PLE_DIM, D), PLE_DIM ** -0.5),
        "norm_final": gain((D,)),
    }


def reference(x_prompt, x_sample, p_prompt, p_sample, state_wkv, state_shift, state_conv,
              norm_ffn1, w_ffn1_gate, w_ffn1_up, w_ffn1_down,
              norm_mix, w_in, mu_shift, w_decay_up, decay_bias, w_iclr_up, iclr_bias,
              w_gate_up, k_k, k_a, r_k, gn_w, gn_b, conv_w, w_out,
              norm_ffn2, w_ffn2_gate, w_ffn2_up, w_ffn2_down,
              norm_ple, w_ple_gate, w_ple_proj, norm_final):
    xp, xs = x_prompt, x_sample
    bp = x_prompt.shape[0]
    wkv_p, sh_p, cv_p, wkv_s, sh_s, cv_s = [], [], [], [], [], []
    for i in range(DEPTH):
        lw = (norm_ffn1[i], w_ffn1_gate[i], w_ffn1_up[i], w_ffn1_down[i],
              norm_mix[i], w_in[i], mu_shift[i], w_decay_up[i], decay_bias[i], w_iclr_up[i], iclr_bias[i],
              w_gate_up[i], k_k[i], k_a[i], r_k[i], gn_w[i], gn_b[i], conv_w[i], w_out[i],
              norm_ffn2[i], w_ffn2_gate[i], w_ffn2_up[i], w_ffn2_down[i],
              norm_ple[i], w_ple_gate[i], w_ple_proj[i])
        S0p = jnp.zeros((bp, N_HEADS, HEAD_SIZE, HEAD_SIZE), state_wkv.dtype)
        sh0p = jnp.zeros((bp, SHIFT_COLS), state_shift.dtype)
        cv0p = jnp.zeros((bp, CONV_WIDTH - 1, D_MODEL), state_conv.dtype)
        xp, Sp, shp, cvp = decoder_layer(xp, p_prompt[i], S0p, sh0p, cv0p, *lw)
        xs, Ss, shs, cvs = decoder_layer(xs, p_sample[i], state_wkv[i], state_shift[i], state_conv[i], *lw)
        wkv_p.append(Sp.astype(state_wkv.dtype)); sh_p.append(shp.astype(state_shift.dtype)); cv_p.append(cvp.astype(state_conv.dtype))
        wkv_s.append(Ss.astype(state_wkv.dtype)); sh_s.append(shs.astype(state_shift.dtype)); cv_s.append(cvs.astype(state_conv.dtype))
    y_prompt = rmsnorm(xp, norm_final)
    y_sample = rmsnorm(xs, norm_final)
    new_wkv_prompt = jnp.stack(wkv_p, 0)
    new_shift_prompt = jnp.stack(sh_p, 0)
    new_conv_prompt = jnp.stack(cv_p, 0)
    new_wkv_sample = jnp.stack(wkv_s, 0)
    new_shift_sample = jnp.stack(sh_s, 0)
    new_conv_sample = jnp.stack(cv_s, 0)
    return (y_prompt, y_sample, new_wkv_prompt, new_shift_prompt, new_conv_prompt, new_wkv_sample, new_shift_sample, new_conv_sample)
```

```python
import functools

import jax
import jax.numpy as jnp
from jax import lax
from jax.experimental import pallas as pl
from jax.experimental.pallas import tpu as pltpu

F32 = jnp.float32
BF16 = jnp.bfloat16
HIGHEST = lax.Precision.HIGHEST

HEAD = 64
LANES = 128
CHUNK = 64
RMS_EPS = 1e-6
GN_EPS = 64e-5
NORM_EPS = 1e-12
VMEM_LIMIT = 56 * 1024 * 1024


def _rms(x, g):
    return x * lax.rsqrt(jnp.mean(x * x, axis=-1, keepdims=True) + RMS_EPS) * g


def _sigmoid(x):
    return 1.0 / (1.0 + jnp.exp(-x))


def _softplus(x):
    return jnp.maximum(x, 0.0) + jnp.log(1.0 + jnp.exp(-jnp.abs(x)))


def _bdot(a, b):
    return jnp.dot(a.astype(BF16), b, preferred_element_type=F32)


def _const_spec(shape):
    zeros = (0,) * len(shape)
    return pl.BlockSpec(shape, lambda *_: zeros, pipeline_mode=pl.Buffered(1))


def _swiglu_half(x, g, wg_ref, wu_ref, wd_ref):
    h = _rms(x, g).astype(BF16)
    gate = jnp.dot(h, wg_ref[...], preferred_element_type=F32)
    up = jnp.dot(h, wu_ref[...], preferred_element_type=F32)
    act = gate * _sigmoid(gate) * up
    return x + 0.5 * _bdot(act, wd_ref[...])


def _ffn_kernel(x_ref, g_ref, wg_ref, wu_ref, wd_ref, o_ref):
    o_ref[...] = _swiglu_half(x_ref[...], g_ref[...], wg_ref, wu_ref, wd_ref)


def _ffn(x, g, wg, wu, wd, tm):
    m, d = x.shape
    row = pl.BlockSpec((tm, d), lambda i: (i, 0))
    return pl.pallas_call(
        _ffn_kernel,
        grid=(m // tm,),
        in_specs=[row, _const_spec(g.shape), _const_spec(wg.shape),
                  _const_spec(wu.shape), _const_spec(wd.shape)],
        out_specs=row,
        out_shape=jax.ShapeDtypeStruct((m, d), F32),
        compiler_params=pltpu.CompilerParams(
            dimension_semantics=("arbitrary",), vmem_limit_bytes=VMEM_LIMIT),
        name="ffn",
    )(x, g, wg, wu, wd)


def _mix_in_kernel(x_ref, nm_ref, w_in_ref, mu_ref, sh0_ref, c0a_ref, c0b_ref,
                   wdec_ref, wicl_ref, wgat_ref, dbias_ref, ibias_ref, cw_ref,
                   r_ref, k_ref, v_ref, al_ref, ld_ref, gg_ref, gyb_ref,
                   sh_out_ref, cv_out_ref, *scratch, decode, d, shift_cols):
    tm = x_ref.shape[0]
    h = _rms(x_ref[...], nm_ref[...]).astype(BF16)
    z_sh = jnp.dot(h, w_in_ref[:, 0:shift_cols], preferred_element_type=F32)
    z_cv = jnp.dot(h, w_in_ref[:, shift_cols:shift_cols + 3 * d],
                   preferred_element_type=F32)
    z_gt = jnp.dot(h, w_in_ref[:, shift_cols + 3 * d:shift_cols + 5 * d],
                   preferred_element_type=F32)
    cu = z_cv[:, d:2 * d] * z_cv[:, 2 * d:3 * d]

    if decode:
        z_prev = sh0_ref[...]
        cu_m2 = c0a_ref[...]
        cu_m1 = c0b_ref[...]
        sh_out_ref[...] = z_sh
        cv_out_ref[...] = cu
    else:
        carry_sh, carry_cu = scratch

        @pl.when(pl.program_id(1) == 0)
        def _():
            carry_sh[0:1, :] = sh0_ref[0]
            carry_cu[0:1, :] = c0a_ref[0]
            carry_cu[1:2, :] = c0b_ref[0]

        row = lax.broadcasted_iota(jnp.int32, (tm, 1), 0)
        z_prev = jnp.where(row == 0, carry_sh[0:1, :], pltpu.roll(z_sh, 1, 0))
        cu_m1 = jnp.where(row == 0, carry_cu[1:2, :], pltpu.roll(cu, 1, 0))
        cu_m2 = jnp.where(row == 0, carry_cu[0:1, :],
                          jnp.where(row == 1, carry_cu[1:2, :], pltpu.roll(cu, 2, 0)))
        carry_sh[0:1, :] = z_sh[tm - 1:tm, :]
        carry_cu[0:2, :] = cu[tm - 2:tm, :]
        sh_out_ref[0] = z_sh[tm - 1:tm, :]
        cv_out_ref[0] = cu[tm - 2:tm, :]

    zs = z_sh + mu_ref[...] * (z_prev - z_sh)
    r_ref[...] = zs[:, 0:d]
    k_ref[...] = zs[:, d:2 * d]
    v_ref[...] = zs[:, 2 * d:3 * d]
    tail = zs[:, 3 * d:shift_cols]
    dec = dbias_ref[...] + _bdot(jnp.tanh(tail), wdec_ref[...])
    ld_ref[...] = -jnp.exp(-_softplus(-dec) - 0.5)
    al_ref[...] = _sigmoid(ibias_ref[...] + _bdot(tail, wicl_ref[...]))
    gate = _bdot(_sigmoid(tail), wgat_ref[...])
    gg_ref[...] = gate * _sigmoid(z_gt[:, 0:d])
    conv = cu_m2 * cw_ref[0:1, :] + cu_m1 * cw_ref[1:2, :] + cu * cw_ref[2:3, :]
    gyb_ref[...] = _sigmoid(z_gt[:, d:2 * d]) * (z_cv[:, 0:d] * conv)


def _mix_in(x, nm, w_in, mu, sh0, c0a, c0b, wdec, wicl, wgat, dbias, ibias, cw,
            *, decode, batch, seq, tm):
    m, d = x.shape
    shift_cols = mu.shape[1]
    consts = [nm, w_in, mu]
    lora = [wdec, wicl, wgat, dbias, ibias, cw]
    if decode:
        grid = (1,)
        row = lambda w: pl.BlockSpec((m, w), lambda i: (0, 0))
        state_specs = [row(shift_cols), row(d), row(d)]
        state_out = [row(shift_cols), row(d)]
        state_shapes = [jax.ShapeDtypeStruct((m, shift_cols), F32),
                        jax.ShapeDtypeStruct((m, d), F32)]
        scratch = []
        sem = ("arbitrary",)
    else:
        nt = seq // tm
        grid = (batch, nt)
        row = lambda w: pl.BlockSpec((tm, w), lambda b, t: (b * nt + t, 0))
        per_b = lambda r, w: pl.BlockSpec((1, r, w), lambda b, t: (b, 0, 0))
        state_specs = [per_b(1, shift_cols), per_b(1, d), per_b(1, d)]
        state_out = [per_b(1, shift_cols), per_b(2, d)]
        state_shapes = [jax.ShapeDtypeStruct((batch, 1, shift_cols), F32),
                        jax.ShapeDtypeStruct((batch, 2, d), F32)]
        scratch = [pltpu.VMEM((8, shift_cols), F32), pltpu.VMEM((8, d), F32)]
        sem = ("arbitrary", "arbitrary")
    tok = jax.ShapeDtypeStruct((m, d), F32)
    return pl.pallas_call(
        functools.partial(_mix_in_kernel, decode=decode, d=d, shift_cols=shift_cols),
        grid=grid,
        in_specs=([row(d)] + [_const_spec(a.shape) for a in consts] + state_specs
                  + [_const_spec(a.shape) for a in lora]),
        out_specs=[row(d)] * 7 + state_out,
        out_shape=[tok] * 7 + state_shapes,
        scratch_shapes=scratch,
        compiler_params=pltpu.CompilerParams(
            dimension_semantics=sem, vmem_limit_bytes=VMEM_LIMIT),
        name="mix_in_decode" if decode else "mix_in",
    )(x, *consts, sh0, c0a, c0b, *lora)


def _head_sum(x, first):
    x1 = jnp.where(first, x, 0.0)
    s1 = jnp.sum(x1, axis=-1, keepdims=True)
    s2 = jnp.sum(x - x1, axis=-1, keepdims=True)
    return jnp.where(first, s1, s2)


def _stack_heads(x):
    lane = lax.broadcasted_iota(jnp.int32, x.shape, 1) % LANES
    first = lane < HEAD
    return jnp.concatenate([jnp.where(first, x, 0.0), jnp.where(first, 0.0, x)], axis=0)


def _apply(mat, x):
    return jnp.dot(mat, _stack_heads(x), precision=HIGHEST, preferred_element_type=F32)


def _tdot(a, b):
    return lax.dot_general(a, b, (((0,), (0,)), ((), ())), precision=HIGHEST,
                           preferred_element_type=F32)


def _wkv_kernel(r_ref, k_ref, v_ref, al_ref, ld_ref, kk_ref, ka_ref, rk_ref,
                gw_ref, gb_ref, y_ref, st_ref, h_ref):
    c = pl.program_id(1)
    n_pairs = h_ref.shape[0]

    @pl.when(c == 0)
    def _():
        h_ref[...] = jnp.zeros_like(h_ref)

    C = CHUNK
    row = lax.broadcasted_iota(jnp.int32, (C, LANES), 0)
    col = lax.broadcasted_iota(jnp.int32, (C, LANES), 1)
    first = col < HEAD
    j = col % HEAD
    strict = j < row
    incl = j <= row
    cs_r = lax.broadcasted_iota(jnp.int32, (C, C), 0)
    cs_c = lax.broadcasted_iota(jnp.int32, (C, C), 1)
    tri = (cs_c <= cs_r).astype(F32)
    sq_r = lax.broadcasted_iota(jnp.int32, (LANES, LANES), 0)
    sq_c = lax.broadcasted_iota(jnp.int32, (LANES, LANES), 1)
    same_head = (sq_r // HEAD) == (sq_c // HEAD)
    eye = sq_r == sq_c

    for p in range(n_pairs):
        sl = slice(p * LANES, (p + 1) * LANES)
        r = r_ref[:, sl]
        k = k_ref[:, sl]
        v = v_ref[:, sl]
        al = al_ref[:, sl]
        ld = ld_ref[:, sl]
        lp = jnp.dot(tri, ld, precision=HIGHEST, preferred_element_type=F32)
        lp_end = lp[C - 1:C, :]
        kkr = k * kk_ref[:, sl]
        nrm = jnp.sqrt(_head_sum(kkr * kkr, first))
        kk = kkr / jnp.maximum(nrm, NORM_EPS)
        kmod = k * (1.0 + (al - 1.0) * ka_ref[:, sl])
        bv = kk * al
        e_neg = jnp.exp(-lp)
        e_end = jnp.exp(lp_end - lp)
        at = kk * jnp.exp(lp - ld)
        rt = r * jnp.exp(lp)
        bt = bv * e_neg
        kt = kmod * e_neg
        bh = bv * e_end
        kh = kmod * e_end

        sc = lax.dot_general(
            jnp.concatenate([at, rt], axis=0),
            jnp.concatenate([_stack_heads(bt), _stack_heads(kt)], axis=0),
            (((1,), (1,)), ((), ())), precision=HIGHEST, preferred_element_type=F32)
        lab = jnp.where(strict, sc[0:C, 0:LANES], 0.0)
        lak = jnp.where(strict, sc[0:C, LANES:2 * LANES], 0.0)
        mrb = jnp.where(incl, sc[C:2 * C, 0:LANES], 0.0)
        mrk = jnp.where(incl, sc[C:2 * C, LANES:2 * LANES], 0.0)

        inv = jnp.where(j == row, 1.0, 0.0) - jnp.where((j // 2) == (row // 2), lab, 0.0)
        s = 2
        while s < C:
            off = jnp.where(((j // (2 * s)) == (row // (2 * s))) & ((j // s) != (row // s)),
                            lab, 0.0)
            inv = inv - _apply(_apply(inv, off), inv)
            s *= 2

        xv = _apply(jnp.concatenate([lak, mrk], axis=0), v)
        uw = _apply(inv, jnp.concatenate([at, xv[0:C]], axis=1))
        mu = _apply(mrb, uw)
        q = rt - mu[:, 0:LANES]
        y0 = xv[C:2 * C] - mu[:, LANES:2 * LANES]
        g_c = _tdot(jnp.concatenate([kh, -bh], axis=0),
                    jnp.concatenate([v, uw[:, LANES:2 * LANES]], axis=0))
        a_c = jnp.where(eye, jnp.exp(lp_end), 0.0) - _tdot(bh, uw[:, 0:LANES])
        a_c = jnp.where(same_head, a_c, 0.0)
        g_c = jnp.where(same_head, g_c, 0.0)

        h0 = h_ref[p]
        y = jnp.dot(q, h0, precision=HIGHEST, preferred_element_type=F32) + y0
        h_ref[p] = jnp.dot(a_c, h0, precision=HIGHEST, preferred_element_type=F32) + g_c

        mean = _head_sum(y, first) * (1.0 / HEAD)
        yc = y - mean
        var = _head_sum(yc * yc, first) * (1.0 / HEAD)
        yn = yc * lax.rsqrt(var + GN_EPS) * gw_ref[:, sl] + gb_ref[:, sl]
        bonus = _head_sum(r * kmod * rk_ref[:, sl], first) * v
        y_ref[:, sl] = yn + bonus

    @pl.when(c == pl.num_programs(1) - 1)
    def _():
        st_ref[0] = h_ref[...]


def _wkv(r, k, v, al, ld, kk, ka, rk, gw, gb, *, batch, seq):
    m, d = r.shape
    nc = seq // CHUNK
    n_pairs = d // LANES
    tok = pl.BlockSpec((CHUNK, d), lambda b, c: (b * nc + c, 0))
    par = pl.BlockSpec((1, d), lambda b, c: (0, 0))
    return pl.pallas_call(
        _wkv_kernel,
        grid=(batch, nc),
        in_specs=[tok] * 5 + [par] * 5,
        out_specs=[tok, pl.BlockSpec((1, n_pairs, LANES, LANES), lambda b, c: (b, 0, 0, 0))],
        out_shape=[jax.ShapeDtypeStruct((m, d), F32),
                   jax.ShapeDtypeStruct((batch, n_pairs, LANES, LANES), F32)],
        scratch_shapes=[pltpu.VMEM((n_pairs, LANES, LANES), F32)],
        compiler_params=pltpu.CompilerParams(
            dimension_semantics=("arbitrary", "arbitrary"), vmem_limit_bytes=VMEM_LIMIT),
        name="wkv",
    )(r, k, v, al, ld, kk, ka, rk, gw, gb)


def _wkv_step_kernel(r_ref, k_ref, v_ref, al_ref, ld_ref, kk_ref, ka_ref, rk_ref,
                     gw_ref, gb_ref, s_ref, y_ref, so_ref):
    r = r_ref[...]
    k = k_ref[...]
    v = v_ref[...]
    al = al_ref[...]
    kkr = k * kk_ref[...]
    nrm = jnp.sqrt(jnp.sum(kkr * kkr, axis=-1, keepdims=True))
    kk = kkr / jnp.maximum(nrm, NORM_EPS)
    kmod = k * (1.0 + (al - 1.0) * ka_ref[...])
    bv = kk * al
    decay = jnp.exp(ld_ref[...])
    s = s_ref[...]
    eye = (lax.broadcasted_iota(jnp.int32, (HEAD, HEAD), 0)
           == lax.broadcasted_iota(jnp.int32, (HEAD, HEAD), 1))
    sa = jnp.sum(s * kk, axis=-1, keepdims=True)
    v_col = jnp.sum(jnp.where(eye, v, 0.0), axis=-1, keepdims=True)
    s_new = s * decay - sa * bv + v_col * kmod
    so_ref[...] = s_new
    y_col = jnp.sum(s_new * r, axis=-1, keepdims=True)
    y = jnp.sum(jnp.where(eye, y_col, 0.0), axis=-2, keepdims=True)
    mean = jnp.mean(y, axis=-1, keepdims=True)
    yc = y - mean
    var = jnp.mean(yc * yc, axis=-1, keepdims=True)
    yn = yc * lax.rsqrt(var + GN_EPS) * gw_ref[...] + gb_ref[...]
    bonus = jnp.sum(r * kmod * rk_ref[...], axis=-1, keepdims=True) * v
    y_ref[...] = yn + bonus


def _wkv_step(r, k, v, al, ld, kk, ka, rk, gw, gb, s0, bb):
    b, nh = s0.shape[0], s0.shape[1]
    tok = pl.BlockSpec((bb, nh, 1, HEAD), lambda i: (i, 0, 0, 0))
    par = pl.BlockSpec((1, nh, 1, HEAD), lambda i: (0, 0, 0, 0))
    st = pl.BlockSpec((bb, nh, HEAD, HEAD), lambda i: (i, 0, 0, 0))
    return pl.pallas_call(
        _wkv_step_kernel,
        grid=(b // bb,),
        in_specs=[tok] * 5 + [par] * 5 + [st],
        out_specs=[tok, st],
        out_shape=[jax.ShapeDtypeStruct((b, nh, 1, HEAD), F32),
                   jax.ShapeDtypeStruct(s0.shape, F32)],
        compiler_params=pltpu.CompilerParams(
            dimension_semantics=("arbitrary",), vmem_limit_bytes=VMEM_LIMIT),
        name="wkv_step",
    )(r, k, v, al, ld, kk, ka, rk, gw, gb, s0)


def _tail_kernel(x_ref, ya_ref, gg_ref, gyb_ref, p_ref, wo_ref, nf_ref, wg_ref, wu_ref,
                 wd_ref, npl_ref, wpg_ref, wpp_ref, nfin_ref, o_ref, *, final):
    merged = ya_ref[...] * gg_ref[...] + gyb_ref[...]
    x = x_ref[...] + _bdot(merged, wo_ref[...])
    x = _swiglu_half(x, nf_ref[...], wg_ref, wu_ref, wd_ref)
    hp = _rms(x, npl_ref[...])
    x = x + _sigmoid(_bdot(hp, wpg_ref[...])) * _bdot(p_ref[...], wpp_ref[...])
    o_ref[...] = _rms(x, nfin_ref[...]) if final else x


def _tail(x, ya, gg, gyb, p, wo, nf, wg, wu, wd, npl, wpg, wpp, nfin, *, final, tm):
    m, d = x.shape
    row = lambda w: pl.BlockSpec((tm, w), lambda i: (i, 0))
    weights = [wo, nf, wg, wu, wd, npl, wpg, wpp, nfin]
    return pl.pallas_call(
        functools.partial(_tail_kernel, final=final),
        grid=(m // tm,),
        in_specs=[row(d)] * 4 + [row(p.shape[1])] + [_const_spec(w.shape) for w in weights],
        out_specs=row(d),
        out_shape=jax.ShapeDtypeStruct((m, d), F32),
        compiler_params=pltpu.CompilerParams(
            dimension_semantics=("arbitrary",), vmem_limit_bytes=VMEM_LIMIT),
        name="tail",
    )(x, ya, gg, gyb, p, *weights)


def _layer(x, p, s0, sh0, cv0, w, *, decode, batch, seq, final):
    d = x.shape[1]
    nh = d // HEAD
    x1 = _ffn(x, w["norm_ffn1"], w["ffn1_gate"], w["ffn1_up"], w["ffn1_down"],
              tm=min(512, x.shape[0]))
    if decode:
        sh_in, c0a, c0b = sh0, cv0[:, 0], cv0[:, 1]
    else:
        sh_in, c0a, c0b = sh0[:, None], cv0[:, 0:1], cv0[:, 1:2]
    r, k, v, al, ld, gg, gyb, sh_new, cv_new = _mix_in(
        x1, w["norm_mix"], w["w_in"], w["mu"], sh_in, c0a, c0b, w["wdec"], w["wicl"],
        w["wgat"], w["decay_bias"], w["iclr_bias"], w["conv_w"],
        decode=decode, batch=batch, seq=seq, tm=min(256, x.shape[0]))
    pars = [w["k_k"], w["k_a"], w["r_k"], w["gn_w"], w["gn_b"]]
    if decode:
        hd = lambda a: a.reshape(a.shape[0], nh, 1, HEAD)
        ya, s_new = _wkv_step(*[hd(a) for a in (r, k, v, al, ld)], *[hd(a) for a in pars],
                              s0, bb=8)
        ya = ya.reshape(batch, d)
        sh_new = sh_new
        cv_new = jnp.stack([cv0[:, 1], cv_new], axis=1)
    else:
        ya, h_pairs = _wkv(r, k, v, al, ld, *pars, batch=batch, seq=seq)
        hp = h_pairs.reshape(batch, nh // 2, 2, HEAD, 2, HEAD)
        s_new = jnp.stack([hp[:, :, 0, :, 0, :], hp[:, :, 1, :, 1, :]], axis=2)
        s_new = jnp.swapaxes(s_new.reshape(batch, nh, HEAD, HEAD), -1, -2)
        sh_new = sh_new[:, 0]
    x_out = _tail(x1, ya, gg, gyb, p, w["w_out"], w["norm_ffn2"], w["ffn2_gate"],
                  w["ffn2_up"], w["ffn2_down"], w["norm_ple"], w["ple_gate"], w["ple_proj"],
                  w["norm_final"], final=final, tm=min(256, x.shape[0]))
    return x_out, s_new, sh_new, cv_new


def kernel(x_prompt, x_sample, p_prompt, p_sample, state_wkv, state_shift, state_conv,
           norm_ffn1, w_ffn1_gate, w_ffn1_up, w_ffn1_down, norm_mix, w_in, mu_shift,
           w_decay_up, decay_bias, w_iclr_up, iclr_bias, w_gate_up, k_k, k_a, r_k, gn_w,
           gn_b, conv_w, w_out, norm_ffn2, w_ffn2_gate, w_ffn2_up, w_ffn2_down, norm_ple,
           w_ple_gate, w_ple_proj, norm_final):
    depth = w_in.shape[0]
    bp, tp, d = x_prompt.shape
    bs, ts, _ = x_sample.shape
    assert ts == 1 and tp % CHUNK == 0
    shift_cols = mu_shift.shape[1]
    n_dec, n_icl, n_gat = w_decay_up.shape[1], w_iclr_up.shape[1], w_gate_up.shape[1]
    assert 3 * d + n_dec + n_icl + n_gat == shift_cols

    xp = x_prompt.reshape(bp * tp, d)
    xs = x_sample.reshape(bs, d)
    vec = lambda a: a.reshape(1, -1)
    outs = [[] for _ in range(6)]
    for i in range(depth):
        n_tail = shift_cols - 3 * d
        pad = lambda a, lo: jnp.zeros((n_tail, d), BF16).at[lo:lo + a.shape[0]].set(a.astype(BF16))
        w = dict(
            norm_ffn1=vec(norm_ffn1[i]), ffn1_gate=w_ffn1_gate[i].astype(BF16),
            ffn1_up=w_ffn1_up[i].astype(BF16), ffn1_down=w_ffn1_down[i].astype(BF16),
            norm_mix=vec(norm_mix[i]), w_in=w_in[i].astype(BF16), mu=vec(mu_shift[i]),
            wdec=pad(w_decay_up[i], 0), wicl=pad(w_iclr_up[i], n_dec),
            wgat=pad(w_gate_up[i], n_dec + n_icl),
            decay_bias=vec(decay_bias[i]), iclr_bias=vec(iclr_bias[i]), conv_w=conv_w[i],
            k_k=vec(k_k[i]), k_a=vec(k_a[i]), r_k=vec(r_k[i]), gn_w=vec(gn_w[i]),
            gn_b=vec(gn_b[i]), w_out=w_out[i].astype(BF16), norm_ffn2=vec(norm_ffn2[i]),
            ffn2_gate=w_ffn2_gate[i].astype(BF16), ffn2_up=w_ffn2_up[i].astype(BF16),
            ffn2_down=w_ffn2_down[i].astype(BF16), norm_ple=vec(norm_ple[i]),
            ple_gate=w_ple_gate[i].astype(BF16), ple_proj=w_ple_proj[i].astype(BF16),
            norm_final=vec(norm_final))
        final = i == depth - 1
        xp, s_p, sh_p, cv_p = _layer(
            xp, p_prompt[i].reshape(bp * tp, -1), None,
            jnp.zeros((bp, shift_cols), F32), jnp.zeros((bp, 2, d), F32), w,
            decode=False, batch=bp, seq=tp, final=final)
        xs, s_s, sh_s, cv_s = _layer(
            xs, p_sample[i].reshape(bs, -1), state_wkv[i], state_shift[i], state_conv[i], w,
            decode=True, batch=bs, seq=1, final=final)
        for lst, val in zip(outs, (s_p, sh_p, cv_p, s_s, sh_s, cv_s)):
            lst.append(val)
    stk = [jnp.stack(o, 0) for o in outs]
    return (xp.reshape(bp, tp, d), xs.reshape(bs, ts, d), stk[0], stk[1], stk[2],
            stk[3], stk[4], stk[5])
```

```python
import functools

import jax
import jax.numpy as jnp
from jax import lax
from jax.experimental import pallas as pl
from jax.experimental.pallas import tpu as pltpu

F32 = jnp.float32
BF16 = jnp.bfloat16

HEAD = 64
LANES = 128
CHUNK = 64
WKV_GROUP = 8
RMS_EPS = 1e-6
GN_EPS = 64e-5
NORM_EPS = 1e-12
VMEM_LIMIT = 56 * 1024 * 1024


def _rms(x, g):
    return x * lax.rsqrt(jnp.mean(x * x, axis=-1, keepdims=True) + RMS_EPS) * g


def _sigmoid(x):
    return 1.0 / (1.0 + jnp.exp(-x))


def _softplus(x):
    return jnp.maximum(x, 0.0) + jnp.log(1.0 + jnp.exp(-jnp.abs(x)))


def _bdot(a, b):
    return jnp.dot(a.astype(BF16), b, preferred_element_type=F32)


def _const_spec(shape):
    zeros = (0,) * len(shape)
    return pl.BlockSpec(shape, lambda *_: zeros, pipeline_mode=pl.Buffered(1))


def _swiglu_half(x, g, wg_ref, wu_ref, wd_ref):
    h = _rms(x, g).astype(BF16)
    gate = jnp.dot(h, wg_ref[...], preferred_element_type=F32)
    up = jnp.dot(h, wu_ref[...], preferred_element_type=F32)
    act = gate * _sigmoid(gate) * up
    return x + 0.5 * _bdot(act, wd_ref[...])


def _ffn_kernel(x_ref, g_ref, wg_ref, wu_ref, wd_ref, o_ref):
    o_ref[...] = _swiglu_half(x_ref[...], g_ref[...], wg_ref, wu_ref, wd_ref)


def _ffn(x, g, wg, wu, wd, tm):
    m, d = x.shape
    row = pl.BlockSpec((tm, d), lambda i: (i, 0))
    return pl.pallas_call(
        _ffn_kernel,
        grid=(m // tm,),
        in_specs=[row, _const_spec(g.shape), _const_spec(wg.shape),
                  _const_spec(wu.shape), _const_spec(wd.shape)],
        out_specs=row,
        out_shape=jax.ShapeDtypeStruct((m, d), F32),
        compiler_params=pltpu.CompilerParams(
            dimension_semantics=("arbitrary",), vmem_limit_bytes=VMEM_LIMIT),
        name="ffn",
    )(x, g, wg, wu, wd)


def _mix_in_kernel(x_ref, nm_ref, w_in_ref, mu_ref, sh0_ref, c0a_ref, c0b_ref,
                   wdec_ref, wicl_ref, wgat_ref, dbias_ref, ibias_ref, cw_ref,
                   r_ref, k_ref, v_ref, al_ref, ld_ref, gg_ref, gyb_ref,
                   sh_out_ref, cv_out_ref, *scratch, decode, d, shift_cols):
    tm = x_ref.shape[0]
    h = _rms(x_ref[...], nm_ref[...]).astype(BF16)
    z_sh = jnp.dot(h, w_in_ref[:, 0:shift_cols], preferred_element_type=F32)
    z_cv = jnp.dot(h, w_in_ref[:, shift_cols:shift_cols + 3 * d],
                   preferred_element_type=F32)
    z_gt = jnp.dot(h, w_in_ref[:, shift_cols + 3 * d:shift_cols + 5 * d],
                   preferred_element_type=F32)
    cu = z_cv[:, d:2 * d] * z_cv[:, 2 * d:3 * d]

    if decode:
        z_prev = sh0_ref[...]
        cu_m2 = c0a_ref[...]
        cu_m1 = c0b_ref[...]
        sh_out_ref[...] = z_sh
        cv_out_ref[...] = cu
    else:
        carry_sh, carry_cu = scratch

        @pl.when(pl.program_id(1) == 0)
        def _():
            carry_sh[0:1, :] = sh0_ref[0]
            carry_cu[0:1, :] = c0a_ref[0]
            carry_cu[1:2, :] = c0b_ref[0]

        row = lax.broadcasted_iota(jnp.int32, (tm, 1), 0)
        z_prev = jnp.where(row == 0, carry_sh[0:1, :], pltpu.roll(z_sh, 1, 0))
        cu_m1 = jnp.where(row == 0, carry_cu[1:2, :], pltpu.roll(cu, 1, 0))
        cu_m2 = jnp.where(row == 0, carry_cu[0:1, :],
                          jnp.where(row == 1, carry_cu[1:2, :], pltpu.roll(cu, 2, 0)))
        carry_sh[0:1, :] = z_sh[tm - 1:tm, :]
        carry_cu[0:2, :] = cu[tm - 2:tm, :]
        sh_out_ref[0] = z_sh[tm - 1:tm, :]
        cv_out_ref[0] = cu[tm - 2:tm, :]

    zs = z_sh + mu_ref[...] * (z_prev - z_sh)
    r_ref[...] = zs[:, 0:d]
    k_ref[...] = zs[:, d:2 * d]
    v_ref[...] = zs[:, 2 * d:3 * d]
    tail = zs[:, 3 * d:shift_cols]
    dec = dbias_ref[...] + _bdot(jnp.tanh(tail), wdec_ref[...])
    ld_ref[...] = -jnp.exp(-_softplus(-dec) - 0.5)
    al_ref[...] = _sigmoid(ibias_ref[...] + _bdot(tail, wicl_ref[...]))
    gate = _bdot(_sigmoid(tail), wgat_ref[...])
    gg_ref[...] = gate * _sigmoid(z_gt[:, 0:d])
    conv = cu_m2 * cw_ref[0:1, :] + cu_m1 * cw_ref[1:2, :] + cu * cw_ref[2:3, :]
    gyb_ref[...] = _sigmoid(z_gt[:, d:2 * d]) * (z_cv[:, 0:d] * conv)


def _mix_in(x, nm, w_in, mu, sh0, c0a, c0b, wdec, wicl, wgat, dbias, ibias, cw,
            *, decode, batch, seq, tm):
    m, d = x.shape
    shift_cols = mu.shape[1]
    consts = [nm, w_in, mu]
    lora = [wdec, wicl, wgat, dbias, ibias, cw]
    if decode:
        grid = (1,)
        row = lambda w: pl.BlockSpec((m, w), lambda i: (0, 0))
        state_specs = [row(shift_cols), row(d), row(d)]
        state_out = [row(shift_cols), row(d)]
        state_shapes = [jax.ShapeDtypeStruct((m, shift_cols), F32),
                        jax.ShapeDtypeStruct((m, d), F32)]
        scratch = []
        sem = ("arbitrary",)
    else:
        nt = seq // tm
        grid = (batch, nt)
        row = lambda w: pl.BlockSpec((tm, w), lambda b, t: (b * nt + t, 0))
        per_b = lambda r, w: pl.BlockSpec((1, r, w), lambda b, t: (b, 0, 0))
        state_specs = [per_b(1, shift_cols), per_b(1, d), per_b(1, d)]
        state_out = [per_b(1, shift_cols), per_b(2, d)]
        state_shapes = [jax.ShapeDtypeStruct((batch, 1, shift_cols), F32),
                        jax.ShapeDtypeStruct((batch, 2, d), F32)]
        scratch = [pltpu.VMEM((8, shift_cols), F32), pltpu.VMEM((8, d), F32)]
        sem = ("arbitrary", "arbitrary")
    tok = jax.ShapeDtypeStruct((m, d), F32)
    return pl.pallas_call(
        functools.partial(_mix_in_kernel, decode=decode, d=d, shift_cols=shift_cols),
        grid=grid,
        in_specs=([row(d)] + [_const_spec(a.shape) for a in consts] + state_specs
                  + [_const_spec(a.shape) for a in lora]),
        out_specs=[row(d)] * 7 + state_out,
        out_shape=[tok] * 7 + state_shapes,
        scratch_shapes=scratch,
        compiler_params=pltpu.CompilerParams(
            dimension_semantics=sem, vmem_limit_bytes=VMEM_LIMIT),
        name="mix_in_decode" if decode else "mix_in",
    )(x, *consts, sh0, c0a, c0b, *lora)


def _head_sum(x, first):
    x1 = jnp.where(first, x, 0.0)
    s1 = jnp.sum(x1, axis=-1, keepdims=True)
    s2 = jnp.sum(x - x1, axis=-1, keepdims=True)
    return jnp.where(first, s1, s2)


def _stack_heads(x):
    x = x.astype(BF16)
    lane = lax.broadcasted_iota(jnp.int32, x.shape, 1) % LANES
    first = lane < HEAD
    zero = jnp.zeros_like(x)
    return jnp.concatenate([jnp.where(first, x, zero), jnp.where(first, zero, x)], axis=0)


def _apply(mat, x):
    return jnp.dot(mat.astype(BF16), _stack_heads(x), preferred_element_type=F32)


def _split2(x):
    hi = x.astype(BF16)
    return hi, (x - hi.astype(F32)).astype(BF16)


def _split3(x):
    hi = x.astype(BF16)
    rest = x - hi.astype(F32)
    mid = rest.astype(BF16)
    return hi, mid, (rest - mid.astype(F32)).astype(BF16)


def _wkv_kernel(r_ref, k_ref, v_ref, al_ref, ld_ref, kk_ref, ka_ref, rk_ref,
                gw_ref, gb_ref, y_ref, st_ref, h_ref, *, group):
    c = pl.program_id(1)
    n_pairs = h_ref.shape[0]

    @pl.when(c == 0)
    def _():
        h_ref[...] = jnp.zeros_like(h_ref)

    C = CHUNK
    row = lax.broadcasted_iota(jnp.int32, (C, LANES), 0)
    col = lax.broadcasted_iota(jnp.int32, (C, LANES), 1)
    first = col < HEAD
    j = col % HEAD
    strict = j < row
    incl = j <= row
    cs_r = lax.broadcasted_iota(jnp.int32, (C, 3 * C), 0)
    cs_c = lax.broadcasted_iota(jnp.int32, (C, 3 * C), 1) % C
    tri3 = (cs_c <= cs_r).astype(BF16)
    sq_r = lax.broadcasted_iota(jnp.int32, (LANES, LANES), 0)
    sq_c = lax.broadcasted_iota(jnp.int32, (LANES, LANES), 1)
    same_head = (sq_r // HEAD) == (sq_c // HEAD)
    eye = sq_r == sq_c

    def prep(p):
        sl = slice(p * LANES, (p + 1) * LANES)
        r = r_ref[:, sl]
        k = k_ref[:, sl]
        v = v_ref[:, sl]
        al = al_ref[:, sl]
        ld = ld_ref[:, sl]
        lp = jnp.dot(tri3, jnp.concatenate(_split3(ld), axis=0), preferred_element_type=F32)
        lp_end = lp[C - 1:C, :]
        kkr = k * kk_ref[:, sl]
        nrm = jnp.sqrt(_head_sum(kkr * kkr, first))
        kk = kkr / jnp.maximum(nrm, NORM_EPS)
        kmod = k * (1.0 + (al - 1.0) * ka_ref[:, sl])
        bv = kk * al
        e_neg = jnp.exp(-lp)
        e_end = jnp.exp(lp_end - lp)
        st = dict(sl=sl, v=v, at=kk * jnp.exp(lp - ld), rt=r * jnp.exp(lp),
                  bh=bv * e_end, kh=kmod * e_end, p_end=jnp.exp(lp_end),
                  bonus=_head_sum(r * kmod * rk_ref[:, sl], first) * v)
        sc = lax.dot_general(
            jnp.concatenate([st["at"], st["rt"]], axis=0).astype(BF16),
            jnp.concatenate([_stack_heads(bv * e_neg), _stack_heads(kmod * e_neg)], axis=0),
            (((1,), (1,)), ((), ())), preferred_element_type=F32)
        st["lab"] = jnp.where(strict, sc[0:C, 0:LANES], 0.0)
        st["lak"] = jnp.where(strict, sc[0:C, LANES:2 * LANES], 0.0)
        st["mrb"] = jnp.where(incl, sc[C:2 * C, 0:LANES], 0.0)
        st["mrk"] = jnp.where(incl, sc[C:2 * C, LANES:2 * LANES], 0.0)
        st["inv"] = (jnp.where(j == row, 1.0, 0.0)
                     - jnp.where((j // 2) == (row // 2), st["lab"], 0.0))
        return st

    def double_inverse(st, s):
        off = jnp.where(((j // (2 * s)) == (row // (2 * s))) & ((j // s) != (row // s)),
                        st["lab"], 0.0)
        st["inv"] = st["inv"] - _apply(_apply(st["inv"], off), st["inv"])

    def values(st):
        st["xv"] = _apply(jnp.concatenate([st["lak"], st["mrk"]], axis=0), st["v"])
        lhs_t = jnp.concatenate([st["kh"], -st["bh"]], axis=0).T
        l_hi, l_lo = _split2(lhs_t)
        st["lhs3"] = jnp.concatenate([l_hi, l_hi, l_lo], axis=1)

    def solve(st):
        st["uw"] = _apply(st["inv"], jnp.concatenate([st["at"], st["xv"][0:C]], axis=1))

    def outputs(st):
        uw, v = st["uw"], st["v"]
        mu = _apply(st["mrb"], uw)
        st["q"] = (st["rt"] - mu[:, 0:LANES]).astype(BF16)
        st["y0"] = st["xv"][C:2 * C] - mu[:, LANES:2 * LANES]
        rhs = jnp.concatenate(
            [jnp.concatenate([v, jnp.zeros_like(v)], axis=1),
             jnp.concatenate([uw[:, LANES:2 * LANES], -uw[:, 0:LANES]], axis=1)], axis=0)
        r_hi, r_lo = _split2(rhs)
        gb = jnp.dot(st["lhs3"], jnp.concatenate([r_hi, r_lo, r_hi], axis=0),
                     preferred_element_type=F32)
        st["g_c"] = jnp.where(same_head, gb[:, 0:LANES], 0.0)
        st["a_c"] = jnp.where(same_head, jnp.where(eye, st["p_end"], 0.0)
                              - gb[:, LANES:2 * LANES], 0.0).astype(BF16)

    def advance(p, st):
        h0_b = h_ref[p].astype(BF16)
        st["y"] = jnp.dot(st["q"], h0_b, preferred_element_type=F32) + st["y0"]
        h_ref[p] = jnp.dot(st["a_c"], h0_b, preferred_element_type=F32) + st["g_c"]

    def normalize(st):
        sl, y = st["sl"], st["y"]
        mean = _head_sum(y, first) * (1.0 / HEAD)
        yc = y - mean
        var = _head_sum(yc * yc, first) * (1.0 / HEAD)
        yn = yc * lax.rsqrt(var + GN_EPS) * gw_ref[:, sl] + gb_ref[:, sl]
        y_ref[:, sl] = yn + st["bonus"]

    for g0 in range(0, n_pairs, group):
        pairs = range(g0, min(g0 + group, n_pairs))
        sts = [prep(p) for p in pairs]
        for st in sts:
            values(st)
        s = 2
        while s < C:
            for st in sts:
                double_inverse(st, s)
            s *= 2
        for stage in (solve, outputs):
            for st in sts:
                stage(st)
        for p, st in zip(pairs, sts):
            advance(p, st)
        for st in sts:
            normalize(st)

    @pl.when(c == pl.num_programs(1) - 1)
    def _():
        st_ref[0] = h_ref[...]


def _wkv(r, k, v, al, ld, kk, ka, rk, gw, gb, *, batch, seq):
    m, d = r.shape
    nc = seq // CHUNK
    n_pairs = d // LANES
    tok = pl.BlockSpec((CHUNK, d), lambda b, c: (b * nc + c, 0))
    par = pl.BlockSpec((1, d), lambda b, c: (0, 0))
    return pl.pallas_call(
        functools.partial(_wkv_kernel, group=WKV_GROUP),
        grid=(batch, nc),
        in_specs=[tok] * 5 + [par] * 5,
        out_specs=[tok, pl.BlockSpec((1, n_pairs, LANES, LANES), lambda b, c: (b, 0, 0, 0))],
        out_shape=[jax.ShapeDtypeStruct((m, d), F32),
                   jax.ShapeDtypeStruct((batch, n_pairs, LANES, LANES), F32)],
        scratch_shapes=[pltpu.VMEM((n_pairs, LANES, LANES), F32)],
        compiler_params=pltpu.CompilerParams(
            dimension_semantics=("arbitrary", "arbitrary"), vmem_limit_bytes=VMEM_LIMIT),
        name="wkv",
    )(r, k, v, al, ld, kk, ka, rk, gw, gb)


def _wkv_step_kernel(r_ref, k_ref, v_ref, al_ref, ld_ref, kk_ref, ka_ref, rk_ref,
                     gw_ref, gb_ref, s_ref, y_ref, so_ref):
    r = r_ref[...]
    k = k_ref[...]
    v = v_ref[...]
    al = al_ref[...]
    kkr = k * kk_ref[...]
    nrm = jnp.sqrt(jnp.sum(kkr * kkr, axis=-1, keepdims=True))
    kk = kkr / jnp.maximum(nrm, NORM_EPS)
    kmod = k * (1.0 + (al - 1.0) * ka_ref[...])
    bv = kk * al
    decay = jnp.exp(ld_ref[...])
    s = s_ref[...]
    eye = (lax.broadcasted_iota(jnp.int32, (HEAD, HEAD), 0)
           == lax.broadcasted_iota(jnp.int32, (HEAD, HEAD), 1))
    sa = jnp.sum(s * kk, axis=-1, keepdims=True)
    v_col = jnp.sum(jnp.where(eye, v, 0.0), axis=-1, keepdims=True)
    s_new = s * decay - sa * bv + v_col * kmod
    so_ref[...] = s_new
    y_col = jnp.sum(s_new * r, axis=-1, keepdims=True)
    y = jnp.sum(jnp.where(eye, y_col, 0.0), axis=-2, keepdims=True)
    mean = jnp.mean(y, axis=-1, keepdims=True)
    yc = y - mean
    var = jnp.mean(yc * yc, axis=-1, keepdims=True)
    yn = yc * lax.rsqrt(var + GN_EPS) * gw_ref[...] + gb_ref[...]
    bonus = jnp.sum(r * kmod * rk_ref[...], axis=-1, keepdims=True) * v
    y_ref[...] = yn + bonus


def _wkv_step(r, k, v, al, ld, kk, ka, rk, gw, gb, s0, bb):
    b, nh = s0.shape[0], s0.shape[1]
    tok = pl.BlockSpec((bb, nh, 1, HEAD), lambda i: (i, 0, 0, 0))
    par = pl.BlockSpec((1, nh, 1, HEAD), lambda i: (0, 0, 0, 0))
    st = pl.BlockSpec((bb, nh, HEAD, HEAD), lambda i: (i, 0, 0, 0))
    return pl.pallas_call(
        _wkv_step_kernel,
        grid=(b // bb,),
        in_specs=[tok] * 5 + [par] * 5 + [st],
        out_specs=[tok, st],
        out_shape=[jax.ShapeDtypeStruct((b, nh, 1, HEAD), F32),
                   jax.ShapeDtypeStruct(s0.shape, F32)],
        compiler_params=pltpu.CompilerParams(
            dimension_semantics=("arbitrary",), vmem_limit_bytes=VMEM_LIMIT),
        name="wkv_step",
    )(r, k, v, al, ld, kk, ka, rk, gw, gb, s0)


def _tail_kernel(x_ref, ya_ref, gg_ref, gyb_ref, p_ref, wo_ref, nf_ref, wg_ref, wu_ref,
                 wd_ref, npl_ref, wpg_ref, wpp_ref, nfin_ref, o_ref, *, final):
    merged = ya_ref[...] * gg_ref[...] + gyb_ref[...]
    x = x_ref[...] + _bdot(merged, wo_ref[...])
    x = _swiglu_half(x, nf_ref[...], wg_ref, wu_ref, wd_ref)
    hp = _rms(x, npl_ref[...])
    x = x + _sigmoid(_bdot(hp, wpg_ref[...])) * _bdot(p_ref[...], wpp_ref[...])
    o_ref[...] = _rms(x, nfin_ref[...]) if final else x


def _tail(x, ya, gg, gyb, p, wo, nf, wg, wu, wd, npl, wpg, wpp, nfin, *, final, tm):
    m, d = x.shape
    row = lambda w: pl.BlockSpec((tm, w), lambda i: (i, 0))
    weights = [wo, nf, wg, wu, wd, npl, wpg, wpp, nfin]
    return pl.pallas_call(
        functools.partial(_tail_kernel, final=final),
        grid=(m // tm,),
        in_specs=[row(d)] * 4 + [row(p.shape[1])] + [_const_spec(w.shape) for w in weights],
        out_specs=row(d),
        out_shape=jax.ShapeDtypeStruct((m, d), F32),
        compiler_params=pltpu.CompilerParams(
            dimension_semantics=("arbitrary",), vmem_limit_bytes=VMEM_LIMIT),
        name="tail",
    )(x, ya, gg, gyb, p, *weights)


def _layer(x, p, s0, sh0, cv0, w, *, decode, batch, seq, final):
    d = x.shape[1]
    nh = d // HEAD
    x1 = _ffn(x, w["norm_ffn1"], w["ffn1_gate"], w["ffn1_up"], w["ffn1_down"],
              tm=min(512, x.shape[0]))
    if decode:
        sh_in, c0a, c0b = sh0, cv0[:, 0], cv0[:, 1]
    else:
        sh_in, c0a, c0b = sh0[:, None], cv0[:, 0:1], cv0[:, 1:2]
    r, k, v, al, ld, gg, gyb, sh_new, cv_new = _mix_in(
        x1, w["norm_mix"], w["w_in"], w["mu"], sh_in, c0a, c0b, w["wdec"], w["wicl"],
        w["wgat"], w["decay_bias"], w["iclr_bias"], w["conv_w"],
        decode=decode, batch=batch, seq=seq, tm=min(256, x.shape[0]))
    pars = [w["k_k"], w["k_a"], w["r_k"], w["gn_w"], w["gn_b"]]
    if decode:
        hd = lambda a: a.reshape(a.shape[0], nh, 1, HEAD)
        ya, s_new = _wkv_step(*[hd(a) for a in (r, k, v, al, ld)], *[hd(a) for a in pars],
                              s0, bb=8)
        ya = ya.reshape(batch, d)
        sh_new = sh_new
        cv_new = jnp.stack([cv0[:, 1], cv_new], axis=1)
    else:
        ya, h_pairs = _wkv(r, k, v, al, ld, *pars, batch=batch, seq=seq)
        hp = h_pairs.reshape(batch, nh // 2, 2, HEAD, 2, HEAD)
        s_new = jnp.stack([hp[:, :, 0, :, 0, :], hp[:, :, 1, :, 1, :]], axis=2)
        s_new = jnp.swapaxes(s_new.reshape(batch, nh, HEAD, HEAD), -1, -2)
        sh_new = sh_new[:, 0]
    x_out = _tail(x1, ya, gg, gyb, p, w["w_out"], w["norm_ffn2"], w["ffn2_gate"],
                  w["ffn2_up"], w["ffn2_down"], w["norm_ple"], w["ple_gate"], w["ple_proj"],
                  w["norm_final"], final=final, tm=min(256, x.shape[0]))
    return x_out, s_new, sh_new, cv_new


def kernel(x_prompt, x_sample, p_prompt, p_sample, state_wkv, state_shift, state_conv,
           norm_ffn1, w_ffn1_gate, w_ffn1_up, w_ffn1_down, norm_mix, w_in, mu_shift,
           w_decay_up, decay_bias, w_iclr_up, iclr_bias, w_gate_up, k_k, k_a, r_k, gn_w,
           gn_b, conv_w, w_out, norm_ffn2, w_ffn2_gate, w_ffn2_up, w_ffn2_down, norm_ple,
           w_ple_gate, w_ple_proj, norm_final):
    depth = w_in.shape[0]
    bp, tp, d = x_prompt.shape
    bs, ts, _ = x_sample.shape
    assert ts == 1 and tp % CHUNK == 0
    shift_cols = mu_shift.shape[1]
    n_dec, n_icl, n_gat = w_decay_up.shape[1], w_iclr_up.shape[1], w_gate_up.shape[1]
    assert 3 * d + n_dec + n_icl + n_gat == shift_cols

    xp = x_prompt.reshape(bp * tp, d)
    xs = x_sample.reshape(bs, d)
    vec = lambda a: a.reshape(1, -1)
    outs = [[] for _ in range(6)]
    for i in range(depth):
        n_tail = shift_cols - 3 * d
        pad = lambda a, lo: jnp.zeros((n_tail, d), BF16).at[lo:lo + a.shape[0]].set(a.astype(BF16))
        w = dict(
            norm_ffn1=vec(norm_ffn1[i]), ffn1_gate=w_ffn1_gate[i].astype(BF16),
            ffn1_up=w_ffn1_up[i].astype(BF16), ffn1_down=w_ffn1_down[i].astype(BF16),
            norm_mix=vec(norm_mix[i]), w_in=w_in[i].astype(BF16), mu=vec(mu_shift[i]),
            wdec=pad(w_decay_up[i], 0), wicl=pad(w_iclr_up[i], n_dec),
            wgat=pad(w_gate_up[i], n_dec + n_icl),
            decay_bias=vec(decay_bias[i]), iclr_bias=vec(iclr_bias[i]), conv_w=conv_w[i],
            k_k=vec(k_k[i]), k_a=vec(k_a[i]), r_k=vec(r_k[i]), gn_w=vec(gn_w[i]),
            gn_b=vec(gn_b[i]), w_out=w_out[i].astype(BF16), norm_ffn2=vec(norm_ffn2[i]),
            ffn2_gate=w_ffn2_gate[i].astype(BF16), ffn2_up=w_ffn2_up[i].astype(BF16),
            ffn2_down=w_ffn2_down[i].astype(BF16), norm_ple=vec(norm_ple[i]),
            ple_gate=w_ple_gate[i].astype(BF16), ple_proj=w_ple_proj[i].astype(BF16),
            norm_final=vec(norm_final))
        final = i == depth - 1
        xp, s_p, sh_p, cv_p = _layer(
            xp, p_prompt[i].reshape(bp * tp, -1), None,
            jnp.zeros((bp, shift_cols), F32), jnp.zeros((bp, 2, d), F32), w,
            decode=False, batch=bp, seq=tp, final=final)
        xs, s_s, sh_s, cv_s = _layer(
            xs, p_sample[i].reshape(bs, -1), state_wkv[i], state_shift[i], state_conv[i], w,
            decode=True, batch=bs, seq=1, final=final)
        for lst, val in zip(outs, (s_p, sh_p, cv_p, s_s, sh_s, cv_s)):
            lst.append(val)
    stk = [jnp.stack(o, 0) for o in outs]
    return (xp.reshape(bp, tp, d), xs.reshape(bs, ts, d), stk[0], stk[1], stk[2],
            stk[3], stk[4], stk[5])
```

```python
import functools

import jax
import jax.numpy as jnp
from jax import lax
from jax.experimental import pallas as pl
from jax.experimental.pallas import tpu as pltpu

F32 = jnp.float32
BF16 = jnp.bfloat16

HEAD = 64
LANES = 128
CHUNK = 64
WKV_GROUP = 8
RMS_EPS = 1e-6
GN_EPS = 64e-5
NORM_EPS = 1e-12
VMEM_LIMIT = 56 * 1024 * 1024


def _rms(x, g):
    return x * lax.rsqrt(jnp.mean(x * x, axis=-1, keepdims=True) + RMS_EPS) * g


def _sigmoid(x):
    return 1.0 / (1.0 + jnp.exp(-x))


def _softplus(x):
    return jnp.maximum(x, 0.0) + jnp.log(1.0 + jnp.exp(-jnp.abs(x)))


def _bdot(a, b):
    return jnp.dot(a.astype(BF16), b, preferred_element_type=F32)


def _const_spec(shape):
    zeros = (0,) * len(shape)
    return pl.BlockSpec(shape, lambda *_: zeros, pipeline_mode=pl.Buffered(1))


def _swiglu_half(x, g, wg_ref, wu_ref, wd_ref):
    h = _rms(x, g).astype(BF16)
    gate = jnp.dot(h, wg_ref[...], preferred_element_type=F32)
    up = jnp.dot(h, wu_ref[...], preferred_element_type=F32)
    act = gate * _sigmoid(gate) * up
    return x + 0.5 * _bdot(act, wd_ref[...])


def _ffn_kernel(x_ref, g_ref, wg_ref, wu_ref, wd_ref, o_ref):
    o_ref[...] = _swiglu_half(x_ref[...], g_ref[...], wg_ref, wu_ref, wd_ref)


def _ffn(x, g, wg, wu, wd, tm):
    m, d = x.shape
    row = pl.BlockSpec((tm, d), lambda i: (i, 0))
    return pl.pallas_call(
        _ffn_kernel,
        grid=(m // tm,),
        in_specs=[row, _const_spec(g.shape), _const_spec(wg.shape),
                  _const_spec(wu.shape), _const_spec(wd.shape)],
        out_specs=row,
        out_shape=jax.ShapeDtypeStruct((m, d), F32),
        compiler_params=pltpu.CompilerParams(
            dimension_semantics=("arbitrary",), vmem_limit_bytes=VMEM_LIMIT),
        name="ffn",
    )(x, g, wg, wu, wd)


def _mix_in_kernel(x_ref, nm_ref, w_in_ref, mu_ref, sh0_ref, c0a_ref, c0b_ref,
                   wdec_ref, wicl_ref, wgat_ref, dbias_ref, ibias_ref, cw_ref,
                   r_ref, k_ref, v_ref, al_ref, ld_ref, gg_ref, gyb_ref,
                   sh_out_ref, cv_out_ref, *scratch, decode, d, shift_cols):
    tm = x_ref.shape[0]
    h = _rms(x_ref[...], nm_ref[...]).astype(BF16)
    z_sh = jnp.dot(h, w_in_ref[:, 0:shift_cols], preferred_element_type=F32)
    z_cv = jnp.dot(h, w_in_ref[:, shift_cols:shift_cols + 3 * d],
                   preferred_element_type=F32)
    z_gt = jnp.dot(h, w_in_ref[:, shift_cols + 3 * d:shift_cols + 5 * d],
                   preferred_element_type=F32)
    cu = z_cv[:, d:2 * d] * z_cv[:, 2 * d:3 * d]

    if decode:
        z_prev = sh0_ref[...]
        cu_m2 = c0a_ref[...]
        cu_m1 = c0b_ref[...]
        sh_out_ref[...] = z_sh
        cv_out_ref[...] = cu
    else:
        carry_sh, carry_cu = scratch

        @pl.when(pl.program_id(1) == 0)
        def _():
            carry_sh[0:1, :] = sh0_ref[0]
            carry_cu[0:1, :] = c0a_ref[0]
            carry_cu[1:2, :] = c0b_ref[0]

        row = lax.broadcasted_iota(jnp.int32, (tm, 1), 0)
        z_prev = jnp.where(row == 0, carry_sh[0:1, :], pltpu.roll(z_sh, 1, 0))
        cu_m1 = jnp.where(row == 0, carry_cu[1:2, :], pltpu.roll(cu, 1, 0))
        cu_m2 = jnp.where(row == 0, carry_cu[0:1, :],
                          jnp.where(row == 1, carry_cu[1:2, :], pltpu.roll(cu, 2, 0)))
        carry_sh[0:1, :] = z_sh[tm - 1:tm, :]
        carry_cu[0:2, :] = cu[tm - 2:tm, :]
        sh_out_ref[0] = z_sh[tm - 1:tm, :]
        cv_out_ref[0] = cu[tm - 2:tm, :]

    zs = z_sh + mu_ref[...] * (z_prev - z_sh)
    r_ref[...] = zs[:, 0:d]
    k_ref[...] = zs[:, d:2 * d]
    v_ref[...] = zs[:, 2 * d:3 * d]
    tail = zs[:, 3 * d:shift_cols]
    dec = dbias_ref[...] + _bdot(jnp.tanh(tail), wdec_ref[...])
    ld_ref[...] = -jnp.exp(-_softplus(-dec) - 0.5)
    al_ref[...] = _sigmoid(ibias_ref[...] + _bdot(tail, wicl_ref[...]))
    gate = _bdot(_sigmoid(tail), wgat_ref[...])
    gg_ref[...] = gate * _sigmoid(z_gt[:, 0:d])
    conv = cu_m2 * cw_ref[0:1, :] + cu_m1 * cw_ref[1:2, :] + cu * cw_ref[2:3, :]
    gyb_ref[...] = _sigmoid(z_gt[:, d:2 * d]) * (z_cv[:, 0:d] * conv)


def _mix_in(x, nm, w_in, mu, sh0, c0a, c0b, wdec, wicl, wgat, dbias, ibias, cw,
            *, decode, batch, seq, tm):
    m, d = x.shape
    shift_cols = mu.shape[1]
    consts = [nm, w_in, mu]
    lora = [wdec, wicl, wgat, dbias, ibias, cw]
    if decode:
        grid = (1,)
        row = lambda w: pl.BlockSpec((m, w), lambda i: (0, 0))
        state_specs = [row(shift_cols), row(d), row(d)]
        state_out = [row(shift_cols), row(d)]
        state_shapes = [jax.ShapeDtypeStruct((m, shift_cols), F32),
                        jax.ShapeDtypeStruct((m, d), F32)]
        scratch = []
        sem = ("arbitrary",)
    else:
        nt = seq // tm
        grid = (batch, nt)
        row = lambda w: pl.BlockSpec((tm, w), lambda b, t: (b * nt + t, 0))
        per_b = lambda r, w: pl.BlockSpec((1, r, w), lambda b, t: (b, 0, 0))
        state_specs = [per_b(1, shift_cols), per_b(1, d), per_b(1, d)]
        state_out = [per_b(1, shift_cols), per_b(2, d)]
        state_shapes = [jax.ShapeDtypeStruct((batch, 1, shift_cols), F32),
                        jax.ShapeDtypeStruct((batch, 2, d), F32)]
        scratch = [pltpu.VMEM((8, shift_cols), F32), pltpu.VMEM((8, d), F32)]
        sem = ("arbitrary", "arbitrary")
    tok = jax.ShapeDtypeStruct((m, d), F32)
    return pl.pallas_call(
        functools.partial(_mix_in_kernel, decode=decode, d=d, shift_cols=shift_cols),
        grid=grid,
        in_specs=([row(d)] + [_const_spec(a.shape) for a in consts] + state_specs
                  + [_const_spec(a.shape) for a in lora]),
        out_specs=[row(d)] * 7 + state_out,
        out_shape=[tok] * 7 + state_shapes,
        scratch_shapes=scratch,
        compiler_params=pltpu.CompilerParams(
            dimension_semantics=sem, vmem_limit_bytes=VMEM_LIMIT),
        name="mix_in_decode" if decode else "mix_in",
    )(x, *consts, sh0, c0a, c0b, *lora)


def _head_sum(x, first):
    x1 = jnp.where(first, x, 0.0)
    s1 = jnp.sum(x1, axis=-1, keepdims=True)
    s2 = jnp.sum(x - x1, axis=-1, keepdims=True)
    return jnp.where(first, s1, s2)


def _stack_heads(x):
    x = x.astype(BF16)
    lane = lax.broadcasted_iota(jnp.int32, x.shape, 1) % LANES
    first = lane < HEAD
    zero = jnp.zeros_like(x)
    return jnp.concatenate([jnp.where(first, x, zero), jnp.where(first, zero, x)], axis=0)


def _apply(mat, x):
    return jnp.dot(mat.astype(BF16), _stack_heads(x), preferred_element_type=F32)


def _split2(x):
    hi = x.astype(BF16)
    return hi, (x - hi.astype(F32)).astype(BF16)


def _split3(x):
    hi = x.astype(BF16)
    rest = x - hi.astype(F32)
    mid = rest.astype(BF16)
    return hi, mid, (rest - mid.astype(F32)).astype(BF16)


def _wkv_kernel(r_ref, k_ref, v_ref, al_ref, ld_ref, kk_ref, ka_ref, rk_ref,
                gw_ref, gb_ref, y_ref, st_ref, h_ref, *, group):
    c = pl.program_id(1)
    n_pairs = h_ref.shape[0]

    @pl.when(c == 0)
    def _():
        h_ref[...] = jnp.zeros_like(h_ref)

    C = CHUNK
    row = lax.broadcasted_iota(jnp.int32, (C, LANES), 0)
    col = lax.broadcasted_iota(jnp.int32, (C, LANES), 1)
    first = col < HEAD
    j = col % HEAD
    strict = j < row
    incl = j <= row
    cs_r = lax.broadcasted_iota(jnp.int32, (C, 3 * C), 0)
    cs_c = lax.broadcasted_iota(jnp.int32, (C, 3 * C), 1) % C
    tri3 = (cs_c <= cs_r).astype(BF16)
    sq_r = lax.broadcasted_iota(jnp.int32, (LANES, LANES), 0)
    sq_c = lax.broadcasted_iota(jnp.int32, (LANES, LANES), 1)
    same_head = (sq_r // HEAD) == (sq_c // HEAD)
    eye = sq_r == sq_c

    def prep(p):
        sl = slice(p * LANES, (p + 1) * LANES)
        r = r_ref[:, sl]
        k = k_ref[:, sl]
        v = v_ref[:, sl]
        al = al_ref[:, sl]
        ld = ld_ref[:, sl]
        lp = jnp.dot(tri3, jnp.concatenate(_split3(ld), axis=0), preferred_element_type=F32)
        lp_end = lp[C - 1:C, :]
        kkr = k * kk_ref[:, sl]
        nrm = jnp.sqrt(_head_sum(kkr * kkr, first))
        kk = kkr / jnp.maximum(nrm, NORM_EPS)
        kmod = k * (1.0 + (al - 1.0) * ka_ref[:, sl])
        bv = kk * al
        e_neg = jnp.exp(-lp)
        e_end = jnp.exp(lp_end - lp)
        st = dict(sl=sl, v=v, at=kk * jnp.exp(lp - ld), rt=r * jnp.exp(lp),
                  bh=bv * e_end, kh=kmod * e_end, p_end=jnp.exp(lp_end),
                  bonus=_head_sum(r * kmod * rk_ref[:, sl], first) * v)
        sc = lax.dot_general(
            jnp.concatenate([st["at"], st["rt"]], axis=0).astype(BF16),
            jnp.concatenate([_stack_heads(bv * e_neg), _stack_heads(kmod * e_neg)], axis=0),
            (((1,), (1,)), ((), ())), preferred_element_type=F32)
        st["lab"] = jnp.where(strict, sc[0:C, 0:LANES], 0.0)
        st["lak"] = jnp.where(strict, sc[0:C, LANES:2 * LANES], 0.0)
        st["mrb"] = jnp.where(incl, sc[C:2 * C, 0:LANES], 0.0)
        st["mrk"] = jnp.where(incl, sc[C:2 * C, LANES:2 * LANES], 0.0)
        st["inv"] = (jnp.where(j == row, 1.0, 0.0)
                     - jnp.where((j // 2) == (row // 2), st["lab"], 0.0))
        return st

    def double_inverse(st, s):
        off = jnp.where(((j // (2 * s)) == (row // (2 * s))) & ((j // s) != (row // s)),
                        st["lab"], 0.0)
        st["inv"] = st["inv"] - _apply(_apply(st["inv"], off), st["inv"])

    def values(st):
        st["xv"] = _apply(jnp.concatenate([st["lak"], st["mrk"]], axis=0), st["v"])
        lhs_t = jnp.concatenate([st["kh"], -st["bh"]], axis=0).T
        l_hi, l_lo = _split2(lhs_t)
        st["lhs3"] = jnp.concatenate([l_hi, l_hi, l_lo], axis=1)

    def solve(st):
        st["uw"] = _apply(st["inv"], jnp.concatenate([st["at"], st["xv"][0:C]], axis=1))

    def outputs(st):
        uw, v = st["uw"], st["v"]
        mu = _apply(st["mrb"], uw)
        st["q"] = (st["rt"] - mu[:, 0:LANES]).astype(BF16)
        st["y0"] = st["xv"][C:2 * C] - mu[:, LANES:2 * LANES]
        rhs = jnp.concatenate(
            [jnp.concatenate([v, jnp.zeros_like(v)], axis=1),
             jnp.concatenate([uw[:, LANES:2 * LANES], -uw[:, 0:LANES]], axis=1)], axis=0)
        r_hi, r_lo = _split2(rhs)
        gb = jnp.dot(st["lhs3"], jnp.concatenate([r_hi, r_lo, r_hi], axis=0),
                     preferred_element_type=F32)
        st["g_c"] = jnp.where(same_head, gb[:, 0:LANES], 0.0)
        st["a_c"] = jnp.where(same_head, jnp.where(eye, st["p_end"], 0.0)
                              - gb[:, LANES:2 * LANES], 0.0).astype(BF16)

    def advance(p, st):
        h0_b = h_ref[p].astype(BF16)
        st["y"] = jnp.dot(st["q"], h0_b, preferred_element_type=F32) + st["y0"]
        h_ref[p] = jnp.dot(st["a_c"], h0_b, preferred_element_type=F32) + st["g_c"]

    def normalize(st):
        sl, y = st["sl"], st["y"]
        mean = _head_sum(y, first) * (1.0 / HEAD)
        yc = y - mean
        var = _head_sum(yc * yc, first) * (1.0 / HEAD)
        yn = yc * lax.rsqrt(var + GN_EPS) * gw_ref[:, sl] + gb_ref[:, sl]
        y_ref[:, sl] = yn + st["bonus"]

    for g0 in range(0, n_pairs, group):
        pairs = range(g0, min(g0 + group, n_pairs))
        sts = [prep(p) for p in pairs]
        for st in sts:
            values(st)
        s = 2
        while s < C:
            for st in sts:
                double_inverse(st, s)
            s *= 2
        for stage in (solve, outputs):
            for st in sts:
                stage(st)
        for p, st in zip(pairs, sts):
            advance(p, st)
        for st in sts:
            normalize(st)

    @pl.when(c == pl.num_programs(1) - 1)
    def _():
        for p in range(n_pairs):
            s_pair = h_ref[p].T
            st_ref[0, 2 * p] = s_pair[0:HEAD, 0:HEAD]
            st_ref[0, 2 * p + 1] = pltpu.roll(s_pair, HEAD, 1)[HEAD:2 * HEAD, 0:HEAD]


def _wkv(r, k, v, al, ld, kk, ka, rk, gw, gb, *, batch, seq):
    m, d = r.shape
    nc = seq // CHUNK
    n_pairs = d // LANES
    tok = pl.BlockSpec((CHUNK, d), lambda b, c: (b * nc + c, 0))
    par = pl.BlockSpec((1, d), lambda b, c: (0, 0))
    return pl.pallas_call(
        functools.partial(_wkv_kernel, group=WKV_GROUP),
        grid=(batch, nc),
        in_specs=[tok] * 5 + [par] * 5,
        out_specs=[tok, pl.BlockSpec((1, 2 * n_pairs, HEAD, HEAD), lambda b, c: (b, 0, 0, 0))],
        out_shape=[jax.ShapeDtypeStruct((m, d), F32),
                   jax.ShapeDtypeStruct((batch, 2 * n_pairs, HEAD, HEAD), F32)],
        scratch_shapes=[pltpu.VMEM((n_pairs, LANES, LANES), F32)],
        compiler_params=pltpu.CompilerParams(
            dimension_semantics=("arbitrary", "arbitrary"), vmem_limit_bytes=VMEM_LIMIT),
        name="wkv",
    )(r, k, v, al, ld, kk, ka, rk, gw, gb)


def _wkv_step_kernel(r_ref, k_ref, v_ref, al_ref, ld_ref, kk_ref, ka_ref, rk_ref,
                     gw_ref, gb_ref, s_ref, y_ref, so_ref):
    hb, _, _, nb = s_ref.shape
    r_t, k_t, v_t, al_t, ld_t = [ref[...].T for ref in (r_ref, k_ref, v_ref, al_ref, ld_ref)]
    diag = (lax.broadcasted_iota(jnp.int32, (HEAD, HEAD, nb), 0)
            == lax.broadcasted_iota(jnp.int32, (HEAD, HEAD, nb), 1))
    ys = []
    for i in range(hb):
        rows = slice(i * HEAD, (i + 1) * HEAD)
        r, k, v, al = r_t[rows], k_t[rows], v_t[rows], al_t[rows]
        decay = jnp.exp(ld_t[rows])
        kkr = k * kk_ref[rows, :]
        nrm = jnp.sqrt(jnp.sum(kkr * kkr, axis=0, keepdims=True))
        kk = kkr / jnp.maximum(nrm, NORM_EPS)
        kmod = k * (1.0 + (al - 1.0) * ka_ref[rows, :])
        bv = kk * al
        s = s_ref[i]
        sa = jnp.sum(s * kk[None], axis=1, keepdims=True)
        v3 = jnp.sum(jnp.where(diag, v[None], 0.0), axis=1, keepdims=True)
        s_new = s * decay[None] - sa * bv[None] + v3 * kmod[None]
        so_ref[i] = s_new
        y3 = jnp.sum(s_new * r[None], axis=1, keepdims=True)
        y = jnp.sum(jnp.where(diag, y3, 0.0), axis=0)
        mean = jnp.mean(y, axis=0, keepdims=True)
        yc = y - mean
        var = jnp.mean(yc * yc, axis=0, keepdims=True)
        yn = yc * lax.rsqrt(var + GN_EPS) * gw_ref[rows, :] + gb_ref[rows, :]
        bonus = jnp.sum(r * kmod * rk_ref[rows, :], axis=0, keepdims=True) * v
        ys.append(yn + bonus)
    y_ref[...] = jnp.concatenate(ys, axis=0).T


def _wkv_step(r, k, v, al, ld, kk, ka, rk, gw, gb, s0, hb):
    nh, _, _, b = s0.shape
    d = r.shape[1]
    tok = pl.BlockSpec((b, hb * HEAD), lambda i: (0, i))
    par = pl.BlockSpec((hb * HEAD, 1), lambda i: (i, 0))
    st = pl.BlockSpec((hb, HEAD, HEAD, b), lambda i: (i, 0, 0, 0))
    return pl.pallas_call(
        _wkv_step_kernel,
        grid=(nh // hb,),
        in_specs=[tok] * 5 + [par] * 5 + [st],
        out_specs=[tok, st],
        out_shape=[jax.ShapeDtypeStruct((b, d), F32), jax.ShapeDtypeStruct(s0.shape, F32)],
        compiler_params=pltpu.CompilerParams(
            dimension_semantics=("arbitrary",), vmem_limit_bytes=VMEM_LIMIT),
        name="wkv_step",
    )(r, k, v, al, ld, kk, ka, rk, gw, gb, s0)


def _tail_kernel(x_ref, ya_ref, gg_ref, gyb_ref, p_ref, wo_ref, nf_ref, wg_ref, wu_ref,
                 wd_ref, npl_ref, wpg_ref, wpp_ref, nfin_ref, o_ref, *, final):
    merged = ya_ref[...] * gg_ref[...] + gyb_ref[...]
    x = x_ref[...] + _bdot(merged, wo_ref[...])
    x = _swiglu_half(x, nf_ref[...], wg_ref, wu_ref, wd_ref)
    hp = _rms(x, npl_ref[...])
    x = x + _sigmoid(_bdot(hp, wpg_ref[...])) * _bdot(p_ref[...], wpp_ref[...])
    o_ref[...] = _rms(x, nfin_ref[...]) if final else x


def _tail(x, ya, gg, gyb, p, wo, nf, wg, wu, wd, npl, wpg, wpp, nfin, *, final, tm):
    m, d = x.shape
    row = lambda w: pl.BlockSpec((tm, w), lambda i: (i, 0))
    weights = [wo, nf, wg, wu, wd, npl, wpg, wpp, nfin]
    return pl.pallas_call(
        functools.partial(_tail_kernel, final=final),
        grid=(m // tm,),
        in_specs=[row(d)] * 4 + [row(p.shape[1])] + [_const_spec(w.shape) for w in weights],
        out_specs=row(d),
        out_shape=jax.ShapeDtypeStruct((m, d), F32),
        compiler_params=pltpu.CompilerParams(
            dimension_semantics=("arbitrary",), vmem_limit_bytes=VMEM_LIMIT),
        name="tail",
    )(x, ya, gg, gyb, p, *weights)


def _layer(x, p, s0, sh0, cv0, w, *, decode, batch, seq, final):
    d = x.shape[1]
    nh = d // HEAD
    x1 = _ffn(x, w["norm_ffn1"], w["ffn1_gate"], w["ffn1_up"], w["ffn1_down"],
              tm=min(512, x.shape[0]))
    if decode:
        sh_in, c0a, c0b = sh0, cv0[:, 0], cv0[:, 1]
    else:
        sh_in, c0a, c0b = sh0[:, None], cv0[:, 0:1], cv0[:, 1:2]
    r, k, v, al, ld, gg, gyb, sh_new, cv_new = _mix_in(
        x1, w["norm_mix"], w["w_in"], w["mu"], sh_in, c0a, c0b, w["wdec"], w["wicl"],
        w["wgat"], w["decay_bias"], w["iclr_bias"], w["conv_w"],
        decode=decode, batch=batch, seq=seq, tm=min(256, x.shape[0]))
    pars = [w["k_k"], w["k_a"], w["r_k"], w["gn_w"], w["gn_b"]]
    if decode:
        ya, s_new = _wkv_step(r, k, v, al, ld, *[a.reshape(d, 1) for a in pars],
                              jnp.transpose(s0, (1, 2, 3, 0)), hb=2)
        s_new = jnp.transpose(s_new, (3, 0, 1, 2))
        cv_new = jnp.stack([cv0[:, 1], cv_new], axis=1)
    else:
        ya, s_new = _wkv(r, k, v, al, ld, *pars, batch=batch, seq=seq)
        sh_new = sh_new[:, 0]
    x_out = _tail(x1, ya, gg, gyb, p, w["w_out"], w["norm_ffn2"], w["ffn2_gate"],
                  w["ffn2_up"], w["ffn2_down"], w["norm_ple"], w["ple_gate"], w["ple_proj"],
                  w["norm_final"], final=final, tm=min(256, x.shape[0]))
    return x_out, s_new, sh_new, cv_new


def kernel(x_prompt, x_sample, p_prompt, p_sample, state_wkv, state_shift, state_conv,
           norm_ffn1, w_ffn1_gate, w_ffn1_up, w_ffn1_down, norm_mix, w_in, mu_shift,
           w_decay_up, decay_bias, w_iclr_up, iclr_bias, w_gate_up, k_k, k_a, r_k, gn_w,
           gn_b, conv_w, w_out, norm_ffn2, w_ffn2_gate, w_ffn2_up, w_ffn2_down, norm_ple,
           w_ple_gate, w_ple_proj, norm_final):
    depth = w_in.shape[0]
    bp, tp, d = x_prompt.shape
    bs, ts, _ = x_sample.shape
    assert ts == 1 and tp % CHUNK == 0
    shift_cols = mu_shift.shape[1]
    n_dec, n_icl, n_gat = w_decay_up.shape[1], w_iclr_up.shape[1], w_gate_up.shape[1]
    assert 3 * d + n_dec + n_icl + n_gat == shift_cols

    xp = x_prompt.reshape(bp * tp, d)
    xs = x_sample.reshape(bs, d)
    vec = lambda a: a.reshape(1, -1)
    outs = [[] for _ in range(6)]
    for i in range(depth):
        n_tail = shift_cols - 3 * d
        pad = lambda a, lo: jnp.zeros((n_tail, d), BF16).at[lo:lo + a.shape[0]].set(a.astype(BF16))
        w = dict(
            norm_ffn1=vec(norm_ffn1[i]), ffn1_gate=w_ffn1_gate[i].astype(BF16),
            ffn1_up=w_ffn1_up[i].astype(BF16), ffn1_down=w_ffn1_down[i].astype(BF16),
            norm_mix=vec(norm_mix[i]), w_in=w_in[i].astype(BF16), mu=vec(mu_shift[i]),
            wdec=pad(w_decay_up[i], 0), wicl=pad(w_iclr_up[i], n_dec),
            wgat=pad(w_gate_up[i], n_dec + n_icl),
            decay_bias=vec(decay_bias[i]), iclr_bias=vec(iclr_bias[i]), conv_w=conv_w[i],
            k_k=vec(k_k[i]), k_a=vec(k_a[i]), r_k=vec(r_k[i]), gn_w=vec(gn_w[i]),
            gn_b=vec(gn_b[i]), w_out=w_out[i].astype(BF16), norm_ffn2=vec(norm_ffn2[i]),
            ffn2_gate=w_ffn2_gate[i].astype(BF16), ffn2_up=w_ffn2_up[i].astype(BF16),
            ffn2_down=w_ffn2_down[i].astype(BF16), norm_ple=vec(norm_ple[i]),
            ple_gate=w_ple_gate[i].astype(BF16), ple_proj=w_ple_proj[i].astype(BF16),
            norm_final=vec(norm_final))
        final = i == depth - 1
        xp, s_p, sh_p, cv_p = _layer(
            xp, p_prompt[i].reshape(bp * tp, -1), None,
            jnp.zeros((bp, shift_cols), F32), jnp.zeros((bp, 2, d), F32), w,
            decode=False, batch=bp, seq=tp, final=final)
        xs, s_s, sh_s, cv_s = _layer(
            xs, p_sample[i].reshape(bs, -1), state_wkv[i], state_shift[i], state_conv[i], w,
            decode=True, batch=bs, seq=1, final=final)
        for lst, val in zip(outs, (s_p, sh_p, cv_p, s_s, sh_s, cv_s)):
            lst.append(val)
    stk = [jnp.stack(o, 0) for o in outs]
    return (xp.reshape(bp, tp, d), xs.reshape(bs, ts, d), stk[0], stk[1], stk[2],
            stk[3], stk[4], stk[5])
```

```python
import functools

import jax
import jax.numpy as jnp
from jax import lax
from jax.experimental import pallas as pl
from jax.experimental.pallas import tpu as pltpu

F32 = jnp.float32
BF16 = jnp.bfloat16

HEAD = 64
LANES = 128
CHUNK = 64
WKV_GROUP = 8
WKV_CHUNKS_PER_STEP = 2
RMS_EPS = 1e-6
GN_EPS = 64e-5
NORM_EPS = 1e-12
VMEM_LIMIT = 56 * 1024 * 1024


def _rms(x, g):
    return x * lax.rsqrt(jnp.mean(x * x, axis=-1, keepdims=True) + RMS_EPS) * g


def _sigmoid(x):
    return 1.0 / (1.0 + jnp.exp(-x))


def _softplus(x):
    return jnp.maximum(x, 0.0) + jnp.log(1.0 + jnp.exp(-jnp.abs(x)))


def _bdot(a, b):
    return jnp.dot(a.astype(BF16), b, preferred_element_type=F32)


def _const_spec(shape):
    zeros = (0,) * len(shape)
    return pl.BlockSpec(shape, lambda *_: zeros, pipeline_mode=pl.Buffered(1))


def _swiglu_half(x, g, wg_ref, wu_ref, wd_ref):
    h = _rms(x, g).astype(BF16)
    gate = jnp.dot(h, wg_ref[...], preferred_element_type=F32)
    up = jnp.dot(h, wu_ref[...], preferred_element_type=F32)
    act = gate * _sigmoid(gate) * up
    return x + 0.5 * _bdot(act, wd_ref[...])


def _ffn_kernel(x_ref, g_ref, wg_ref, wu_ref, wd_ref, o_ref):
    o_ref[...] = _swiglu_half(x_ref[...], g_ref[...], wg_ref, wu_ref, wd_ref)


def _ffn(x, g, wg, wu, wd, tm):
    m, d = x.shape
    row = pl.BlockSpec((tm, d), lambda i: (i, 0))
    return pl.pallas_call(
        _ffn_kernel,
        grid=(m // tm,),
        in_specs=[row, _const_spec(g.shape), _const_spec(wg.shape),
                  _const_spec(wu.shape), _const_spec(wd.shape)],
        out_specs=row,
        out_shape=jax.ShapeDtypeStruct((m, d), F32),
        compiler_params=pltpu.CompilerParams(
            dimension_semantics=("arbitrary",), vmem_limit_bytes=VMEM_LIMIT),
        name="ffn",
    )(x, g, wg, wu, wd)


def _mix_in_kernel(x_ref, nm_ref, w_in_ref, mu_ref, sh0_ref, c0a_ref, c0b_ref,
                   wdec_ref, wicl_ref, wgat_ref, dbias_ref, ibias_ref, cw_ref,
                   r_ref, k_ref, v_ref, al_ref, ld_ref, gg_ref, gyb_ref,
                   sh_out_ref, cv_out_ref, *scratch, decode, d, shift_cols):
    tm = x_ref.shape[0]
    h = _rms(x_ref[...], nm_ref[...]).astype(BF16)
    z_sh = jnp.dot(h, w_in_ref[:, 0:shift_cols], preferred_element_type=F32)
    z_cv = jnp.dot(h, w_in_ref[:, shift_cols:shift_cols + 3 * d],
                   preferred_element_type=F32)
    z_gt = jnp.dot(h, w_in_ref[:, shift_cols + 3 * d:shift_cols + 5 * d],
                   preferred_element_type=F32)
    cu = z_cv[:, d:2 * d] * z_cv[:, 2 * d:3 * d]

    if decode:
        z_prev = sh0_ref[...]
        cu_m2 = c0a_ref[...]
        cu_m1 = c0b_ref[...]
        sh_out_ref[...] = z_sh
        cv_out_ref[...] = cu
    else:
        carry_sh, carry_cu = scratch

        @pl.when(pl.program_id(1) == 0)
        def _():
            carry_sh[0:1, :] = sh0_ref[0]
            carry_cu[0:1, :] = c0a_ref[0]
            carry_cu[1:2, :] = c0b_ref[0]

        row = lax.broadcasted_iota(jnp.int32, (tm, 1), 0)
        z_prev = jnp.where(row == 0, carry_sh[0:1, :], pltpu.roll(z_sh, 1, 0))
        cu_m1 = jnp.where(row == 0, carry_cu[1:2, :], pltpu.roll(cu, 1, 0))
        cu_m2 = jnp.where(row == 0, carry_cu[0:1, :],
                          jnp.where(row == 1, carry_cu[1:2, :], pltpu.roll(cu, 2, 0)))
        carry_sh[0:1, :] = z_sh[tm - 1:tm, :]
        carry_cu[0:2, :] = cu[tm - 2:tm, :]
        sh_out_ref[0] = z_sh[tm - 1:tm, :]
        cv_out_ref[0] = cu[tm - 2:tm, :]

    zs = z_sh + mu_ref[...] * (z_prev - z_sh)
    r_ref[...] = zs[:, 0:d]
    k_ref[...] = zs[:, d:2 * d]
    v_ref[...] = zs[:, 2 * d:3 * d]
    tail = zs[:, 3 * d:shift_cols]
    dec = dbias_ref[...] + _bdot(jnp.tanh(tail), wdec_ref[...])
    ld_ref[...] = -jnp.exp(-_softplus(-dec) - 0.5)
    al_ref[...] = _sigmoid(ibias_ref[...] + _bdot(tail, wicl_ref[...]))
    gate = _bdot(_sigmoid(tail), wgat_ref[...])
    gg_ref[...] = gate * _sigmoid(z_gt[:, 0:d])
    conv = cu_m2 * cw_ref[0:1, :] + cu_m1 * cw_ref[1:2, :] + cu * cw_ref[2:3, :]
    gyb_ref[...] = _sigmoid(z_gt[:, d:2 * d]) * (z_cv[:, 0:d] * conv)


def _mix_in(x, nm, w_in, mu, sh0, c0a, c0b, wdec, wicl, wgat, dbias, ibias, cw,
            *, decode, batch, seq, tm):
    m, d = x.shape
    shift_cols = mu.shape[1]
    consts = [nm, w_in, mu]
    lora = [wdec, wicl, wgat, dbias, ibias, cw]
    if decode:
        grid = (1,)
        row = lambda w: pl.BlockSpec((m, w), lambda i: (0, 0))
        state_specs = [row(shift_cols), row(d), row(d)]
        state_out = [row(shift_cols), row(d)]
        state_shapes = [jax.ShapeDtypeStruct((m, shift_cols), F32),
                        jax.ShapeDtypeStruct((m, d), F32)]
        scratch = []
        sem = ("arbitrary",)
    else:
        nt = seq // tm
        grid = (batch, nt)
        row = lambda w: pl.BlockSpec((tm, w), lambda b, t: (b * nt + t, 0))
        per_b = lambda r, w: pl.BlockSpec((1, r, w), lambda b, t: (b, 0, 0))
        state_specs = [per_b(1, shift_cols), per_b(1, d), per_b(1, d)]
        state_out = [per_b(1, shift_cols), per_b(2, d)]
        state_shapes = [jax.ShapeDtypeStruct((batch, 1, shift_cols), F32),
                        jax.ShapeDtypeStruct((batch, 2, d), F32)]
        scratch = [pltpu.VMEM((8, shift_cols), F32), pltpu.VMEM((8, d), F32)]
        sem = ("arbitrary", "arbitrary")
    tok = jax.ShapeDtypeStruct((m, d), F32)
    return pl.pallas_call(
        functools.partial(_mix_in_kernel, decode=decode, d=d, shift_cols=shift_cols),
        grid=grid,
        in_specs=([row(d)] + [_const_spec(a.shape) for a in consts] + state_specs
                  + [_const_spec(a.shape) for a in lora]),
        out_specs=[row(d)] * 7 + state_out,
        out_shape=[tok] * 7 + state_shapes,
        scratch_shapes=scratch,
        compiler_params=pltpu.CompilerParams(
            dimension_semantics=sem, vmem_limit_bytes=VMEM_LIMIT),
        name="mix_in_decode" if decode else "mix_in",
    )(x, *consts, sh0, c0a, c0b, *lora)


def _head_sum(x, first):
    x1 = jnp.where(first, x, 0.0)
    s1 = jnp.sum(x1, axis=-1, keepdims=True)
    s2 = jnp.sum(x - x1, axis=-1, keepdims=True)
    return jnp.where(first, s1, s2)


def _stack_heads(x):
    x = x.astype(BF16)
    lane = lax.broadcasted_iota(jnp.int32, x.shape, 1) % LANES
    first = lane < HEAD
    zero = jnp.zeros_like(x)
    return jnp.concatenate([jnp.where(first, x, zero), jnp.where(first, zero, x)], axis=0)


def _apply(mat, x):
    return jnp.dot(mat.astype(BF16), _stack_heads(x), preferred_element_type=F32)


def _split2(x):
    hi = x.astype(BF16)
    return hi, (x - hi.astype(F32)).astype(BF16)


def _split3(x):
    hi = x.astype(BF16)
    rest = x - hi.astype(F32)
    mid = rest.astype(BF16)
    return hi, mid, (rest - mid.astype(F32)).astype(BF16)


def _wkv_kernel(r_ref, k_ref, v_ref, al_ref, ld_ref, kk_ref, ka_ref, rk_ref,
                gw_ref, gb_ref, y_ref, st_ref, h_ref, *, group):
    c = pl.program_id(1)
    n_pairs = h_ref.shape[0]

    @pl.when(c == 0)
    def _():
        h_ref[...] = jnp.zeros_like(h_ref)

    C = CHUNK
    row = lax.broadcasted_iota(jnp.int32, (C, LANES), 0)
    col = lax.broadcasted_iota(jnp.int32, (C, LANES), 1)
    first = col < HEAD
    j = col % HEAD
    strict = j < row
    incl = j <= row
    cs_r = lax.broadcasted_iota(jnp.int32, (C, 3 * C), 0)
    cs_c = lax.broadcasted_iota(jnp.int32, (C, 3 * C), 1) % C
    tri3 = (cs_c <= cs_r).astype(BF16)
    sq_r = lax.broadcasted_iota(jnp.int32, (LANES, LANES), 0)
    sq_c = lax.broadcasted_iota(jnp.int32, (LANES, LANES), 1)
    same_head = (sq_r // HEAD) == (sq_c // HEAD)
    eye = sq_r == sq_c

    def prefix(q, p):
        rows = slice(q * C, (q + 1) * C)
        sl = slice(p * LANES, (p + 1) * LANES)
        ld = ld_ref[rows, sl]
        lp = jnp.dot(tri3, jnp.concatenate(_split3(ld), axis=0), preferred_element_type=F32)
        return dict(rows=rows, sl=sl, ld=ld, lp=lp)

    def scores(st):
        rows, sl, ld, lp = st["rows"], st["sl"], st["ld"], st["lp"]
        r = r_ref[rows, sl]
        k = k_ref[rows, sl]
        v = v_ref[rows, sl]
        al = al_ref[rows, sl]
        lp_end = lp[C - 1:C, :]
        kkr = k * kk_ref[:, sl]
        nrm = jnp.sqrt(_head_sum(kkr * kkr, first))
        kk = kkr / jnp.maximum(nrm, NORM_EPS)
        kmod = k * (1.0 + (al - 1.0) * ka_ref[:, sl])
        bv = kk * al
        e_neg = jnp.exp(-lp)
        e_end = jnp.exp(lp_end - lp)
        st.update(v=v, at=kk * jnp.exp(lp - ld), rt=r * jnp.exp(lp),
                  bh=bv * e_end, kh=kmod * e_end, p_end=jnp.exp(lp_end),
                  bonus=_head_sum(r * kmod * rk_ref[:, sl], first) * v)
        sc = lax.dot_general(
            jnp.concatenate([st["at"], st["rt"]], axis=0).astype(BF16),
            jnp.concatenate([_stack_heads(bv * e_neg), _stack_heads(kmod * e_neg)], axis=0),
            (((1,), (1,)), ((), ())), preferred_element_type=F32)
        st["lab"] = jnp.where(strict, sc[0:C, 0:LANES], 0.0)
        st["lak"] = jnp.where(strict, sc[0:C, LANES:2 * LANES], 0.0)
        st["mrb"] = jnp.where(incl, sc[C:2 * C, 0:LANES], 0.0)
        st["mrk"] = jnp.where(incl, sc[C:2 * C, LANES:2 * LANES], 0.0)
        st["inv"] = (jnp.where(j == row, 1.0, 0.0)
                     - jnp.where((j // 2) == (row // 2), st["lab"], 0.0))

    def double_left(st, s):
        off = jnp.where(((j // (2 * s)) == (row // (2 * s))) & ((j // s) != (row // s)),
                        st["lab"], 0.0)
        st["inv_off"] = _apply(st["inv"], off)

    def double_right(st, s):
        st["inv"] = st["inv"] - _apply(st["inv_off"], st["inv"])

    def values(st):
        st["xv"] = _apply(jnp.concatenate([st["lak"], st["mrk"]], axis=0), st["v"])
        lhs_t = jnp.concatenate([st["kh"], -st["bh"]], axis=0).T
        l_hi, l_lo = _split2(lhs_t)
        st["lhs3"] = jnp.concatenate([l_hi, l_hi, l_lo], axis=1)

    def solve(st):
        st["uw"] = _apply(st["inv"], jnp.concatenate([st["at"], st["xv"][0:C]], axis=1))

    def outputs(st):
        uw, v = st["uw"], st["v"]
        mu = _apply(st["mrb"], uw)
        st["q"] = (st["rt"] - mu[:, 0:LANES]).astype(BF16)
        st["y0"] = st["xv"][C:2 * C] - mu[:, LANES:2 * LANES]
        rhs = jnp.concatenate(
            [jnp.concatenate([v, jnp.zeros_like(v)], axis=1),
             jnp.concatenate([uw[:, LANES:2 * LANES], -uw[:, 0:LANES]], axis=1)], axis=0)
        r_hi, r_lo = _split2(rhs)
        gb = jnp.dot(st["lhs3"], jnp.concatenate([r_hi, r_lo, r_hi], axis=0),
                     preferred_element_type=F32)
        st["g_c"] = jnp.where(same_head, gb[:, 0:LANES], 0.0)
        st["a_c"] = jnp.where(same_head, jnp.where(eye, st["p_end"], 0.0)
                              - gb[:, LANES:2 * LANES], 0.0).astype(BF16)

    def advance(p, st):
        h0_b = h_ref[p].astype(BF16)
        st["y"] = jnp.dot(st["q"], h0_b, preferred_element_type=F32) + st["y0"]
        h_ref[p] = jnp.dot(st["a_c"], h0_b, preferred_element_type=F32) + st["g_c"]

    def normalize(st):
        sl, y = st["sl"], st["y"]
        mean = _head_sum(y, first) * (1.0 / HEAD)
        yc = y - mean
        var = _head_sum(yc * yc, first) * (1.0 / HEAD)
        yn = yc * lax.rsqrt(var + GN_EPS) * gw_ref[:, sl] + gb_ref[:, sl]
        y_ref[st["rows"], sl] = yn + st["bonus"]

    n_chunks = r_ref.shape[0] // C
    for g0 in range(0, n_pairs, group):
        units = [(q, p) for q in range(n_chunks) for p in range(g0, min(g0 + group, n_pairs))]
        sts = [prefix(q, p) for q, p in units]
        for stage in (scores, values):
            for st in sts:
                stage(st)
        s = 2
        while s < C:
            for stage in (double_left, double_right):
                for st in sts:
                    stage(st, s)
            s *= 2
        for stage in (solve, outputs):
            for st in sts:
                stage(st)
        for (_, p), st in zip(units, sts):
            advance(p, st)
        for st in sts:
            normalize(st)

    @pl.when(c == pl.num_programs(1) - 1)
    def _():
        for p in range(n_pairs):
            s_pair = h_ref[p].T
            st_ref[0, 2 * p] = s_pair[0:HEAD, 0:HEAD]
            st_ref[0, 2 * p + 1] = pltpu.roll(s_pair, HEAD, 1)[HEAD:2 * HEAD, 0:HEAD]


def _wkv(r, k, v, al, ld, kk, ka, rk, gw, gb, *, batch, seq):
    m, d = r.shape
    rows = WKV_CHUNKS_PER_STEP * CHUNK
    nc = seq // rows
    n_pairs = d // LANES
    tok = pl.BlockSpec((rows, d), lambda b, c: (b * nc + c, 0))
    par = pl.BlockSpec((1, d), lambda b, c: (0, 0))
    return pl.pallas_call(
        functools.partial(_wkv_kernel, group=WKV_GROUP),
        grid=(batch, nc),
        in_specs=[tok] * 5 + [par] * 5,
        out_specs=[tok, pl.BlockSpec((1, 2 * n_pairs, HEAD, HEAD), lambda b, c: (b, 0, 0, 0))],
        out_shape=[jax.ShapeDtypeStruct((m, d), F32),
                   jax.ShapeDtypeStruct((batch, 2 * n_pairs, HEAD, HEAD), F32)],
        scratch_shapes=[pltpu.VMEM((n_pairs, LANES, LANES), F32)],
        compiler_params=pltpu.CompilerParams(
            dimension_semantics=("arbitrary", "arbitrary"), vmem_limit_bytes=VMEM_LIMIT),
        name="wkv",
    )(r, k, v, al, ld, kk, ka, rk, gw, gb)


def _wkv_step_kernel(r_ref, k_ref, v_ref, al_ref, ld_ref, kk_ref, ka_ref, rk_ref,
                     gw_ref, gb_ref, s_ref, y_ref, so_ref):
    hb, _, _, nb = s_ref.shape
    r_t, k_t, v_t, al_t, ld_t = [ref[...].T for ref in (r_ref, k_ref, v_ref, al_ref, ld_ref)]
    diag = (lax.broadcasted_iota(jnp.int32, (HEAD, HEAD, nb), 0)
            == lax.broadcasted_iota(jnp.int32, (HEAD, HEAD, nb), 1))
    ys = []
    for i in range(hb):
        rows = slice(i * HEAD, (i + 1) * HEAD)
        r, k, v, al = r_t[rows], k_t[rows], v_t[rows], al_t[rows]
        decay = jnp.exp(ld_t[rows])
        kkr = k * kk_ref[rows, :]
        nrm = jnp.sqrt(jnp.sum(kkr * kkr, axis=0, keepdims=True))
        kk = kkr / jnp.maximum(nrm, NORM_EPS)
        kmod = k * (1.0 + (al - 1.0) * ka_ref[rows, :])
        bv = kk * al
        s = s_ref[i]
        sa = jnp.sum(s * kk[None], axis=1, keepdims=True)
        v3 = jnp.sum(jnp.where(diag, v[None], 0.0), axis=1, keepdims=True)
        s_new = s * decay[None] - sa * bv[None] + v3 * kmod[None]
        so_ref[i] = s_new
        y3 = jnp.sum(s_new * r[None], axis=1, keepdims=True)
        y = jnp.sum(jnp.where(diag, y3, 0.0), axis=0)
        mean = jnp.mean(y, axis=0, keepdims=True)
        yc = y - mean
        var = jnp.mean(yc * yc, axis=0, keepdims=True)
        yn = yc * lax.rsqrt(var + GN_EPS) * gw_ref[rows, :] + gb_ref[rows, :]
        bonus = jnp.sum(r * kmod * rk_ref[rows, :], axis=0, keepdims=True) * v
        ys.append(yn + bonus)
    y_ref[...] = jnp.concatenate(ys, axis=0).T


def _wkv_step(r, k, v, al, ld, kk, ka, rk, gw, gb, s0, hb):
    nh, _, _, b = s0.shape
    d = r.shape[1]
    tok = pl.BlockSpec((b, hb * HEAD), lambda i: (0, i))
    par = pl.BlockSpec((hb * HEAD, 1), lambda i: (i, 0))
    st = pl.BlockSpec((hb, HEAD, HEAD, b), lambda i: (i, 0, 0, 0))
    return pl.pallas_call(
        _wkv_step_kernel,
        grid=(nh // hb,),
        in_specs=[tok] * 5 + [par] * 5 + [st],
        out_specs=[tok, st],
        out_shape=[jax.ShapeDtypeStruct((b, d), F32), jax.ShapeDtypeStruct(s0.shape, F32)],
        compiler_params=pltpu.CompilerParams(
            dimension_semantics=("arbitrary",), vmem_limit_bytes=VMEM_LIMIT),
        name="wkv_step",
    )(r, k, v, al, ld, kk, ka, rk, gw, gb, s0)


def _tail_kernel(x_ref, ya_ref, gg_ref, gyb_ref, p_ref, wo_ref, nf_ref, wg_ref, wu_ref,
                 wd_ref, npl_ref, wpg_ref, wpp_ref, nfin_ref, o_ref, *, final):
    merged = ya_ref[...] * gg_ref[...] + gyb_ref[...]
    x = x_ref[...] + _bdot(merged, wo_ref[...])
    x = _swiglu_half(x, nf_ref[...], wg_ref, wu_ref, wd_ref)
    hp = _rms(x, npl_ref[...])
    x = x + _sigmoid(_bdot(hp, wpg_ref[...])) * _bdot(p_ref[...], wpp_ref[...])
    o_ref[...] = _rms(x, nfin_ref[...]) if final else x


def _tail(x, ya, gg, gyb, p, wo, nf, wg, wu, wd, npl, wpg, wpp, nfin, *, final, tm):
    m, d = x.shape
    row = lambda w: pl.BlockSpec((tm, w), lambda i: (i, 0))
    weights = [wo, nf, wg, wu, wd, npl, wpg, wpp, nfin]
    return pl.pallas_call(
        functools.partial(_tail_kernel, final=final),
        grid=(m // tm,),
        in_specs=[row(d)] * 4 + [row(p.shape[1])] + [_const_spec(w.shape) for w in weights],
        out_specs=row(d),
        out_shape=jax.ShapeDtypeStruct((m, d), F32),
        compiler_params=pltpu.CompilerParams(
            dimension_semantics=("arbitrary",), vmem_limit_bytes=VMEM_LIMIT),
        name="tail",
    )(x, ya, gg, gyb, p, *weights)


def _layer(x, p, s0, sh0, cv0, w, *, decode, batch, seq, final):
    d = x.shape[1]
    nh = d // HEAD
    x1 = _ffn(x, w["norm_ffn1"], w["ffn1_gate"], w["ffn1_up"], w["ffn1_down"],
              tm=min(512, x.shape[0]))
    if decode:
        sh_in, c0a, c0b = sh0, cv0[:, 0], cv0[:, 1]
    else:
        sh_in, c0a, c0b = sh0[:, None], cv0[:, 0:1], cv0[:, 1:2]
    r, k, v, al, ld, gg, gyb, sh_new, cv_new = _mix_in(
        x1, w["norm_mix"], w["w_in"], w["mu"], sh_in, c0a, c0b, w["wdec"], w["wicl"],
        w["wgat"], w["decay_bias"], w["iclr_bias"], w["conv_w"],
        decode=decode, batch=batch, seq=seq, tm=min(256, x.shape[0]))
    pars = [w["k_k"], w["k_a"], w["r_k"], w["gn_w"], w["gn_b"]]
    if decode:
        ya, s_new = _wkv_step(r, k, v, al, ld, *[a.reshape(d, 1) for a in pars],
                              jnp.transpose(s0, (1, 2, 3, 0)), hb=2)
        s_new = jnp.transpose(s_new, (3, 0, 1, 2))
        cv_new = jnp.stack([cv0[:, 1], cv_new], axis=1)
    else:
        ya, s_new = _wkv(r, k, v, al, ld, *pars, batch=batch, seq=seq)
        sh_new = sh_new[:, 0]
    x_out = _tail(x1, ya, gg, gyb, p, w["w_out"], w["norm_ffn2"], w["ffn2_gate"],
                  w["ffn2_up"], w["ffn2_down"], w["norm_ple"], w["ple_gate"], w["ple_proj"],
                  w["norm_final"], final=final, tm=min(256, x.shape[0]))
    return x_out, s_new, sh_new, cv_new


def kernel(x_prompt, x_sample, p_prompt, p_sample, state_wkv, state_shift, state_conv,
           norm_ffn1, w_ffn1_gate, w_ffn1_up, w_ffn1_down, norm_mix, w_in, mu_shift,
           w_decay_up, decay_bias, w_iclr_up, iclr_bias, w_gate_up, k_k, k_a, r_k, gn_w,
           gn_b, conv_w, w_out, norm_ffn2, w_ffn2_gate, w_ffn2_up, w_ffn2_down, norm_ple,
           w_ple_gate, w_ple_proj, norm_final):
    depth = w_in.shape[0]
    bp, tp, d = x_prompt.shape
    bs, ts, _ = x_sample.shape
    assert ts == 1 and tp % (WKV_CHUNKS_PER_STEP * CHUNK) == 0
    shift_cols = mu_shift.shape[1]
    n_dec, n_icl, n_gat = w_decay_up.shape[1], w_iclr_up.shape[1], w_gate_up.shape[1]
    assert 3 * d + n_dec + n_icl + n_gat == shift_cols

    xp = x_prompt.reshape(bp * tp, d)
    xs = x_sample.reshape(bs, d)
    vec = lambda a: a.reshape(1, -1)
    outs = [[] for _ in range(6)]
    for i in range(depth):
        n_tail = shift_cols - 3 * d
        pad = lambda a, lo: jnp.zeros((n_tail, d), BF16).at[lo:lo + a.shape[0]].set(a.astype(BF16))
        w = dict(
            norm_ffn1=vec(norm_ffn1[i]), ffn1_gate=w_ffn1_gate[i].astype(BF16),
            ffn1_up=w_ffn1_up[i].astype(BF16), ffn1_down=w_ffn1_down[i].astype(BF16),
            norm_mix=vec(norm_mix[i]), w_in=w_in[i].astype(BF16), mu=vec(mu_shift[i]),
            wdec=pad(w_decay_up[i], 0), wicl=pad(w_iclr_up[i], n_dec),
            wgat=pad(w_gate_up[i], n_dec + n_icl),
            decay_bias=vec(decay_bias[i]), iclr_bias=vec(iclr_bias[i]), conv_w=conv_w[i],
            k_k=vec(k_k[i]), k_a=vec(k_a[i]), r_k=vec(r_k[i]), gn_w=vec(gn_w[i]),
            gn_b=vec(gn_b[i]), w_out=w_out[i].astype(BF16), norm_ffn2=vec(norm_ffn2[i]),
            ffn2_gate=w_ffn2_gate[i].astype(BF16), ffn2_up=w_ffn2_up[i].astype(BF16),
            ffn2_down=w_ffn2_down[i].astype(BF16), norm_ple=vec(norm_ple[i]),
            ple_gate=w_ple_gate[i].astype(BF16), ple_proj=w_ple_proj[i].astype(BF16),
            norm_final=vec(norm_final))
        final = i == depth - 1
        xp, s_p, sh_p, cv_p = _layer(
            xp, p_prompt[i].reshape(bp * tp, -1), None,
            jnp.zeros((bp, shift_cols), F32), jnp.zeros((bp, 2, d), F32), w,
            decode=False, batch=bp, seq=tp, final=final)
        xs, s_s, sh_s, cv_s = _layer(
            xs, p_sample[i].reshape(bs, -1), state_wkv[i], state_shift[i], state_conv[i], w,
            decode=True, batch=bs, seq=1, final=final)
        for lst, val in zip(outs, (s_p, sh_p, cv_p, s_s, sh_s, cv_s)):
            lst.append(val)
    stk = [jnp.stack(o, 0) for o in outs]
    return (xp.reshape(bp, tp, d), xs.reshape(bs, ts, d), stk[0], stk[1], stk[2],
            stk[3], stk[4], stk[5])
```

```python
import functools

import jax
import jax.numpy as jnp
from jax import lax
from jax.experimental import pallas as pl
from jax.experimental.pallas import tpu as pltpu

F32 = jnp.float32
BF16 = jnp.bfloat16

HEAD = 64
LANES = 128
CHUNK = 64
MIX_COL_BLOCK = 256
WKV_GROUP = 8
WKV_CHUNKS_PER_STEP = 2
RMS_EPS = 1e-6
GN_EPS = 64e-5
NORM_EPS = 1e-12
VMEM_LIMIT = 56 * 1024 * 1024


def _rms(x, g):
    return x * lax.rsqrt(jnp.mean(x * x, axis=-1, keepdims=True) + RMS_EPS) * g


def _sigmoid(x):
    return 1.0 / (1.0 + jnp.exp(-x))


def _softplus(x):
    return jnp.maximum(x, 0.0) + jnp.log(1.0 + jnp.exp(-jnp.abs(x)))


def _bdot(a, b):
    return jnp.dot(a.astype(BF16), b, preferred_element_type=F32)


def _const_spec(shape):
    zeros = (0,) * len(shape)
    return pl.BlockSpec(shape, lambda *_: zeros, pipeline_mode=pl.Buffered(1))


def _swiglu_half(x, g, wg_ref, wu_ref, wd_ref):
    h = _rms(x, g).astype(BF16)
    gate = jnp.dot(h, wg_ref[...], preferred_element_type=F32)
    up = jnp.dot(h, wu_ref[...], preferred_element_type=F32)
    act = gate * _sigmoid(gate) * up
    return x + 0.5 * _bdot(act, wd_ref[...])


def _ffn_kernel(x_ref, g_ref, wg_ref, wu_ref, wd_ref, o_ref):
    o_ref[...] = _swiglu_half(x_ref[...], g_ref[...], wg_ref, wu_ref, wd_ref)


def _ffn(x, g, wg, wu, wd, tm):
    m, d = x.shape
    row = pl.BlockSpec((tm, d), lambda i: (i, 0))
    return pl.pallas_call(
        _ffn_kernel,
        grid=(m // tm,),
        in_specs=[row, _const_spec(g.shape), _const_spec(wg.shape),
                  _const_spec(wu.shape), _const_spec(wd.shape)],
        out_specs=row,
        out_shape=jax.ShapeDtypeStruct((m, d), F32),
        compiler_params=pltpu.CompilerParams(
            dimension_semantics=("arbitrary",), vmem_limit_bytes=VMEM_LIMIT),
        name="ffn",
    )(x, g, wg, wu, wd)


def _mix_in_kernel(x_ref, nm_ref, w_in_ref, mu_ref, sh0_ref, c0a_ref, c0b_ref,
                   wdec_ref, wicl_ref, wgat_ref, dbias_ref, ibias_ref, cw_ref,
                   r_ref, k_ref, v_ref, al_ref, ld_ref, gg_ref, gyb_ref,
                   sh_out_ref, cv_out_ref, *scratch, decode, d, shift_cols, blk):
    tm = x_ref.shape[0]
    h = _rms(x_ref[...], nm_ref[...]).astype(BF16)
    conv0, gate0 = shift_cols, shift_cols + 3 * d

    if not decode:
        carry_sh, carry_cu = scratch

        @pl.when(pl.program_id(1) == 0)
        def _():
            carry_sh[0:1, :] = sh0_ref[0]
            carry_cu[0:1, :] = c0a_ref[0]
            carry_cu[1:2, :] = c0b_ref[0]

        row = lax.broadcasted_iota(jnp.int32, (tm, 1), 0)

    def proj(c0, width):
        return jnp.dot(h, w_in_ref[:, c0:c0 + width], preferred_element_type=F32)

    def shifted(c0, width):
        cols = slice(c0, c0 + width)
        z = proj(c0, width)
        if decode:
            z_prev = sh0_ref[:, cols]
            sh_out_ref[:, cols] = z
        else:
            z_prev = jnp.where(row == 0, carry_sh[0:1, cols], pltpu.roll(z, 1, 0))
            carry_sh[0:1, cols] = z[tm - 1:tm, :]
            sh_out_ref[0, :, cols] = z[tm - 1:tm, :]
        return z + mu_ref[:, cols] * (z_prev - z)

    tail = shifted(3 * d, shift_cols - 3 * d)
    tail_tanh = jnp.tanh(tail).astype(BF16)
    tail_sig = _sigmoid(tail).astype(BF16)
    tail_raw = tail.astype(BF16)
    for c in range(0, d, blk):
        cols = slice(c, c + blk)
        r_ref[:, cols] = shifted(c, blk)
        k_ref[:, cols] = shifted(d + c, blk)
        v_ref[:, cols] = shifted(2 * d + c, blk)
        dec = dbias_ref[:, cols] + jnp.dot(tail_tanh, wdec_ref[:, cols],
                                           preferred_element_type=F32)
        ld_ref[:, cols] = -jnp.exp(-_softplus(-dec) - 0.5)
        al_ref[:, cols] = _sigmoid(ibias_ref[:, cols] + jnp.dot(
            tail_raw, wicl_ref[:, cols], preferred_element_type=F32))
        gate = jnp.dot(tail_sig, wgat_ref[:, cols], preferred_element_type=F32)
        gg_ref[:, cols] = gate * _sigmoid(proj(gate0 + c, blk))
        cu = proj(conv0 + d + c, blk) * proj(conv0 + 2 * d + c, blk)
        if decode:
            cu_m2 = c0a_ref[:, cols]
            cu_m1 = c0b_ref[:, cols]
            cv_out_ref[:, cols] = cu
        else:
            cu_m1 = jnp.where(row == 0, carry_cu[1:2, cols], pltpu.roll(cu, 1, 0))
            cu_m2 = jnp.where(row == 0, carry_cu[0:1, cols],
                              jnp.where(row == 1, carry_cu[1:2, cols], pltpu.roll(cu, 2, 0)))
            carry_cu[0:2, cols] = cu[tm - 2:tm, :]
            cv_out_ref[0, :, cols] = cu[tm - 2:tm, :]
        conv = cu_m2 * cw_ref[0:1, cols] + cu_m1 * cw_ref[1:2, cols] + cu * cw_ref[2:3, cols]
        gyb_ref[:, cols] = _sigmoid(proj(gate0 + d + c, blk)) * (proj(conv0 + c, blk) * conv)


def _mix_in(x, nm, w_in, mu, sh0, c0a, c0b, wdec, wicl, wgat, dbias, ibias, cw,
            *, decode, batch, seq, tm):
    m, d = x.shape
    shift_cols = mu.shape[1]
    consts = [nm, w_in, mu]
    lora = [wdec, wicl, wgat, dbias, ibias, cw]
    if decode:
        grid = (1,)
        row = lambda w: pl.BlockSpec((m, w), lambda i: (0, 0))
        state_specs = [row(shift_cols), row(d), row(d)]
        state_out = [row(shift_cols), row(d)]
        state_shapes = [jax.ShapeDtypeStruct((m, shift_cols), F32),
                        jax.ShapeDtypeStruct((m, d), F32)]
        scratch = []
        sem = ("arbitrary",)
    else:
        nt = seq // tm
        grid = (batch, nt)
        row = lambda w: pl.BlockSpec((tm, w), lambda b, t: (b * nt + t, 0))
        per_b = lambda r, w: pl.BlockSpec((1, r, w), lambda b, t: (b, 0, 0))
        state_specs = [per_b(1, shift_cols), per_b(1, d), per_b(1, d)]
        state_out = [per_b(1, shift_cols), per_b(2, d)]
        state_shapes = [jax.ShapeDtypeStruct((batch, 1, shift_cols), F32),
                        jax.ShapeDtypeStruct((batch, 2, d), F32)]
        scratch = [pltpu.VMEM((8, shift_cols), F32), pltpu.VMEM((8, d), F32)]
        sem = ("arbitrary", "arbitrary")
    tok = jax.ShapeDtypeStruct((m, d), F32)
    return pl.pallas_call(
        functools.partial(_mix_in_kernel, decode=decode, d=d, shift_cols=shift_cols,
                          blk=MIX_COL_BLOCK),
        grid=grid,
        in_specs=([row(d)] + [_const_spec(a.shape) for a in consts] + state_specs
                  + [_const_spec(a.shape) for a in lora]),
        out_specs=[row(d)] * 7 + state_out,
        out_shape=[tok] * 7 + state_shapes,
        scratch_shapes=scratch,
        compiler_params=pltpu.CompilerParams(
            dimension_semantics=sem, vmem_limit_bytes=VMEM_LIMIT),
        name="mix_in_decode" if decode else "mix_in",
    )(x, *consts, sh0, c0a, c0b, *lora)


def _head_sum(x, first):
    x1 = jnp.where(first, x, 0.0)
    s1 = jnp.sum(x1, axis=-1, keepdims=True)
    s2 = jnp.sum(x - x1, axis=-1, keepdims=True)
    return jnp.where(first, s1, s2)


def _stack_heads(x):
    x = x.astype(BF16)
    lane = lax.broadcasted_iota(jnp.int32, x.shape, 1) % LANES
    first = lane < HEAD
    zero = jnp.zeros_like(x)
    return jnp.concatenate([jnp.where(first, x, zero), jnp.where(first, zero, x)], axis=0)


def _apply(mat, x):
    return jnp.dot(mat.astype(BF16), _stack_heads(x), preferred_element_type=F32)


def _split2(x):
    hi = x.astype(BF16)
    return hi, (x - hi.astype(F32)).astype(BF16)


def _split3(x):
    hi = x.astype(BF16)
    rest = x - hi.astype(F32)
    mid = rest.astype(BF16)
    return hi, mid, (rest - mid.astype(F32)).astype(BF16)


def _wkv_kernel(r_ref, k_ref, v_ref, al_ref, ld_ref, kk_ref, ka_ref, rk_ref,
                gw_ref, gb_ref, y_ref, st_ref, h_ref, *, group):
    c = pl.program_id(1)
    n_pairs = h_ref.shape[0]

    @pl.when(c == 0)
    def _():
        h_ref[...] = jnp.zeros_like(h_ref)

    C = CHUNK
    row = lax.broadcasted_iota(jnp.int32, (C, LANES), 0)
    col = lax.broadcasted_iota(jnp.int32, (C, LANES), 1)
    first = col < HEAD
    j = col % HEAD
    strict = j < row
    incl = j <= row
    cs_r = lax.broadcasted_iota(jnp.int32, (C, 3 * C), 0)
    cs_c = lax.broadcasted_iota(jnp.int32, (C, 3 * C), 1) % C
    tri3 = (cs_c <= cs_r).astype(BF16)
    sq_r = lax.broadcasted_iota(jnp.int32, (LANES, LANES), 0)
    sq_c = lax.broadcasted_iota(jnp.int32, (LANES, LANES), 1)
    same_head = (sq_r // HEAD) == (sq_c // HEAD)
    eye = sq_r == sq_c

    def prefix(q, p):
        rows = slice(q * C, (q + 1) * C)
        sl = slice(p * LANES, (p + 1) * LANES)
        ld = ld_ref[rows, sl]
        lp = jnp.dot(tri3, jnp.concatenate(_split3(ld), axis=0), preferred_element_type=F32)
        return dict(rows=rows, sl=sl, ld=ld, lp=lp)

    def scores(st):
        rows, sl, ld, lp = st["rows"], st["sl"], st["ld"], st["lp"]
        r = r_ref[rows, sl]
        k = k_ref[rows, sl]
        v = v_ref[rows, sl]
        al = al_ref[rows, sl]
        lp_end = lp[C - 1:C, :]
        kkr = k * kk_ref[:, sl]
        nrm = jnp.sqrt(_head_sum(kkr * kkr, first))
        kk = kkr / jnp.maximum(nrm, NORM_EPS)
        kmod = k * (1.0 + (al - 1.0) * ka_ref[:, sl])
        bv = kk * al
        e_neg = jnp.exp(-lp)
        e_end = jnp.exp(lp_end - lp)
        st.update(v=v, at=kk * jnp.exp(lp - ld), rt=r * jnp.exp(lp),
                  bh=bv * e_end, kh=kmod * e_end, p_end=jnp.exp(lp_end),
                  bonus=_head_sum(r * kmod * rk_ref[:, sl], first) * v)
        sc = lax.dot_general(
            jnp.concatenate([st["at"], st["rt"]], axis=0).astype(BF16),
            jnp.concatenate([_stack_heads(bv * e_neg), _stack_heads(kmod * e_neg)], axis=0),
            (((1,), (1,)), ((), ())), preferred_element_type=F32)
        st["lab"] = jnp.where(strict, sc[0:C, 0:LANES], 0.0)
        st["lak"] = jnp.where(strict, sc[0:C, LANES:2 * LANES], 0.0)
        st["mrb"] = jnp.where(incl, sc[C:2 * C, 0:LANES], 0.0)
        st["mrk"] = jnp.where(incl, sc[C:2 * C, LANES:2 * LANES], 0.0)
        st["inv"] = (jnp.where(j == row, 1.0, 0.0)
                     - jnp.where((j // 2) == (row // 2), st["lab"], 0.0))

    def double_left(st, s):
        off = jnp.where(((j // (2 * s)) == (row // (2 * s))) & ((j // s) != (row // s)),
                        st["lab"], 0.0)
        st["inv_off"] = _apply(st["inv"], off)

    def double_right(st, s):
        st["inv"] = st["inv"] - _apply(st["inv_off"], st["inv"])

    def values(st):
        st["xv"] = _apply(jnp.concatenate([st["lak"], st["mrk"]], axis=0), st["v"])
        lhs_t = jnp.concatenate([st["kh"], -st["bh"]], axis=0).T
        l_hi, l_lo = _split2(lhs_t)
        st["lhs3"] = jnp.concatenate([l_hi, l_hi, l_lo], axis=1)

    def solve(st):
        st["uw"] = _apply(st["inv"], jnp.concatenate([st["at"], st["xv"][0:C]], axis=1))

    def outputs(st):
        uw, v = st["uw"], st["v"]
        mu = _apply(st["mrb"], uw)
        st["q"] = (st["rt"] - mu[:, 0:LANES]).astype(BF16)
        st["y0"] = st["xv"][C:2 * C] - mu[:, LANES:2 * LANES]
        rhs = jnp.concatenate(
            [jnp.concatenate([v, jnp.zeros_like(v)], axis=1),
             jnp.concatenate([uw[:, LANES:2 * LANES], -uw[:, 0:LANES]], axis=1)], axis=0)
        r_hi, r_lo = _split2(rhs)
        gb = jnp.dot(st["lhs3"], jnp.concatenate([r_hi, r_lo, r_hi], axis=0),
                     preferred_element_type=F32)
        st["g_c"] = jnp.where(same_head, gb[:, 0:LANES], 0.0)
        st["a_c"] = jnp.where(same_head, jnp.where(eye, st["p_end"], 0.0)
                              - gb[:, LANES:2 * LANES], 0.0).astype(BF16)

    def advance(p, st):
        h0_b = h_ref[p].astype(BF16)
        st["y"] = jnp.dot(st["q"], h0_b, preferred_element_type=F32) + st["y0"]
        h_ref[p] = jnp.dot(st["a_c"], h0_b, preferred_element_type=F32) + st["g_c"]

    def normalize(st):
        sl, y = st["sl"], st["y"]
        mean = _head_sum(y, first) * (1.0 / HEAD)
        yc = y - mean
        var = _head_sum(yc * yc, first) * (1.0 / HEAD)
        yn = yc * lax.rsqrt(var + GN_EPS) * gw_ref[:, sl] + gb_ref[:, sl]
        y_ref[st["rows"], sl] = yn + st["bonus"]

    n_chunks = r_ref.shape[0] // C
    for g0 in range(0, n_pairs, group):
        units = [(q, p) for q in range(n_chunks) for p in range(g0, min(g0 + group, n_pairs))]
        sts = [prefix(q, p) for q, p in units]
        for stage in (scores, values):
            for st in sts:
                stage(st)
        s = 2
        while s < C:
            for stage in (double_left, double_right):
                for st in sts:
                    stage(st, s)
            s *= 2
        for stage in (solve, outputs):
            for st in sts:
                stage(st)
        for (_, p), st in zip(units, sts):
            advance(p, st)
        for st in sts:
            normalize(st)

    @pl.when(c == pl.num_programs(1) - 1)
    def _():
        for p in range(n_pairs):
            s_pair = h_ref[p].T
            st_ref[0, 2 * p] = s_pair[0:HEAD, 0:HEAD]
            st_ref[0, 2 * p + 1] = pltpu.roll(s_pair, HEAD, 1)[HEAD:2 * HEAD, 0:HEAD]


def _wkv(r, k, v, al, ld, kk, ka, rk, gw, gb, *, batch, seq):
    m, d = r.shape
    rows = WKV_CHUNKS_PER_STEP * CHUNK
    nc = seq // rows
    n_pairs = d // LANES
    tok = pl.BlockSpec((rows, d), lambda b, c: (b * nc + c, 0))
    par = pl.BlockSpec((1, d), lambda b, c: (0, 0))
    return pl.pallas_call(
        functools.partial(_wkv_kernel, group=WKV_GROUP),
        grid=(batch, nc),
        in_specs=[tok] * 5 + [par] * 5,
        out_specs=[tok, pl.BlockSpec((1, 2 * n_pairs, HEAD, HEAD), lambda b, c: (b, 0, 0, 0))],
        out_shape=[jax.ShapeDtypeStruct((m, d), F32),
                   jax.ShapeDtypeStruct((batch, 2 * n_pairs, HEAD, HEAD), F32)],
        scratch_shapes=[pltpu.VMEM((n_pairs, LANES, LANES), F32)],
        compiler_params=pltpu.CompilerParams(
            dimension_semantics=("arbitrary", "arbitrary"), vmem_limit_bytes=VMEM_LIMIT),
        name="wkv",
    )(r, k, v, al, ld, kk, ka, rk, gw, gb)


def _wkv_step_kernel(r_ref, k_ref, v_ref, al_ref, ld_ref, kk_ref, ka_ref, rk_ref,
                     gw_ref, gb_ref, s_ref, y_ref, so_ref):
    hb, _, _, nb = s_ref.shape
    r_t, k_t, v_t, al_t, ld_t = [ref[...].T for ref in (r_ref, k_ref, v_ref, al_ref, ld_ref)]
    diag = (lax.broadcasted_iota(jnp.int32, (HEAD, HEAD, nb), 0)
            == lax.broadcasted_iota(jnp.int32, (HEAD, HEAD, nb), 1))
    ys = []
    for i in range(hb):
        rows = slice(i * HEAD, (i + 1) * HEAD)
        r, k, v, al = r_t[rows], k_t[rows], v_t[rows], al_t[rows]
        decay = jnp.exp(ld_t[rows])
        kkr = k * kk_ref[rows, :]
        nrm = jnp.sqrt(jnp.sum(kkr * kkr, axis=0, keepdims=True))
        kk = kkr / jnp.maximum(nrm, NORM_EPS)
        kmod = k * (1.0 + (al - 1.0) * ka_ref[rows, :])
        bv = kk * al
        s = s_ref[i]
        sa = jnp.sum(s * kk[None], axis=1, keepdims=True)
        v3 = jnp.sum(jnp.where(diag, v[None], 0.0), axis=1, keepdims=True)
        s_new = s * decay[None] - sa * bv[None] + v3 * kmod[None]
        so_ref[i] = s_new
        y3 = jnp.sum(s_new * r[None], axis=1, keepdims=True)
        y = jnp.sum(jnp.where(diag, y3, 0.0), axis=0)
        mean = jnp.mean(y, axis=0, keepdims=True)
        yc = y - mean
        var = jnp.mean(yc * yc, axis=0, keepdims=True)
        yn = yc * lax.rsqrt(var + GN_EPS) * gw_ref[rows, :] + gb_ref[rows, :]
        bonus = jnp.sum(r * kmod * rk_ref[rows, :], axis=0, keepdims=True) * v
        ys.append(yn + bonus)
    y_ref[...] = jnp.concatenate(ys, axis=0).T


def _wkv_step(r, k, v, al, ld, kk, ka, rk, gw, gb, s0, hb):
    nh, _, _, b = s0.shape
    d = r.shape[1]
    tok = pl.BlockSpec((b, hb * HEAD), lambda i: (0, i))
    par = pl.BlockSpec((hb * HEAD, 1), lambda i: (i, 0))
    st = pl.BlockSpec((hb, HEAD, HEAD, b), lambda i: (i, 0, 0, 0))
    return pl.pallas_call(
        _wkv_step_kernel,
        grid=(nh // hb,),
        in_specs=[tok] * 5 + [par] * 5 + [st],
        out_specs=[tok, st],
        out_shape=[jax.ShapeDtypeStruct((b, d), F32), jax.ShapeDtypeStruct(s0.shape, F32)],
        compiler_params=pltpu.CompilerParams(
            dimension_semantics=("arbitrary",), vmem_limit_bytes=VMEM_LIMIT),
        name="wkv_step",
    )(r, k, v, al, ld, kk, ka, rk, gw, gb, s0)


def _tail_kernel(x_ref, ya_ref, gg_ref, gyb_ref, p_ref, wo_ref, nf_ref, wg_ref, wu_ref,
                 wd_ref, npl_ref, wpg_ref, wpp_ref, nfin_ref, o_ref, *, final):
    merged = ya_ref[...] * gg_ref[...] + gyb_ref[...]
    x = x_ref[...] + _bdot(merged, wo_ref[...])
    x = _swiglu_half(x, nf_ref[...], wg_ref, wu_ref, wd_ref)
    hp = _rms(x, npl_ref[...])
    x = x + _sigmoid(_bdot(hp, wpg_ref[...])) * _bdot(p_ref[...], wpp_ref[...])
    o_ref[...] = _rms(x, nfin_ref[...]) if final else x


def _tail(x, ya, gg, gyb, p, wo, nf, wg, wu, wd, npl, wpg, wpp, nfin, *, final, tm):
    m, d = x.shape
    row = lambda w: pl.BlockSpec((tm, w), lambda i: (i, 0))
    weights = [wo, nf, wg, wu, wd, npl, wpg, wpp, nfin]
    return pl.pallas_call(
        functools.partial(_tail_kernel, final=final),
        grid=(m // tm,),
        in_specs=[row(d)] * 4 + [row(p.shape[1])] + [_const_spec(w.shape) for w in weights],
        out_specs=row(d),
        out_shape=jax.ShapeDtypeStruct((m, d), F32),
        compiler_params=pltpu.CompilerParams(
            dimension_semantics=("arbitrary",), vmem_limit_bytes=VMEM_LIMIT),
        name="tail",
    )(x, ya, gg, gyb, p, *weights)


def _layer(x, p, s0, sh0, cv0, w, *, decode, batch, seq, final):
    d = x.shape[1]
    nh = d // HEAD
    x1 = _ffn(x, w["norm_ffn1"], w["ffn1_gate"], w["ffn1_up"], w["ffn1_down"],
              tm=min(512, x.shape[0]))
    if decode:
        sh_in, c0a, c0b = sh0, cv0[:, 0], cv0[:, 1]
    else:
        sh_in, c0a, c0b = sh0[:, None], cv0[:, 0:1], cv0[:, 1:2]
    r, k, v, al, ld, gg, gyb, sh_new, cv_new = _mix_in(
        x1, w["norm_mix"], w["w_in"], w["mu"], sh_in, c0a, c0b, w["wdec"], w["wicl"],
        w["wgat"], w["decay_bias"], w["iclr_bias"], w["conv_w"],
        decode=decode, batch=batch, seq=seq, tm=min(256, x.shape[0]))
    pars = [w["k_k"], w["k_a"], w["r_k"], w["gn_w"], w["gn_b"]]
    if decode:
        ya, s_new = _wkv_step(r, k, v, al, ld, *[a.reshape(d, 1) for a in pars],
                              jnp.transpose(s0, (1, 2, 3, 0)), hb=2)
        s_new = jnp.transpose(s_new, (3, 0, 1, 2))
        cv_new = jnp.stack([cv0[:, 1], cv_new], axis=1)
    else:
        ya, s_new = _wkv(r, k, v, al, ld, *pars, batch=batch, seq=seq)
        sh_new = sh_new[:, 0]
    x_out = _tail(x1, ya, gg, gyb, p, w["w_out"], w["norm_ffn2"], w["ffn2_gate"],
                  w["ffn2_up"], w["ffn2_down"], w["norm_ple"], w["ple_gate"], w["ple_proj"],
                  w["norm_final"], final=final, tm=min(256, x.shape[0]))
    return x_out, s_new, sh_new, cv_new


def kernel(x_prompt, x_sample, p_prompt, p_sample, state_wkv, state_shift, state_conv,
           norm_ffn1, w_ffn1_gate, w_ffn1_up, w_ffn1_down, norm_mix, w_in, mu_shift,
           w_decay_up, decay_bias, w_iclr_up, iclr_bias, w_gate_up, k_k, k_a, r_k, gn_w,
           gn_b, conv_w, w_out, norm_ffn2, w_ffn2_gate, w_ffn2_up, w_ffn2_down, norm_ple,
           w_ple_gate, w_ple_proj, norm_final):
    depth = w_in.shape[0]
    bp, tp, d = x_prompt.shape
    bs, ts, _ = x_sample.shape
    assert ts == 1 and tp % (WKV_CHUNKS_PER_STEP * CHUNK) == 0
    shift_cols = mu_shift.shape[1]
    n_dec, n_icl, n_gat = w_decay_up.shape[1], w_iclr_up.shape[1], w_gate_up.shape[1]
    assert 3 * d + n_dec + n_icl + n_gat == shift_cols

    xp = x_prompt.reshape(bp * tp, d)
    xs = x_sample.reshape(bs, d)
    vec = lambda a: a.reshape(1, -1)
    outs = [[] for _ in range(6)]
    for i in range(depth):
        n_tail = shift_cols - 3 * d
        pad = lambda a, lo: jnp.zeros((n_tail, d), BF16).at[lo:lo + a.shape[0]].set(a.astype(BF16))
        w = dict(
            norm_ffn1=vec(norm_ffn1[i]), ffn1_gate=w_ffn1_gate[i].astype(BF16),
            ffn1_up=w_ffn1_up[i].astype(BF16), ffn1_down=w_ffn1_down[i].astype(BF16),
            norm_mix=vec(norm_mix[i]), w_in=w_in[i].astype(BF16), mu=vec(mu_shift[i]),
            wdec=pad(w_decay_up[i], 0), wicl=pad(w_iclr_up[i], n_dec),
            wgat=pad(w_gate_up[i], n_dec + n_icl),
            decay_bias=vec(decay_bias[i]), iclr_bias=vec(iclr_bias[i]), conv_w=conv_w[i],
            k_k=vec(k_k[i]), k_a=vec(k_a[i]), r_k=vec(r_k[i]), gn_w=vec(gn_w[i]),
            gn_b=vec(gn_b[i]), w_out=w_out[i].astype(BF16), norm_ffn2=vec(norm_ffn2[i]),
            ffn2_gate=w_ffn2_gate[i].astype(BF16), ffn2_up=w_ffn2_up[i].astype(BF16),
            ffn2_down=w_ffn2_down[i].astype(BF16), norm_ple=vec(norm_ple[i]),
            ple_gate=w_ple_gate[i].astype(BF16), ple_proj=w_ple_proj[i].astype(BF16),
            norm_final=vec(norm_final))
        final = i == depth - 1
        xp, s_p, sh_p, cv_p = _layer(
            xp, p_prompt[i].reshape(bp * tp, -1), None,
            jnp.zeros((bp, shift_cols), F32), jnp.zeros((bp, 2, d), F32), w,
            decode=False, batch=bp, seq=tp, final=final)
        xs, s_s, sh_s, cv_s = _layer(
            xs, p_sample[i].reshape(bs, -1), state_wkv[i], state_shift[i], state_conv[i], w,
            decode=True, batch=bs, seq=1, final=final)
        for lst, val in zip(outs, (s_p, sh_p, cv_p, s_s, sh_s, cv_s)):
            lst.append(val)
    stk = [jnp.stack(o, 0) for o in outs]
    return (xp.reshape(bp, tp, d), xs.reshape(bs, ts, d), stk[0], stk[1], stk[2],
            stk[3], stk[4], stk[5])
```

```python
import functools

import jax
import jax.numpy as jnp
from jax import lax
from jax.experimental import pallas as pl
from jax.experimental.pallas import tpu as pltpu

F32 = jnp.float32
BF16 = jnp.bfloat16

HEAD = 64
LANES = 128
CHUNK = 64
MIX_COL_BLOCK = 256
WKV_CHUNKS_PER_STEP = 4
WKV_CHUNKS_PER_GROUP = 2
WKV_STAGE_LAG = 5
RMS_EPS = 1e-6
GN_EPS = 64e-5
NORM_EPS = 1e-12
VMEM_LIMIT = 56 * 1024 * 1024


def _rms(x, g):
    return x * lax.rsqrt(jnp.mean(x * x, axis=-1, keepdims=True) + RMS_EPS) * g


def _sigmoid(x):
    return 1.0 / (1.0 + jnp.exp(-x))


def _softplus(x):
    return jnp.maximum(x, 0.0) + jnp.log(1.0 + jnp.exp(-jnp.abs(x)))


def _bdot(a, b):
    return jnp.dot(a.astype(BF16), b, preferred_element_type=F32)


def _const_spec(shape):
    zeros = (0,) * len(shape)
    return pl.BlockSpec(shape, lambda *_: zeros, pipeline_mode=pl.Buffered(1))


def _swiglu_half(x, g, wg_ref, wu_ref, wd_ref):
    h = _rms(x, g).astype(BF16)
    gate = jnp.dot(h, wg_ref[...], preferred_element_type=F32)
    up = jnp.dot(h, wu_ref[...], preferred_element_type=F32)
    act = gate * _sigmoid(gate) * up
    return x + 0.5 * _bdot(act, wd_ref[...])


def _ffn_kernel(x_ref, g_ref, wg_ref, wu_ref, wd_ref, o_ref):
    o_ref[...] = _swiglu_half(x_ref[...], g_ref[...], wg_ref, wu_ref, wd_ref)


def _ffn(x, g, wg, wu, wd, tm):
    m, d = x.shape
    row = pl.BlockSpec((tm, d), lambda i: (i, 0))
    return pl.pallas_call(
        _ffn_kernel,
        grid=(m // tm,),
        in_specs=[row, _const_spec(g.shape), _const_spec(wg.shape),
                  _const_spec(wu.shape), _const_spec(wd.shape)],
        out_specs=row,
        out_shape=jax.ShapeDtypeStruct((m, d), F32),
        compiler_params=pltpu.CompilerParams(
            dimension_semantics=("arbitrary",), vmem_limit_bytes=VMEM_LIMIT),
        name="ffn",
    )(x, g, wg, wu, wd)


def _mix_in_kernel(x_ref, nm_ref, w_in_ref, mu_ref, sh0_ref, c0a_ref, c0b_ref,
                   wdec_ref, wicl_ref, wgat_ref, dbias_ref, ibias_ref, cw_ref,
                   r_ref, k_ref, v_ref, al_ref, ld_ref, gg_ref, gyb_ref,
                   sh_out_ref, cv_out_ref, *scratch, decode, d, shift_cols, blk):
    tm = x_ref.shape[0]
    h = _rms(x_ref[...], nm_ref[...]).astype(BF16)
    conv0, gate0 = shift_cols, shift_cols + 3 * d

    if not decode:
        carry_sh, carry_cu = scratch

        @pl.when(pl.program_id(1) == 0)
        def _():
            carry_sh[0:1, :] = sh0_ref[0]
            carry_cu[0:1, :] = c0a_ref[0]
            carry_cu[1:2, :] = c0b_ref[0]

        row = lax.broadcasted_iota(jnp.int32, (tm, 1), 0)

    def proj(c0, width):
        return jnp.dot(h, w_in_ref[:, c0:c0 + width], preferred_element_type=F32)

    def shifted(c0, width):
        cols = slice(c0, c0 + width)
        z = proj(c0, width)
        if decode:
            z_prev = sh0_ref[:, cols]
            sh_out_ref[:, cols] = z
        else:
            z_prev = jnp.where(row == 0, carry_sh[0:1, cols], pltpu.roll(z, 1, 0))
            carry_sh[0:1, cols] = z[tm - 1:tm, :]
            sh_out_ref[0, :, cols] = z[tm - 1:tm, :]
        return z + mu_ref[:, cols] * (z_prev - z)

    tail = shifted(3 * d, shift_cols - 3 * d)
    tail_tanh = jnp.tanh(tail).astype(BF16)
    tail_sig = _sigmoid(tail).astype(BF16)
    tail_raw = tail.astype(BF16)
    for c in range(0, d, blk):
        cols = slice(c, c + blk)
        r_ref[:, cols] = shifted(c, blk)
        k_ref[:, cols] = shifted(d + c, blk)
        v_ref[:, cols] = shifted(2 * d + c, blk)
        dec = dbias_ref[:, cols] + jnp.dot(tail_tanh, wdec_ref[:, cols],
                                           preferred_element_type=F32)
        ld_ref[:, cols] = -jnp.exp(-_softplus(-dec) - 0.5)
        al_ref[:, cols] = _sigmoid(ibias_ref[:, cols] + jnp.dot(
            tail_raw, wicl_ref[:, cols], preferred_element_type=F32))
        gate = jnp.dot(tail_sig, wgat_ref[:, cols], preferred_element_type=F32)
        gg_ref[:, cols] = gate * _sigmoid(proj(gate0 + c, blk))
        cu = proj(conv0 + d + c, blk) * proj(conv0 + 2 * d + c, blk)
        if decode:
            cu_m2 = c0a_ref[:, cols]
            cu_m1 = c0b_ref[:, cols]
            cv_out_ref[:, cols] = cu
        else:
            cu_m1 = jnp.where(row == 0, carry_cu[1:2, cols], pltpu.roll(cu, 1, 0))
            cu_m2 = jnp.where(row == 0, carry_cu[0:1, cols],
                              jnp.where(row == 1, carry_cu[1:2, cols], pltpu.roll(cu, 2, 0)))
            carry_cu[0:2, cols] = cu[tm - 2:tm, :]
            cv_out_ref[0, :, cols] = cu[tm - 2:tm, :]
        conv = cu_m2 * cw_ref[0:1, cols] + cu_m1 * cw_ref[1:2, cols] + cu * cw_ref[2:3, cols]
        gyb_ref[:, cols] = _sigmoid(proj(gate0 + d + c, blk)) * (proj(conv0 + c, blk) * conv)


def _mix_in(x, nm, w_in, mu, sh0, c0a, c0b, wdec, wicl, wgat, dbias, ibias, cw,
            *, decode, batch, seq, tm):
    m, d = x.shape
    shift_cols = mu.shape[1]
    consts = [nm, w_in, mu]
    lora = [wdec, wicl, wgat, dbias, ibias, cw]
    if decode:
        grid = (1,)
        row = lambda w: pl.BlockSpec((m, w), lambda i: (0, 0))
        state_specs = [row(shift_cols), row(d), row(d)]
        state_out = [row(shift_cols), row(d)]
        state_shapes = [jax.ShapeDtypeStruct((m, shift_cols), F32),
                        jax.ShapeDtypeStruct((m, d), F32)]
        scratch = []
        sem = ("arbitrary",)
    else:
        nt = seq // tm
        grid = (batch, nt)
        row = lambda w: pl.BlockSpec((tm, w), lambda b, t: (b * nt + t, 0))
        per_b = lambda r, w: pl.BlockSpec((1, r, w), lambda b, t: (b, 0, 0))
        state_specs = [per_b(1, shift_cols), per_b(1, d), per_b(1, d)]
        state_out = [per_b(1, shift_cols), per_b(2, d)]
        state_shapes = [jax.ShapeDtypeStruct((batch, 1, shift_cols), F32),
                        jax.ShapeDtypeStruct((batch, 2, d), F32)]
        scratch = [pltpu.VMEM((8, shift_cols), F32), pltpu.VMEM((8, d), F32)]
        sem = ("arbitrary", "arbitrary")
    tok = jax.ShapeDtypeStruct((m, d), F32)
    return pl.pallas_call(
        functools.partial(_mix_in_kernel, decode=decode, d=d, shift_cols=shift_cols,
                          blk=MIX_COL_BLOCK),
        grid=grid,
        in_specs=([row(d)] + [_const_spec(a.shape) for a in consts] + state_specs
                  + [_const_spec(a.shape) for a in lora]),
        out_specs=[row(d)] * 7 + state_out,
        out_shape=[tok] * 7 + state_shapes,
        scratch_shapes=scratch,
        compiler_params=pltpu.CompilerParams(
            dimension_semantics=sem, vmem_limit_bytes=VMEM_LIMIT),
        name="mix_in_decode" if decode else "mix_in",
    )(x, *consts, sh0, c0a, c0b, *lora)


def _head_sum(x, first):
    x1 = jnp.where(first, x, 0.0)
    s1 = jnp.sum(x1, axis=-1, keepdims=True)
    s2 = jnp.sum(x - x1, axis=-1, keepdims=True)
    return jnp.where(first, s1, s2)


def _stack_heads(x):
    x = x.astype(BF16)
    lane = lax.broadcasted_iota(jnp.int32, x.shape, 1) % LANES
    first = lane < HEAD
    zero = jnp.zeros_like(x)
    return jnp.concatenate([jnp.where(first, x, zero), jnp.where(first, zero, x)], axis=0)


def _apply(mat, x):
    return jnp.dot(mat.astype(BF16), _stack_heads(x), preferred_element_type=F32)


def _split2(x):
    hi = x.astype(BF16)
    return hi, (x - hi.astype(F32)).astype(BF16)


def _split3(x):
    hi = x.astype(BF16)
    rest = x - hi.astype(F32)
    mid = rest.astype(BF16)
    return hi, mid, (rest - mid.astype(F32)).astype(BF16)


def _wkv_kernel(r_ref, k_ref, v_ref, al_ref, ld_ref, gg_ref, gyb_ref, kk_ref, ka_ref, rk_ref,
                gw_ref, gb_ref, y_ref, st_ref, h_ref, *, group, lag):
    c = pl.program_id(1)
    n_pairs = h_ref.shape[0]

    @pl.when(c == 0)
    def _():
        h_ref[...] = jnp.zeros_like(h_ref)

    C = CHUNK
    row = lax.broadcasted_iota(jnp.int32, (C, LANES), 0)
    col = lax.broadcasted_iota(jnp.int32, (C, LANES), 1)
    first = col < HEAD
    j = col % HEAD
    strict = j < row
    incl = j <= row
    cs_r = lax.broadcasted_iota(jnp.int32, (C, 3 * C), 0)
    cs_c = lax.broadcasted_iota(jnp.int32, (C, 3 * C), 1) % C
    tri3 = (cs_c <= cs_r).astype(BF16)
    sq_r = lax.broadcasted_iota(jnp.int32, (LANES, LANES), 0)
    sq_c = lax.broadcasted_iota(jnp.int32, (LANES, LANES), 1)
    same_head = (sq_r // HEAD) == (sq_c // HEAD)
    eye = sq_r == sq_c

    def prefix(q, p):
        rows = slice(q * C, (q + 1) * C)
        sl = slice(p * LANES, (p + 1) * LANES)
        ld = ld_ref[rows, sl]
        lp = jnp.dot(tri3, jnp.concatenate(_split3(ld), axis=0), preferred_element_type=F32)
        return dict(p=p, rows=rows, sl=sl, ld=ld, lp=lp)

    def scores(st):
        rows, sl, ld, lp = st["rows"], st["sl"], st["ld"], st["lp"]
        r = r_ref[rows, sl]
        k = k_ref[rows, sl]
        v = v_ref[rows, sl]
        al = al_ref[rows, sl]
        lp_end = lp[C - 1:C, :]
        kkr = k * kk_ref[:, sl]
        nrm = jnp.sqrt(_head_sum(kkr * kkr, first))
        kk = kkr / jnp.maximum(nrm, NORM_EPS)
        kmod = k * (1.0 + (al - 1.0) * ka_ref[:, sl])
        bv = kk * al
        e_neg = jnp.exp(-lp)
        e_end = jnp.exp(lp_end - lp)
        st.update(v=v, at=kk * jnp.exp(lp - ld), rt=r * jnp.exp(lp),
                  bh=bv * e_end, kh=kmod * e_end, p_end=jnp.exp(lp_end),
                  bonus=_head_sum(r * kmod * rk_ref[:, sl], first) * v)
        sc = lax.dot_general(
            jnp.concatenate([st["at"], st["rt"]], axis=0).astype(BF16),
            jnp.concatenate([_stack_heads(bv * e_neg), _stack_heads(kmod * e_neg)], axis=0),
            (((1,), (1,)), ((), ())), preferred_element_type=F32)
        st["lab"] = jnp.where(strict, sc[0:C, 0:LANES], 0.0)
        st["lak"] = jnp.where(strict, sc[0:C, LANES:2 * LANES], 0.0)
        st["mrb"] = jnp.where(incl, sc[C:2 * C, 0:LANES], 0.0)
        st["mrk"] = jnp.where(incl, sc[C:2 * C, LANES:2 * LANES], 0.0)
        st["inv"] = (jnp.where(j == row, 1.0, 0.0)
                     - jnp.where((j // 2) == (row // 2), st["lab"], 0.0))

    def double_left(st, s):
        off = jnp.where(((j // (2 * s)) == (row // (2 * s))) & ((j // s) != (row // s)),
                        st["lab"], 0.0)
        st["inv_off"] = _apply(st["inv"], off)

    def double_right(st):
        st["inv"] = st["inv"] - _apply(st["inv_off"], st["inv"])

    def values(st):
        st["xv"] = _apply(jnp.concatenate([st["lak"], st["mrk"]], axis=0), st["v"])
        lhs_t = jnp.concatenate([st["kh"], -st["bh"]], axis=0).T
        l_hi, l_lo = _split2(lhs_t)
        st["lhs3"] = jnp.concatenate([l_hi, l_hi, l_lo], axis=1)

    def solve(st):
        st["uw"] = _apply(st["inv"], jnp.concatenate([st["at"], st["xv"][0:C]], axis=1))

    def outputs(st):
        uw, v = st["uw"], st["v"]
        mu = _apply(st["mrb"], uw)
        st["q"] = (st["rt"] - mu[:, 0:LANES]).astype(BF16)
        st["y0"] = st["xv"][C:2 * C] - mu[:, LANES:2 * LANES]
        rhs = jnp.concatenate(
            [jnp.concatenate([v, jnp.zeros_like(v)], axis=1),
             jnp.concatenate([uw[:, LANES:2 * LANES], -uw[:, 0:LANES]], axis=1)], axis=0)
        r_hi, r_lo = _split2(rhs)
        gb = jnp.dot(st["lhs3"], jnp.concatenate([r_hi, r_lo, r_hi], axis=0),
                     preferred_element_type=F32)
        st["g_c"] = jnp.where(same_head, gb[:, 0:LANES], 0.0)
        st["a_c"] = jnp.where(same_head, jnp.where(eye, st["p_end"], 0.0)
                              - gb[:, LANES:2 * LANES], 0.0).astype(BF16)

    def advance(st):
        p = st["p"]
        qa = jnp.dot(jnp.concatenate([st["q"], st["a_c"]], axis=0), h_ref[p].astype(BF16),
                     preferred_element_type=F32)
        st["y"] = qa[0:C] + st["y0"]
        h_ref[p] = qa[C:C + LANES] + st["g_c"]

    def normalize(st):
        sl, y = st["sl"], st["y"]
        mean = _head_sum(y, first) * (1.0 / HEAD)
        yc = y - mean
        var = _head_sum(yc * yc, first) * (1.0 / HEAD)
        yn = yc * lax.rsqrt(var + GN_EPS) * gw_ref[:, sl] + gb_ref[:, sl]
        merged = (yn + st["bonus"]) * gg_ref[st["rows"], sl] + gyb_ref[st["rows"], sl]
        y_ref[st["rows"], sl] = merged.astype(BF16)

    def program(chunks):
        units = []

        def s_prefix():
            units.extend(prefix(q, p) for q in chunks for p in range(n_pairs))

        def each(fn, *args):
            return lambda: [fn(st, *args) for st in units]

        stages = [s_prefix, each(scores), each(values)]
        s = 2
        while s < C:
            stages += [each(double_left, s), each(double_right)]
            s *= 2
        return stages + [each(solve), each(outputs), each(advance), each(normalize)]

    n_chunks = r_ref.shape[0] // C
    programs = [program(range(g, g + group)) for g in range(0, n_chunks, group)]
    for t in range(len(programs[0]) + lag * (len(programs) - 1)):
        for g, stages in enumerate(programs):
            if 0 <= t - g * lag < len(stages):
                stages[t - g * lag]()

    @pl.when(c == pl.num_programs(1) - 1)
    def _():
        for p in range(n_pairs):
            s_pair = h_ref[p].T
            st_ref[0, 2 * p] = s_pair[0:HEAD, 0:HEAD]
            st_ref[0, 2 * p + 1] = pltpu.roll(s_pair, HEAD, 1)[HEAD:2 * HEAD, 0:HEAD]


def _wkv(r, k, v, al, ld, gg, gyb, kk, ka, rk, gw, gb, *, batch, seq):
    m, d = r.shape
    rows = WKV_CHUNKS_PER_STEP * CHUNK
    nc = seq // rows
    n_pairs = d // LANES
    tok = pl.BlockSpec((rows, d), lambda b, c: (b * nc + c, 0))
    par = pl.BlockSpec((1, d), lambda b, c: (0, 0))
    return pl.pallas_call(
        functools.partial(_wkv_kernel, group=WKV_CHUNKS_PER_GROUP, lag=WKV_STAGE_LAG),
        grid=(batch, nc),
        in_specs=[tok] * 7 + [par] * 5,
        out_specs=[tok, pl.BlockSpec((1, 2 * n_pairs, HEAD, HEAD), lambda b, c: (b, 0, 0, 0))],
        out_shape=[jax.ShapeDtypeStruct((m, d), BF16),
                   jax.ShapeDtypeStruct((batch, 2 * n_pairs, HEAD, HEAD), F32)],
        scratch_shapes=[pltpu.VMEM((n_pairs, LANES, LANES), F32)],
        compiler_params=pltpu.CompilerParams(
            dimension_semantics=("arbitrary", "arbitrary"), vmem_limit_bytes=VMEM_LIMIT),
        name="wkv",
    )(r, k, v, al, ld, gg, gyb, kk, ka, rk, gw, gb)


def _wkv_step_kernel(r_ref, k_ref, v_ref, al_ref, ld_ref, gg_ref, gyb_ref, kk_ref, ka_ref,
                     rk_ref, gw_ref, gb_ref, s_ref, y_ref, so_ref):
    hb, _, _, nb = s_ref.shape
    r_t, k_t, v_t, al_t, ld_t = [ref[...].T for ref in (r_ref, k_ref, v_ref, al_ref, ld_ref)]
    diag = (lax.broadcasted_iota(jnp.int32, (HEAD, HEAD, nb), 0)
            == lax.broadcasted_iota(jnp.int32, (HEAD, HEAD, nb), 1))
    ys = []
    for i in range(hb):
        rows = slice(i * HEAD, (i + 1) * HEAD)
        r, k, v, al = r_t[rows], k_t[rows], v_t[rows], al_t[rows]
        decay = jnp.exp(ld_t[rows])
        kkr = k * kk_ref[rows, :]
        nrm = jnp.sqrt(jnp.sum(kkr * kkr, axis=0, keepdims=True))
        kk = kkr / jnp.maximum(nrm, NORM_EPS)
        kmod = k * (1.0 + (al - 1.0) * ka_ref[rows, :])
        bv = kk * al
        s = s_ref[i]
        sa = jnp.sum(s * kk[None], axis=1, keepdims=True)
        v3 = jnp.sum(jnp.where(diag, v[None], 0.0), axis=1, keepdims=True)
        s_new = s * decay[None] - sa * bv[None] + v3 * kmod[None]
        so_ref[i] = s_new
        y3 = jnp.sum(s_new * r[None], axis=1, keepdims=True)
        y = jnp.sum(jnp.where(diag, y3, 0.0), axis=0)
        mean = jnp.mean(y, axis=0, keepdims=True)
        yc = y - mean
        var = jnp.mean(yc * yc, axis=0, keepdims=True)
        yn = yc * lax.rsqrt(var + GN_EPS) * gw_ref[rows, :] + gb_ref[rows, :]
        bonus = jnp.sum(r * kmod * rk_ref[rows, :], axis=0, keepdims=True) * v
        ys.append(yn + bonus)
    ya = jnp.concatenate(ys, axis=0).T
    y_ref[...] = (ya * gg_ref[...] + gyb_ref[...]).astype(BF16)


def _wkv_step(r, k, v, al, ld, gg, gyb, kk, ka, rk, gw, gb, s0, hb):
    nh, _, _, b = s0.shape
    d = r.shape[1]
    tok = pl.BlockSpec((b, hb * HEAD), lambda i: (0, i))
    par = pl.BlockSpec((hb * HEAD, 1), lambda i: (i, 0))
    st = pl.BlockSpec((hb, HEAD, HEAD, b), lambda i: (i, 0, 0, 0))
    return pl.pallas_call(
        _wkv_step_kernel,
        grid=(nh // hb,),
        in_specs=[tok] * 7 + [par] * 5 + [st],
        out_specs=[tok, st],
        out_shape=[jax.ShapeDtypeStruct((b, d), BF16), jax.ShapeDtypeStruct(s0.shape, F32)],
        compiler_params=pltpu.CompilerParams(
            dimension_semantics=("arbitrary",), vmem_limit_bytes=VMEM_LIMIT),
        name="wkv_step",
    )(r, k, v, al, ld, gg, gyb, kk, ka, rk, gw, gb, s0)


def _tail_kernel(x_ref, mg_ref, p_ref, wo_ref, nf_ref, wg_ref, wu_ref,
                 wd_ref, npl_ref, wpg_ref, wpp_ref, nfin_ref, o_ref, *, final):
    x = x_ref[...] + jnp.dot(mg_ref[...], wo_ref[...], preferred_element_type=F32)
    x = _swiglu_half(x, nf_ref[...], wg_ref, wu_ref, wd_ref)
    hp = _rms(x, npl_ref[...])
    x = x + _sigmoid(_bdot(hp, wpg_ref[...])) * _bdot(p_ref[...], wpp_ref[...])
    o_ref[...] = _rms(x, nfin_ref[...]) if final else x


def _tail(x, merged, p, wo, nf, wg, wu, wd, npl, wpg, wpp, nfin, *, final, tm):
    m, d = x.shape
    row = lambda w: pl.BlockSpec((tm, w), lambda i: (i, 0))
    weights = [wo, nf, wg, wu, wd, npl, wpg, wpp, nfin]
    return pl.pallas_call(
        functools.partial(_tail_kernel, final=final),
        grid=(m // tm,),
        in_specs=[row(d)] * 2 + [row(p.shape[1])] + [_const_spec(w.shape) for w in weights],
        out_specs=row(d),
        out_shape=jax.ShapeDtypeStruct((m, d), F32),
        compiler_params=pltpu.CompilerParams(
            dimension_semantics=("arbitrary",), vmem_limit_bytes=VMEM_LIMIT),
        name="tail",
    )(x, merged, p, *weights)


def _layer(x, p, s0, sh0, cv0, w, *, decode, batch, seq, final):
    d = x.shape[1]
    nh = d // HEAD
    x1 = _ffn(x, w["norm_ffn1"], w["ffn1_gate"], w["ffn1_up"], w["ffn1_down"],
              tm=min(512, x.shape[0]))
    if decode:
        sh_in, c0a, c0b = sh0, cv0[:, 0], cv0[:, 1]
    else:
        sh_in, c0a, c0b = sh0[:, None], cv0[:, 0:1], cv0[:, 1:2]
    r, k, v, al, ld, gg, gyb, sh_new, cv_new = _mix_in(
        x1, w["norm_mix"], w["w_in"], w["mu"], sh_in, c0a, c0b, w["wdec"], w["wicl"],
        w["wgat"], w["decay_bias"], w["iclr_bias"], w["conv_w"],
        decode=decode, batch=batch, seq=seq, tm=min(256, x.shape[0]))
    pars = [w["k_k"], w["k_a"], w["r_k"], w["gn_w"], w["gn_b"]]
    if decode:
        merged, s_new = _wkv_step(r, k, v, al, ld, gg, gyb, *[a.reshape(d, 1) for a in pars],
                                  jnp.transpose(s0, (1, 2, 3, 0)), hb=2)
        s_new = jnp.transpose(s_new, (3, 0, 1, 2))
        cv_new = jnp.stack([cv0[:, 1], cv_new], axis=1)
    else:
        merged, s_new = _wkv(r, k, v, al, ld, gg, gyb, *pars, batch=batch, seq=seq)
        sh_new = sh_new[:, 0]
    x_out = _tail(x1, merged, p, w["w_out"], w["norm_ffn2"], w["ffn2_gate"],
                  w["ffn2_up"], w["ffn2_down"], w["norm_ple"], w["ple_gate"], w["ple_proj"],
                  w["norm_final"], final=final, tm=min(512, x.shape[0]))
    return x_out, s_new, sh_new, cv_new


def kernel(x_prompt, x_sample, p_prompt, p_sample, state_wkv, state_shift, state_conv,
           norm_ffn1, w_ffn1_gate, w_ffn1_up, w_ffn1_down, norm_mix, w_in, mu_shift,
           w_decay_up, decay_bias, w_iclr_up, iclr_bias, w_gate_up, k_k, k_a, r_k, gn_w,
           gn_b, conv_w, w_out, norm_ffn2, w_ffn2_gate, w_ffn2_up, w_ffn2_down, norm_ple,
           w_ple_gate, w_ple_proj, norm_final):
    depth = w_in.shape[0]
    bp, tp, d = x_prompt.shape
    bs, ts, _ = x_sample.shape
    assert ts == 1 and tp % (WKV_CHUNKS_PER_STEP * CHUNK) == 0
    shift_cols = mu_shift.shape[1]
    n_dec, n_icl, n_gat = w_decay_up.shape[1], w_iclr_up.shape[1], w_gate_up.shape[1]
    assert 3 * d + n_dec + n_icl + n_gat == shift_cols

    xp = x_prompt.reshape(bp * tp, d)
    xs = x_sample.reshape(bs, d)
    vec = lambda a: a.reshape(1, -1)
    outs = [[] for _ in range(6)]
    for i in range(depth):
        n_tail = shift_cols - 3 * d
        pad = lambda a, lo: jnp.zeros((n_tail, d), BF16).at[lo:lo + a.shape[0]].set(a.astype(BF16))
        w = dict(
            norm_ffn1=vec(norm_ffn1[i]), ffn1_gate=w_ffn1_gate[i].astype(BF16),
            ffn1_up=w_ffn1_up[i].astype(BF16), ffn1_down=w_ffn1_down[i].astype(BF16),
            norm_mix=vec(norm_mix[i]), w_in=w_in[i].astype(BF16), mu=vec(mu_shift[i]),
            wdec=pad(w_decay_up[i], 0), wicl=pad(w_iclr_up[i], n_dec),
            wgat=pad(w_gate_up[i], n_dec + n_icl),
            decay_bias=vec(decay_bias[i]), iclr_bias=vec(iclr_bias[i]), conv_w=conv_w[i],
            k_k=vec(k_k[i]), k_a=vec(k_a[i]), r_k=vec(r_k[i]), gn_w=vec(gn_w[i]),
            gn_b=vec(gn_b[i]), w_out=w_out[i].astype(BF16), norm_ffn2=vec(norm_ffn2[i]),
            ffn2_gate=w_ffn2_gate[i].astype(BF16), ffn2_up=w_ffn2_up[i].astype(BF16),
            ffn2_down=w_ffn2_down[i].astype(BF16), norm_ple=vec(norm_ple[i]),
            ple_gate=w_ple_gate[i].astype(BF16), ple_proj=w_ple_proj[i].astype(BF16),
            norm_final=vec(norm_final))
        final = i == depth - 1
        xp, s_p, sh_p, cv_p = _layer(
            xp, p_prompt[i].reshape(bp * tp, -1), None,
            jnp.zeros((bp, shift_cols), F32), jnp.zeros((bp, 2, d), F32), w,
            decode=False, batch=bp, seq=tp, final=final)
        xs, s_s, sh_s, cv_s = _layer(
            xs, p_sample[i].reshape(bs, -1), state_wkv[i], state_shift[i], state_conv[i], w,
            decode=True, batch=bs, seq=1, final=final)
        for lst, val in zip(outs, (s_p, sh_p, cv_p, s_s, sh_s, cv_s)):
            lst.append(val)
    stk = [jnp.stack(o, 0) for o in outs]
    return (xp.reshape(bp, tp, d), xs.reshape(bs, ts, d), stk[0], stk[1], stk[2],
            stk[3], stk[4], stk[5])
```

```python
import functools

import jax
import jax.numpy as jnp
from jax import lax
from jax.experimental import pallas as pl
from jax.experimental.pallas import tpu as pltpu

F32 = jnp.float32
BF16 = jnp.bfloat16

HEAD = 64
LANES = 128
CHUNK = 64
MIX_COL_BLOCK = 256
WKV_CHUNKS_PER_STEP = 4
WKV_CHUNKS_PER_GROUP = 2
WKV_STAGE_LAG = 7
RMS_EPS = 1e-6
GN_EPS = 64e-5
NORM_EPS = 1e-12
VMEM_LIMIT = 56 * 1024 * 1024


def _rms(x, g):
    return x * lax.rsqrt(jnp.mean(x * x, axis=-1, keepdims=True) + RMS_EPS) * g


def _sigmoid(x):
    return 1.0 / (1.0 + jnp.exp(-x))


def _softplus(x):
    return jnp.maximum(x, 0.0) + jnp.log(1.0 + jnp.exp(-jnp.abs(x)))


def _bdot(a, b):
    return jnp.dot(a.astype(BF16), b, preferred_element_type=F32)


def _const_spec(shape):
    zeros = (0,) * len(shape)
    return pl.BlockSpec(shape, lambda *_: zeros, pipeline_mode=pl.Buffered(1))


def _swiglu_half(x, g, wg_ref, wu_ref, wd_ref):
    h = _rms(x, g).astype(BF16)
    gate = jnp.dot(h, wg_ref[...], preferred_element_type=F32)
    up = jnp.dot(h, wu_ref[...], preferred_element_type=F32)
    act = gate * _sigmoid(gate) * up
    return x + 0.5 * _bdot(act, wd_ref[...])


def _ffn_kernel(x_ref, g_ref, wg_ref, wu_ref, wd_ref, o_ref):
    o_ref[...] = _swiglu_half(x_ref[...], g_ref[...], wg_ref, wu_ref, wd_ref)


def _ffn(x, g, wg, wu, wd, tm):
    m, d = x.shape
    row = pl.BlockSpec((tm, d), lambda i: (i, 0))
    return pl.pallas_call(
        _ffn_kernel,
        grid=(m // tm,),
        in_specs=[row, _const_spec(g.shape), _const_spec(wg.shape),
                  _const_spec(wu.shape), _const_spec(wd.shape)],
        out_specs=row,
        out_shape=jax.ShapeDtypeStruct((m, d), F32),
        compiler_params=pltpu.CompilerParams(
            dimension_semantics=("arbitrary",), vmem_limit_bytes=VMEM_LIMIT),
        name="ffn",
    )(x, g, wg, wu, wd)


def _mix_in_kernel(x_ref, nm_ref, w_in_ref, mu_ref, sh0_ref, c0a_ref, c0b_ref,
                   wdec_ref, wicl_ref, wgat_ref, dbias_ref, ibias_ref, cw_ref,
                   r_ref, k_ref, v_ref, al_ref, ld_ref, gg_ref, gyb_ref,
                   sh_out_ref, cv_out_ref, *scratch, decode, d, shift_cols, blk):
    tm = x_ref.shape[0]
    h = _rms(x_ref[...], nm_ref[...]).astype(BF16)
    conv0, gate0 = shift_cols, shift_cols + 3 * d

    if not decode:
        carry_sh, carry_cu = scratch

        @pl.when(pl.program_id(1) == 0)
        def _():
            carry_sh[0:1, :] = sh0_ref[0]
            carry_cu[0:1, :] = c0a_ref[0]
            carry_cu[1:2, :] = c0b_ref[0]

        row = lax.broadcasted_iota(jnp.int32, (tm, 1), 0)

    def proj(c0, width):
        return jnp.dot(h, w_in_ref[:, c0:c0 + width], preferred_element_type=F32)

    def shifted(c0, width):
        cols = slice(c0, c0 + width)
        z = proj(c0, width)
        if decode:
            z_prev = sh0_ref[:, cols]
            sh_out_ref[:, cols] = z
        else:
            z_prev = jnp.where(row == 0, carry_sh[0:1, cols], pltpu.roll(z, 1, 0))
            carry_sh[0:1, cols] = z[tm - 1:tm, :]
            sh_out_ref[0, :, cols] = z[tm - 1:tm, :]
        return z + mu_ref[:, cols] * (z_prev - z)

    tail = shifted(3 * d, shift_cols - 3 * d)
    tail_tanh = jnp.tanh(tail).astype(BF16)
    tail_sig = _sigmoid(tail).astype(BF16)
    tail_raw = tail.astype(BF16)
    for c in range(0, d, blk):
        cols = slice(c, c + blk)
        r_ref[:, cols] = shifted(c, blk)
        k_ref[:, cols] = shifted(d + c, blk)
        v_ref[:, cols] = shifted(2 * d + c, blk)
        dec = dbias_ref[:, cols] + jnp.dot(tail_tanh, wdec_ref[:, cols],
                                           preferred_element_type=F32)
        ld_ref[:, cols] = -jnp.exp(-_softplus(-dec) - 0.5)
        al_ref[:, cols] = _sigmoid(ibias_ref[:, cols] + jnp.dot(
            tail_raw, wicl_ref[:, cols], preferred_element_type=F32))
        gate = jnp.dot(tail_sig, wgat_ref[:, cols], preferred_element_type=F32)
        gg_ref[:, cols] = gate * _sigmoid(proj(gate0 + c, blk))
        cu = proj(conv0 + d + c, blk) * proj(conv0 + 2 * d + c, blk)
        if decode:
            cu_m2 = c0a_ref[:, cols]
            cu_m1 = c0b_ref[:, cols]
            cv_out_ref[:, cols] = cu
        else:
            cu_m1 = jnp.where(row == 0, carry_cu[1:2, cols], pltpu.roll(cu, 1, 0))
            cu_m2 = jnp.where(row == 0, carry_cu[0:1, cols],
                              jnp.where(row == 1, carry_cu[1:2, cols], pltpu.roll(cu, 2, 0)))
            carry_cu[0:2, cols] = cu[tm - 2:tm, :]
            cv_out_ref[0, :, cols] = cu[tm - 2:tm, :]
        conv = cu_m2 * cw_ref[0:1, cols] + cu_m1 * cw_ref[1:2, cols] + cu * cw_ref[2:3, cols]
        gyb_ref[:, cols] = _sigmoid(proj(gate0 + d + c, blk)) * (proj(conv0 + c, blk) * conv)


def _mix_in(x, nm, w_in, mu, sh0, c0a, c0b, wdec, wicl, wgat, dbias, ibias, cw,
            *, decode, batch, seq, tm):
    m, d = x.shape
    shift_cols = mu.shape[1]
    consts = [nm, w_in, mu]
    lora = [wdec, wicl, wgat, dbias, ibias, cw]
    if decode:
        grid = (1,)
        row = lambda w: pl.BlockSpec((m, w), lambda i: (0, 0))
        state_specs = [row(shift_cols), row(d), row(d)]
        state_out = [row(shift_cols), row(d)]
        state_shapes = [jax.ShapeDtypeStruct((m, shift_cols), F32),
                        jax.ShapeDtypeStruct((m, d), F32)]
        scratch = []
        sem = ("arbitrary",)
    else:
        nt = seq // tm
        grid = (batch, nt)
        row = lambda w: pl.BlockSpec((tm, w), lambda b, t: (b * nt + t, 0))
        per_b = lambda r, w: pl.BlockSpec((1, r, w), lambda b, t: (b, 0, 0))
        state_specs = [per_b(1, shift_cols), per_b(1, d), per_b(1, d)]
        state_out = [per_b(1, shift_cols), per_b(2, d)]
        state_shapes = [jax.ShapeDtypeStruct((batch, 1, shift_cols), F32),
                        jax.ShapeDtypeStruct((batch, 2, d), F32)]
        scratch = [pltpu.VMEM((8, shift_cols), F32), pltpu.VMEM((8, d), F32)]
        sem = ("arbitrary", "arbitrary")
    tok = jax.ShapeDtypeStruct((m, d), F32)
    return pl.pallas_call(
        functools.partial(_mix_in_kernel, decode=decode, d=d, shift_cols=shift_cols,
                          blk=MIX_COL_BLOCK),
        grid=grid,
        in_specs=([row(d)] + [_const_spec(a.shape) for a in consts] + state_specs
                  + [_const_spec(a.shape) for a in lora]),
        out_specs=[row(d)] * 7 + state_out,
        out_shape=[tok] * 7 + state_shapes,
        scratch_shapes=scratch,
        compiler_params=pltpu.CompilerParams(
            dimension_semantics=sem, vmem_limit_bytes=VMEM_LIMIT),
        name="mix_in_decode" if decode else "mix_in",
    )(x, *consts, sh0, c0a, c0b, *lora)


def _head_sum(x, first):
    x1 = jnp.where(first, x, 0.0)
    s1 = jnp.sum(x1, axis=-1, keepdims=True)
    s2 = jnp.sum(x - x1, axis=-1, keepdims=True)
    return jnp.where(first, s1, s2)


def _stack_heads(x):
    x = x.astype(BF16)
    lane = lax.broadcasted_iota(jnp.int32, x.shape, 1) % LANES
    first = lane < HEAD
    zero = jnp.zeros_like(x)
    return jnp.concatenate([jnp.where(first, x, zero), jnp.where(first, zero, x)], axis=0)


def _apply(mat, x):
    return jnp.dot(mat.astype(BF16), _stack_heads(x), preferred_element_type=F32)


def _split3(x):
    hi = x.astype(BF16)
    rest = x - hi.astype(F32)
    mid = rest.astype(BF16)
    return hi, mid, (rest - mid.astype(F32)).astype(BF16)


def _wkv_kernel(r_ref, k_ref, v_ref, al_ref, ld_ref, gg_ref, gyb_ref, kk_ref, ka_ref, rk_ref,
                gw_ref, gb_ref, y_ref, st_ref, h_ref, *, group, lag):
    c = pl.program_id(1)
    n_pairs = h_ref.shape[0]

    @pl.when(c == 0)
    def _():
        h_ref[...] = jnp.zeros_like(h_ref)

    C = CHUNK
    row = lax.broadcasted_iota(jnp.int32, (C, LANES), 0)
    col = lax.broadcasted_iota(jnp.int32, (C, LANES), 1)
    first = col < HEAD
    j = col % HEAD
    strict = j < row
    incl = j <= row
    cs_r = lax.broadcasted_iota(jnp.int32, (C, 3 * C), 0)
    cs_c = lax.broadcasted_iota(jnp.int32, (C, 3 * C), 1) % C
    tri3 = (cs_c <= cs_r).astype(BF16)
    sq_r = lax.broadcasted_iota(jnp.int32, (LANES, LANES), 0)
    sq_c = lax.broadcasted_iota(jnp.int32, (LANES, LANES), 1)
    same_head = (sq_r // HEAD) == (sq_c // HEAD)
    eye = sq_r == sq_c

    def prefix(q, p):
        rows = slice(q * C, (q + 1) * C)
        sl = slice(p * LANES, (p + 1) * LANES)
        ld = ld_ref[rows, sl]
        lp = jnp.dot(tri3, jnp.concatenate(_split3(ld), axis=0), preferred_element_type=F32)
        return dict(p=p, rows=rows, sl=sl, ld=ld, lp=lp)

    def scores(st):
        rows, sl, ld, lp = st["rows"], st["sl"], st["ld"], st["lp"]
        r = r_ref[rows, sl]
        k = k_ref[rows, sl]
        v = v_ref[rows, sl]
        al = al_ref[rows, sl]
        lp_end = lp[C - 1:C, :]
        kkr = k * kk_ref[:, sl]
        nrm = jnp.sqrt(_head_sum(kkr * kkr, first))
        kk = kkr / jnp.maximum(nrm, NORM_EPS)
        kmod = k * (1.0 + (al - 1.0) * ka_ref[:, sl])
        bv = kk * al
        e_neg = jnp.exp(-lp)
        e_end = jnp.exp(lp_end - lp)
        st.update(v=v, at=kk * jnp.exp(lp - ld), rt=r * jnp.exp(lp),
                  bh=bv * e_end, kh=kmod * e_end, p_end=jnp.exp(lp_end),
                  bonus=_head_sum(r * kmod * rk_ref[:, sl], first) * v)
        sc = lax.dot_general(
            jnp.concatenate([st["at"], st["rt"]], axis=0).astype(BF16),
            jnp.concatenate([_stack_heads(bv * e_neg), _stack_heads(kmod * e_neg)], axis=0),
            (((1,), (1,)), ((), ())), preferred_element_type=F32)
        st["lab"] = jnp.where(strict, sc[0:C, 0:LANES], 0.0)
        st["lak"] = jnp.where(strict, sc[0:C, LANES:2 * LANES], 0.0)
        st["mrb"] = jnp.where(incl, sc[C:2 * C, 0:LANES], 0.0)
        st["mrk"] = jnp.where(incl, sc[C:2 * C, LANES:2 * LANES], 0.0)
        st["inv"] = (jnp.where(j == row, 1.0, 0.0)
                     - jnp.where((j // 2) == (row // 2), st["lab"], 0.0))
        lab_b = st["lab"].astype(BF16)
        st["lab2"] = jnp.concatenate([lab_b, lab_b], axis=0)

    tok_r = sq_r % HEAD
    tok_c = sq_c % HEAD

    def double_left(st, s):
        keep = same_head & ((tok_r // (2 * s)) == (tok_c // (2 * s))) & ((tok_r // s) != (tok_c // s))
        st["inv_b"] = st["inv"].astype(BF16)
        st["inv_off"] = jnp.dot(st["inv_b"], jnp.where(keep, st["lab2"], jnp.zeros_like(st["lab2"])),
                                preferred_element_type=F32)

    def double_right(st):
        inv2 = jnp.concatenate([st["inv_b"], st["inv_b"]], axis=0)
        st["inv"] = st["inv"] - jnp.dot(st["inv_off"].astype(BF16),
                                        jnp.where(same_head, inv2, jnp.zeros_like(inv2)),
                                        preferred_element_type=F32)

    def values(st):
        st["xv"] = _apply(jnp.concatenate([st["lak"], st["mrk"]], axis=0), st["v"])
        st["lhs_t"] = jnp.concatenate([st["kh"], -st["bh"]], axis=0).T.astype(BF16)

    def solve(st):
        st["uw"] = _apply(st["inv"], jnp.concatenate([st["at"], st["xv"][0:C]], axis=1))

    def outputs(st):
        uw, v = st["uw"], st["v"]
        mu = _apply(st["mrb"], uw)
        st["q"] = (st["rt"] - mu[:, 0:LANES]).astype(BF16)
        st["y0"] = st["xv"][C:2 * C] - mu[:, LANES:2 * LANES]
        rhs = jnp.concatenate(
            [jnp.concatenate([v, jnp.zeros_like(v)], axis=1),
             jnp.concatenate([uw[:, LANES:2 * LANES], -uw[:, 0:LANES]], axis=1)], axis=0)
        gb = jnp.dot(st["lhs_t"], rhs.astype(BF16), preferred_element_type=F32)
        st["g_c"] = jnp.where(same_head, gb[:, 0:LANES], 0.0)
        st["a_c"] = jnp.where(same_head, jnp.where(eye, st["p_end"], 0.0)
                              - gb[:, LANES:2 * LANES], 0.0).astype(BF16)

    def advance(st):
        p = st["p"]
        qa = jnp.dot(jnp.concatenate([st["q"], st["a_c"]], axis=0), h_ref[p].astype(BF16),
                     preferred_element_type=F32)
        st["y"] = qa[0:C] + st["y0"]
        h_ref[p] = qa[C:C + LANES] + st["g_c"]

    def normalize(st):
        sl, y = st["sl"], st["y"]
        mean = _head_sum(y, first) * (1.0 / HEAD)
        yc = y - mean
        var = _head_sum(yc * yc, first) * (1.0 / HEAD)
        yn = yc * lax.rsqrt(var + GN_EPS) * gw_ref[:, sl] + gb_ref[:, sl]
        merged = (yn + st["bonus"]) * gg_ref[st["rows"], sl] + gyb_ref[st["rows"], sl]
        y_ref[st["rows"], sl] = merged.astype(BF16)

    def program(chunks):
        units = []

        def s_prefix():
            units.extend(prefix(q, p) for q in chunks for p in range(n_pairs))

        def each(fn, *args):
            return lambda: [fn(st, *args) for st in units]

        stages = [s_prefix, each(scores), each(values)]
        s = 2
        while s < C:
            stages += [each(double_left, s), each(double_right)]
            s *= 2
        return stages + [each(solve), each(outputs), each(advance), each(normalize)]

    n_chunks = r_ref.shape[0] // C
    programs = [program(range(g, g + group)) for g in range(0, n_chunks, group)]
    for t in range(len(programs[0]) + lag * (len(programs) - 1)):
        for g, stages in enumerate(programs):
            if 0 <= t - g * lag < len(stages):
                stages[t - g * lag]()

    @pl.when(c == pl.num_programs(1) - 1)
    def _():
        for p in range(n_pairs):
            s_pair = h_ref[p].T
            st_ref[0, 2 * p] = s_pair[0:HEAD, 0:HEAD]
            st_ref[0, 2 * p + 1] = pltpu.roll(s_pair, HEAD, 1)[HEAD:2 * HEAD, 0:HEAD]


def _wkv(r, k, v, al, ld, gg, gyb, kk, ka, rk, gw, gb, *, batch, seq):
    m, d = r.shape
    rows = WKV_CHUNKS_PER_STEP * CHUNK
    nc = seq // rows
    n_pairs = d // LANES
    tok = pl.BlockSpec((rows, d), lambda b, c: (b * nc + c, 0))
    par = pl.BlockSpec((1, d), lambda b, c: (0, 0))
    return pl.pallas_call(
        functools.partial(_wkv_kernel, group=WKV_CHUNKS_PER_GROUP, lag=WKV_STAGE_LAG),
        grid=(batch, nc),
        in_specs=[tok] * 7 + [par] * 5,
        out_specs=[tok, pl.BlockSpec((1, 2 * n_pairs, HEAD, HEAD), lambda b, c: (b, 0, 0, 0))],
        out_shape=[jax.ShapeDtypeStruct((m, d), BF16),
                   jax.ShapeDtypeStruct((batch, 2 * n_pairs, HEAD, HEAD), F32)],
        scratch_shapes=[pltpu.VMEM((n_pairs, LANES, LANES), F32)],
        compiler_params=pltpu.CompilerParams(
            dimension_semantics=("arbitrary", "arbitrary"), vmem_limit_bytes=VMEM_LIMIT),
        name="wkv",
    )(r, k, v, al, ld, gg, gyb, kk, ka, rk, gw, gb)


def _wkv_step_kernel(r_ref, k_ref, v_ref, al_ref, ld_ref, gg_ref, gyb_ref, kk_ref, ka_ref,
                     rk_ref, gw_ref, gb_ref, s_ref, y_ref, so_ref):
    hb, _, _, nb = s_ref.shape
    r_t, k_t, v_t, al_t, ld_t = [ref[...].T for ref in (r_ref, k_ref, v_ref, al_ref, ld_ref)]
    diag = (lax.broadcasted_iota(jnp.int32, (HEAD, HEAD, nb), 0)
            == lax.broadcasted_iota(jnp.int32, (HEAD, HEAD, nb), 1))
    ys = []
    for i in range(hb):
        rows = slice(i * HEAD, (i + 1) * HEAD)
        r, k, v, al = r_t[rows], k_t[rows], v_t[rows], al_t[rows]
        decay = jnp.exp(ld_t[rows])
        kkr = k * kk_ref[rows, :]
        nrm = jnp.sqrt(jnp.sum(kkr * kkr, axis=0, keepdims=True))
        kk = kkr / jnp.maximum(nrm, NORM_EPS)
        kmod = k * (1.0 + (al - 1.0) * ka_ref[rows, :])
        bv = kk * al
        s = s_ref[i]
        sa = jnp.sum(s * kk[None], axis=1, keepdims=True)
        v3 = jnp.sum(jnp.where(diag, v[None], 0.0), axis=1, keepdims=True)
        s_new = s * decay[None] - sa * bv[None] + v3 * kmod[None]
        so_ref[i] = s_new
        y3 = jnp.sum(s_new * r[None], axis=1, keepdims=True)
        y = jnp.sum(jnp.where(diag, y3, 0.0), axis=0)
        mean = jnp.mean(y, axis=0, keepdims=True)
        yc = y - mean
        var = jnp.mean(yc * yc, axis=0, keepdims=True)
        yn = yc * lax.rsqrt(var + GN_EPS) * gw_ref[rows, :] + gb_ref[rows, :]
        bonus = jnp.sum(r * kmod * rk_ref[rows, :], axis=0, keepdims=True) * v
        ys.append(yn + bonus)
    ya = jnp.concatenate(ys, axis=0).T
    y_ref[...] = (ya * gg_ref[...] + gyb_ref[...]).astype(BF16)


def _wkv_step(r, k, v, al, ld, gg, gyb, kk, ka, rk, gw, gb, s0, hb):
    nh, _, _, b = s0.shape
    d = r.shape[1]
    tok = pl.BlockSpec((b, hb * HEAD), lambda i: (0, i))
    par = pl.BlockSpec((hb * HEAD, 1), lambda i: (i, 0))
    st = pl.BlockSpec((hb, HEAD, HEAD, b), lambda i: (i, 0, 0, 0))
    return pl.pallas_call(
        _wkv_step_kernel,
        grid=(nh // hb,),
        in_specs=[tok] * 7 + [par] * 5 + [st],
        out_specs=[tok, st],
        out_shape=[jax.ShapeDtypeStruct((b, d), BF16), jax.ShapeDtypeStruct(s0.shape, F32)],
        compiler_params=pltpu.CompilerParams(
            dimension_semantics=("arbitrary",), vmem_limit_bytes=VMEM_LIMIT),
        name="wkv_step",
    )(r, k, v, al, ld, gg, gyb, kk, ka, rk, gw, gb, s0)


def _tail_kernel(x_ref, mg_ref, p_ref, wo_ref, nf_ref, wg_ref, wu_ref,
                 wd_ref, npl_ref, wpg_ref, wpp_ref, nfin_ref, o_ref, *, final):
    x = x_ref[...] + jnp.dot(mg_ref[...], wo_ref[...], preferred_element_type=F32)
    x = _swiglu_half(x, nf_ref[...], wg_ref, wu_ref, wd_ref)
    hp = _rms(x, npl_ref[...])
    x = x + _sigmoid(_bdot(hp, wpg_ref[...])) * _bdot(p_ref[...], wpp_ref[...])
    o_ref[...] = _rms(x, nfin_ref[...]) if final else x


def _tail(x, merged, p, wo, nf, wg, wu, wd, npl, wpg, wpp, nfin, *, final, tm):
    m, d = x.shape
    row = lambda w: pl.BlockSpec((tm, w), lambda i: (i, 0))
    weights = [wo, nf, wg, wu, wd, npl, wpg, wpp, nfin]
    return pl.pallas_call(
        functools.partial(_tail_kernel, final=final),
        grid=(m // tm,),
        in_specs=[row(d)] * 2 + [row(p.shape[1])] + [_const_spec(w.shape) for w in weights],
        out_specs=row(d),
        out_shape=jax.ShapeDtypeStruct((m, d), F32),
        compiler_params=pltpu.CompilerParams(
            dimension_semantics=("arbitrary",), vmem_limit_bytes=VMEM_LIMIT),
        name="tail",
    )(x, merged, p, *weights)


def _layer(x, p, s0, sh0, cv0, w, *, decode, batch, seq, final):
    d = x.shape[1]
    nh = d // HEAD
    x1 = _ffn(x, w["norm_ffn1"], w["ffn1_gate"], w["ffn1_up"], w["ffn1_down"],
              tm=min(512, x.shape[0]))
    if decode:
        sh_in, c0a, c0b = sh0, cv0[:, 0], cv0[:, 1]
    else:
        sh_in, c0a, c0b = sh0[:, None], cv0[:, 0:1], cv0[:, 1:2]
    r, k, v, al, ld, gg, gyb, sh_new, cv_new = _mix_in(
        x1, w["norm_mix"], w["w_in"], w["mu"], sh_in, c0a, c0b, w["wdec"], w["wicl"],
        w["wgat"], w["decay_bias"], w["iclr_bias"], w["conv_w"],
        decode=decode, batch=batch, seq=seq, tm=min(256, x.shape[0]))
    pars = [w["k_k"], w["k_a"], w["r_k"], w["gn_w"], w["gn_b"]]
    if decode:
        merged, s_new = _wkv_step(r, k, v, al, ld, gg, gyb, *[a.reshape(d, 1) for a in pars],
                                  jnp.transpose(s0, (1, 2, 3, 0)), hb=2)
        s_new = jnp.transpose(s_new, (3, 0, 1, 2))
        cv_new = jnp.stack([cv0[:, 1], cv_new], axis=1)
    else:
        merged, s_new = _wkv(r, k, v, al, ld, gg, gyb, *pars, batch=batch, seq=seq)
        sh_new = sh_new[:, 0]
    x_out = _tail(x1, merged, p, w["w_out"], w["norm_ffn2"], w["ffn2_gate"],
                  w["ffn2_up"], w["ffn2_down"], w["norm_ple"], w["ple_gate"], w["ple_proj"],
                  w["norm_final"], final=final, tm=min(512, x.shape[0]))
    return x_out, s_new, sh_new, cv_new


def kernel(x_prompt, x_sample, p_prompt, p_sample, state_wkv, state_shift, state_conv,
           norm_ffn1, w_ffn1_gate, w_ffn1_up, w_ffn1_down, norm_mix, w_in, mu_shift,
           w_decay_up, decay_bias, w_iclr_up, iclr_bias, w_gate_up, k_k, k_a, r_k, gn_w,
           gn_b, conv_w, w_out, norm_ffn2, w_ffn2_gate, w_ffn2_up, w_ffn2_down, norm_ple,
           w_ple_gate, w_ple_proj, norm_final):
    depth = w_in.shape[0]
    bp, tp, d = x_prompt.shape
    bs, ts, _ = x_sample.shape
    assert ts == 1 and tp % (WKV_CHUNKS_PER_STEP * CHUNK) == 0
    shift_cols = mu_shift.shape[1]
    n_dec, n_icl, n_gat = w_decay_up.shape[1], w_iclr_up.shape[1], w_gate_up.shape[1]
    assert 3 * d + n_dec + n_icl + n_gat == shift_cols

    xp = x_prompt.reshape(bp * tp, d)
    xs = x_sample.reshape(bs, d)
    vec = lambda a: a.reshape(1, -1)
    outs = [[] for _ in range(6)]
    for i in range(depth):
        n_tail = shift_cols - 3 * d
        pad = lambda a, lo: jnp.zeros((n_tail, d), BF16).at[lo:lo + a.shape[0]].set(a.astype(BF16))
        w = dict(
            norm_ffn1=vec(norm_ffn1[i]), ffn1_gate=w_ffn1_gate[i].astype(BF16),
            ffn1_up=w_ffn1_up[i].astype(BF16), ffn1_down=w_ffn1_down[i].astype(BF16),
            norm_mix=vec(norm_mix[i]), w_in=w_in[i].astype(BF16), mu=vec(mu_shift[i]),
            wdec=pad(w_decay_up[i], 0), wicl=pad(w_iclr_up[i], n_dec),
            wgat=pad(w_gate_up[i], n_dec + n_icl),
            decay_bias=vec(decay_bias[i]), iclr_bias=vec(iclr_bias[i]), conv_w=conv_w[i],
            k_k=vec(k_k[i]), k_a=vec(k_a[i]), r_k=vec(r_k[i]), gn_w=vec(gn_w[i]),
            gn_b=vec(gn_b[i]), w_out=w_out[i].astype(BF16), norm_ffn2=vec(norm_ffn2[i]),
            ffn2_gate=w_ffn2_gate[i].astype(BF16), ffn2_up=w_ffn2_up[i].astype(BF16),
            ffn2_down=w_ffn2_down[i].astype(BF16), norm_ple=vec(norm_ple[i]),
            ple_gate=w_ple_gate[i].astype(BF16), ple_proj=w_ple_proj[i].astype(BF16),
            norm_final=vec(norm_final))
        final = i == depth - 1
        xp, s_p, sh_p, cv_p = _layer(
            xp, p_prompt[i].reshape(bp * tp, -1), None,
            jnp.zeros((bp, shift_cols), F32), jnp.zeros((bp, 2, d), F32), w,
            decode=False, batch=bp, seq=tp, final=final)
        xs, s_s, sh_s, cv_s = _layer(
            xs, p_sample[i].reshape(bs, -1), state_wkv[i], state_shift[i], state_conv[i], w,
            decode=True, batch=bs, seq=1, final=final)
        for lst, val in zip(outs, (s_p, sh_p, cv_p, s_s, sh_s, cv_s)):
            lst.append(val)
    stk = [jnp.stack(o, 0) for o in outs]
    return (xp.reshape(bp, tp, d), xs.reshape(bs, ts, d), stk[0], stk[1], stk[2],
            stk[3], stk[4], stk[5])
```

```python
import functools

import jax
import jax.numpy as jnp
from jax import lax
from jax.experimental import pallas as pl
from jax.experimental.pallas import tpu as pltpu

F32 = jnp.float32
BF16 = jnp.bfloat16

HEAD = 64
LANES = 128
CHUNK = 64
MXU_DIM = 256
MIX_COL_BLOCK = MXU_DIM
WKV_CHUNKS_PER_STEP = 4
WKV_CHUNKS_PER_GROUP = 2
WKV_STAGE_LAG = 7
RMS_EPS = 1e-6
GN_EPS = 64e-5
NORM_EPS = 1e-12
VMEM_LIMIT = 56 * 1024 * 1024


def _rms(x, g):
    return x * lax.rsqrt(jnp.mean(x * x, axis=-1, keepdims=True) + RMS_EPS) * g


def _sigmoid(x):
    return 1.0 / (1.0 + jnp.exp(-x))


def _softplus(x):
    return jnp.maximum(x, 0.0) + jnp.log(1.0 + jnp.exp(-jnp.abs(x)))


def _bdot(a, b):
    return jnp.dot(a.astype(BF16), b, preferred_element_type=F32)


def _const_spec(shape):
    zeros = (0,) * len(shape)
    return pl.BlockSpec(shape, lambda *_: zeros, pipeline_mode=pl.Buffered(1))


def _row_parts(tm):
    sub = tm // 2 if tm >= 2 * MXU_DIM else tm
    return [slice(s, s + sub) for s in range(0, tm, sub)]


def _swiglu_half(xs, g, wg_ref, wu_ref, wd_ref):
    hs = [_rms(x, g).astype(BF16) for x in xs]
    gates = [jnp.dot(h, wg_ref[...], preferred_element_type=F32) for h in hs]
    ups = [jnp.dot(h, wu_ref[...], preferred_element_type=F32) for h in hs]
    acts = [(gate * _sigmoid(gate) * up).astype(BF16) for gate, up in zip(gates, ups)]
    return [x + 0.5 * jnp.dot(act, wd_ref[...], preferred_element_type=F32)
            for x, act in zip(xs, acts)]


def _ffn_kernel(x_ref, g_ref, wg_ref, wu_ref, wd_ref, o_ref):
    parts = _row_parts(x_ref.shape[0])
    outs = _swiglu_half([x_ref[rows, :] for rows in parts], g_ref[...], wg_ref, wu_ref, wd_ref)
    for rows, out in zip(parts, outs):
        o_ref[rows, :] = out


def _ffn(x, g, wg, wu, wd, tm):
    m, d = x.shape
    row = pl.BlockSpec((tm, d), lambda i: (i, 0))
    return pl.pallas_call(
        _ffn_kernel,
        grid=(m // tm,),
        in_specs=[row, _const_spec(g.shape), _const_spec(wg.shape),
                  _const_spec(wu.shape), _const_spec(wd.shape)],
        out_specs=row,
        out_shape=jax.ShapeDtypeStruct((m, d), F32),
        compiler_params=pltpu.CompilerParams(
            dimension_semantics=("arbitrary",), vmem_limit_bytes=VMEM_LIMIT),
        name="ffn",
    )(x, g, wg, wu, wd)


def _mix_in_kernel(x_ref, nm_ref, w_in_ref, mu_ref, sh0_ref, c0a_ref, c0b_ref,
                   wdec_ref, wicl_ref, wgat_ref, dbias_ref, ibias_ref, cw_ref,
                   r_ref, k_ref, v_ref, al_ref, ld_ref, gg_ref, gyb_ref,
                   sh_out_ref, cv_out_ref, *scratch, decode, d, shift_cols, blk):
    tm = x_ref.shape[0]
    h = _rms(x_ref[...], nm_ref[...]).astype(BF16)
    conv0, gate0 = shift_cols, shift_cols + 3 * d

    if not decode:
        carry_sh, carry_cu = scratch

        @pl.when(pl.program_id(1) == 0)
        def _():
            carry_sh[0:1, :] = sh0_ref[0]
            carry_cu[0:1, :] = c0a_ref[0]
            carry_cu[1:2, :] = c0b_ref[0]

        row = lax.broadcasted_iota(jnp.int32, (tm, 1), 0)

    def proj(c0, width):
        return jnp.dot(h, w_in_ref[:, c0:c0 + width], preferred_element_type=F32)

    def shifted(c0, width):
        cols = slice(c0, c0 + width)
        z = proj(c0, width)
        if decode:
            z_prev = sh0_ref[:, cols]
            sh_out_ref[:, cols] = z
        else:
            z_prev = jnp.where(row == 0, carry_sh[0:1, cols], pltpu.roll(z, 1, 0))
            carry_sh[0:1, cols] = z[tm - 1:tm, :]
            sh_out_ref[0, :, cols] = z[tm - 1:tm, :]
        return z + mu_ref[:, cols] * (z_prev - z)

    tail = shifted(3 * d, shift_cols - 3 * d)
    tail_tanh = jnp.tanh(tail).astype(BF16)
    tail_sig = _sigmoid(tail).astype(BF16)
    tail_raw = tail.astype(BF16)
    for c in range(0, d, blk):
        cols = slice(c, c + blk)
        r_ref[:, cols] = shifted(c, blk)
        k_ref[:, cols] = shifted(d + c, blk)
        v_ref[:, cols] = shifted(2 * d + c, blk)
        dec = dbias_ref[:, cols] + jnp.dot(tail_tanh, wdec_ref[:, cols],
                                           preferred_element_type=F32)
        ld_ref[:, cols] = -jnp.exp(-_softplus(-dec) - 0.5)
        al_ref[:, cols] = _sigmoid(ibias_ref[:, cols] + jnp.dot(
            tail_raw, wicl_ref[:, cols], preferred_element_type=F32))
        gate = jnp.dot(tail_sig, wgat_ref[:, cols], preferred_element_type=F32)
        gg_ref[:, cols] = gate * _sigmoid(proj(gate0 + c, blk))
        cu = proj(conv0 + d + c, blk) * proj(conv0 + 2 * d + c, blk)
        if decode:
            cu_m2 = c0a_ref[:, cols]
            cu_m1 = c0b_ref[:, cols]
            cv_out_ref[:, cols] = cu
        else:
            cu_m1 = jnp.where(row == 0, carry_cu[1:2, cols], pltpu.roll(cu, 1, 0))
            cu_m2 = jnp.where(row == 0, carry_cu[0:1, cols],
                              jnp.where(row == 1, carry_cu[1:2, cols], pltpu.roll(cu, 2, 0)))
            carry_cu[0:2, cols] = cu[tm - 2:tm, :]
            cv_out_ref[0, :, cols] = cu[tm - 2:tm, :]
        conv = cu_m2 * cw_ref[0:1, cols] + cu_m1 * cw_ref[1:2, cols] + cu * cw_ref[2:3, cols]
        gyb_ref[:, cols] = _sigmoid(proj(gate0 + d + c, blk)) * (proj(conv0 + c, blk) * conv)


def _mix_in(x, nm, w_in, mu, sh0, c0a, c0b, wdec, wicl, wgat, dbias, ibias, cw,
            *, decode, batch, seq, tm):
    m, d = x.shape
    shift_cols = mu.shape[1]
    consts = [nm, w_in, mu]
    lora = [wdec, wicl, wgat, dbias, ibias, cw]
    if decode:
        grid = (1,)
        row = lambda w: pl.BlockSpec((m, w), lambda i: (0, 0))
        state_specs = [row(shift_cols), row(d), row(d)]
        state_out = [row(shift_cols), row(d)]
        state_shapes = [jax.ShapeDtypeStruct((m, shift_cols), F32),
                        jax.ShapeDtypeStruct((m, d), F32)]
        scratch = []
        sem = ("arbitrary",)
    else:
        nt = seq // tm
        grid = (batch, nt)
        row = lambda w: pl.BlockSpec((tm, w), lambda b, t: (b * nt + t, 0))
        per_b = lambda r, w: pl.BlockSpec((1, r, w), lambda b, t: (b, 0, 0))
        state_specs = [per_b(1, shift_cols), per_b(1, d), per_b(1, d)]
        state_out = [per_b(1, shift_cols), per_b(2, d)]
        state_shapes = [jax.ShapeDtypeStruct((batch, 1, shift_cols), F32),
                        jax.ShapeDtypeStruct((batch, 2, d), F32)]
        scratch = [pltpu.VMEM((8, shift_cols), F32), pltpu.VMEM((8, d), F32)]
        sem = ("arbitrary", "arbitrary")
    tok = jax.ShapeDtypeStruct((m, d), F32)
    return pl.pallas_call(
        functools.partial(_mix_in_kernel, decode=decode, d=d, shift_cols=shift_cols,
                          blk=MIX_COL_BLOCK),
        grid=grid,
        in_specs=([row(d)] + [_const_spec(a.shape) for a in consts] + state_specs
                  + [_const_spec(a.shape) for a in lora]),
        out_specs=[row(d)] * 7 + state_out,
        out_shape=[tok] * 7 + state_shapes,
        scratch_shapes=scratch,
        compiler_params=pltpu.CompilerParams(
            dimension_semantics=sem, vmem_limit_bytes=VMEM_LIMIT),
        name="mix_in_decode" if decode else "mix_in",
    )(x, *consts, sh0, c0a, c0b, *lora)


def _head_sum(x, first):
    x1 = jnp.where(first, x, 0.0)
    s1 = jnp.sum(x1, axis=-1, keepdims=True)
    s2 = jnp.sum(x - x1, axis=-1, keepdims=True)
    return jnp.where(first, s1, s2)


def _stack_heads(x):
    x = x.astype(BF16)
    lane = lax.broadcasted_iota(jnp.int32, x.shape, 1) % LANES
    first = lane < HEAD
    zero = jnp.zeros_like(x)
    return jnp.concatenate([jnp.where(first, x, zero), jnp.where(first, zero, x)], axis=0)


def _apply(mat, x):
    return jnp.dot(mat.astype(BF16), _stack_heads(x), preferred_element_type=F32)


def _split3(x):
    hi = x.astype(BF16)
    rest = x - hi.astype(F32)
    mid = rest.astype(BF16)
    return hi, mid, (rest - mid.astype(F32)).astype(BF16)


def _wkv_kernel(r_ref, k_ref, v_ref, al_ref, ld_ref, gg_ref, gyb_ref, kk_ref, ka_ref, rk_ref,
                gw_ref, gb_ref, y_ref, st_ref, h_ref, *, group, lag):
    c = pl.program_id(1)
    n_pairs = h_ref.shape[0]

    @pl.when(c == 0)
    def _():
        h_ref[...] = jnp.zeros_like(h_ref)

    C = CHUNK
    row = lax.broadcasted_iota(jnp.int32, (C, LANES), 0)
    col = lax.broadcasted_iota(jnp.int32, (C, LANES), 1)
    first = col < HEAD
    j = col % HEAD
    strict = j < row
    incl = j <= row
    cs_r = lax.broadcasted_iota(jnp.int32, (C, 3 * C), 0)
    cs_c = lax.broadcasted_iota(jnp.int32, (C, 3 * C), 1) % C
    tri3 = (cs_c <= cs_r).astype(BF16)
    sq_r = lax.broadcasted_iota(jnp.int32, (LANES, LANES), 0)
    sq_c = lax.broadcasted_iota(jnp.int32, (LANES, LANES), 1)
    same_head = (sq_r // HEAD) == (sq_c // HEAD)
    eye = sq_r == sq_c

    def prefix(q, p):
        rows = slice(q * C, (q + 1) * C)
        sl = slice(p * LANES, (p + 1) * LANES)
        ld = ld_ref[rows, sl]
        lp = jnp.dot(tri3, jnp.concatenate(_split3(ld), axis=0), preferred_element_type=F32)
        return dict(p=p, rows=rows, sl=sl, ld=ld, lp=lp)

    def scores(st):
        rows, sl, ld, lp = st["rows"], st["sl"], st["ld"], st["lp"]
        r = r_ref[rows, sl]
        k = k_ref[rows, sl]
        v = v_ref[rows, sl]
        al = al_ref[rows, sl]
        lp_end = lp[C - 1:C, :]
        kkr = k * kk_ref[:, sl]
        nrm = jnp.sqrt(_head_sum(kkr * kkr, first))
        kk = kkr / jnp.maximum(nrm, NORM_EPS)
        kmod = k * (1.0 + (al - 1.0) * ka_ref[:, sl])
        bv = kk * al
        e_neg = jnp.exp(-lp)
        e_end = jnp.exp(lp_end - lp)
        st.update(v=v, at=kk * jnp.exp(lp - ld), rt=r * jnp.exp(lp),
                  bh=bv * e_end, kh=kmod * e_end, p_end=jnp.exp(lp_end),
                  bonus=_head_sum(r * kmod * rk_ref[:, sl], first) * v)
        sc = lax.dot_general(
            jnp.concatenate([st["at"], st["rt"]], axis=0).astype(BF16),
            jnp.concatenate([_stack_heads(bv * e_neg), _stack_heads(kmod * e_neg)], axis=0),
            (((1,), (1,)), ((), ())), preferred_element_type=F32)
        st["lab"] = jnp.where(strict, sc[0:C, 0:LANES], 0.0)
        st["lak"] = jnp.where(strict, sc[0:C, LANES:2 * LANES], 0.0)
        st["mrb"] = jnp.where(incl, sc[C:2 * C, 0:LANES], 0.0)
        st["mrk"] = jnp.where(incl, sc[C:2 * C, LANES:2 * LANES], 0.0)
        st["inv"] = (jnp.where(j == row, 1.0, 0.0)
                     - jnp.where((j // 2) == (row // 2), st["lab"], 0.0))
        lab_b = st["lab"].astype(BF16)
        st["lab2"] = jnp.concatenate([lab_b, lab_b], axis=0)

    tok_r = sq_r % HEAD
    tok_c = sq_c % HEAD

    def double_left(st, s):
        keep = same_head & ((tok_r // (2 * s)) == (tok_c // (2 * s))) & ((tok_r // s) != (tok_c // s))
        st["inv_b"] = st["inv"].astype(BF16)
        st["inv_off"] = jnp.dot(st["inv_b"], jnp.where(keep, st["lab2"], jnp.zeros_like(st["lab2"])),
                                preferred_element_type=F32)

    def double_right(st):
        inv2 = jnp.concatenate([st["inv_b"], st["inv_b"]], axis=0)
        st["inv"] = st["inv"] - jnp.dot(st["inv_off"].astype(BF16),
                                        jnp.where(same_head, inv2, jnp.zeros_like(inv2)),
                                        preferred_element_type=F32)

    def values(st):
        st["xv"] = _apply(jnp.concatenate([st["lak"], st["mrk"]], axis=0), st["v"])
        st["lhs_t"] = jnp.concatenate([st["kh"], -st["bh"]], axis=0).T.astype(BF16)

    def solve(st):
        st["uw"] = _apply(st["inv"], jnp.concatenate([st["at"], st["xv"][0:C]], axis=1))

    def outputs(st):
        uw, v = st["uw"], st["v"]
        mu = _apply(st["mrb"], uw)
        st["q"] = (st["rt"] - mu[:, 0:LANES]).astype(BF16)
        st["y0"] = st["xv"][C:2 * C] - mu[:, LANES:2 * LANES]
        rhs = jnp.concatenate(
            [jnp.concatenate([v, jnp.zeros_like(v)], axis=1),
             jnp.concatenate([uw[:, LANES:2 * LANES], -uw[:, 0:LANES]], axis=1)], axis=0)
        gb = jnp.dot(st["lhs_t"], rhs.astype(BF16), preferred_element_type=F32)
        st["g_c"] = jnp.where(same_head, gb[:, 0:LANES], 0.0)
        st["a_c"] = jnp.where(same_head, jnp.where(eye, st["p_end"], 0.0)
                              - gb[:, LANES:2 * LANES], 0.0).astype(BF16)

    def advance(st):
        p = st["p"]
        qa = jnp.dot(jnp.concatenate([st["q"], st["a_c"]], axis=0), h_ref[p].astype(BF16),
                     preferred_element_type=F32)
        st["y"] = qa[0:C] + st["y0"]
        h_ref[p] = qa[C:C + LANES] + st["g_c"]

    def normalize(st):
        sl, y = st["sl"], st["y"]
        mean = _head_sum(y, first) * (1.0 / HEAD)
        yc = y - mean
        var = _head_sum(yc * yc, first) * (1.0 / HEAD)
        yn = yc * lax.rsqrt(var + GN_EPS) * gw_ref[:, sl] + gb_ref[:, sl]
        merged = (yn + st["bonus"]) * gg_ref[st["rows"], sl] + gyb_ref[st["rows"], sl]
        y_ref[st["rows"], sl] = merged.astype(BF16)

    def program(chunks):
        units = []

        def s_prefix():
            units.extend(prefix(q, p) for q in chunks for p in range(n_pairs))

        def each(fn, *args):
            return lambda: [fn(st, *args) for st in units]

        stages = [s_prefix, each(scores), each(values)]
        s = 2
        while s < C:
            stages += [each(double_left, s), each(double_right)]
            s *= 2
        return stages + [each(solve), each(outputs), each(advance), each(normalize)]

    n_chunks = r_ref.shape[0] // C
    programs = [program(range(g, g + group)) for g in range(0, n_chunks, group)]
    for t in range(len(programs[0]) + lag * (len(programs) - 1)):
        for g, stages in enumerate(programs):
            if 0 <= t - g * lag < len(stages):
                stages[t - g * lag]()

    @pl.when(c == pl.num_programs(1) - 1)
    def _():
        for p in range(n_pairs):
            s_pair = h_ref[p].T
            st_ref[0, 2 * p] = s_pair[0:HEAD, 0:HEAD]
            st_ref[0, 2 * p + 1] = pltpu.roll(s_pair, HEAD, 1)[HEAD:2 * HEAD, 0:HEAD]


def _wkv(r, k, v, al, ld, gg, gyb, kk, ka, rk, gw, gb, *, batch, seq):
    m, d = r.shape
    rows = WKV_CHUNKS_PER_STEP * CHUNK
    nc = seq // rows
    n_pairs = d // LANES
    tok = pl.BlockSpec((rows, d), lambda b, c: (b * nc + c, 0))
    par = pl.BlockSpec((1, d), lambda b, c: (0, 0))
    return pl.pallas_call(
        functools.partial(_wkv_kernel, group=WKV_CHUNKS_PER_GROUP, lag=WKV_STAGE_LAG),
        grid=(batch, nc),
        in_specs=[tok] * 7 + [par] * 5,
        out_specs=[tok, pl.BlockSpec((1, 2 * n_pairs, HEAD, HEAD), lambda b, c: (b, 0, 0, 0))],
        out_shape=[jax.ShapeDtypeStruct((m, d), BF16),
                   jax.ShapeDtypeStruct((batch, 2 * n_pairs, HEAD, HEAD), F32)],
        scratch_shapes=[pltpu.VMEM((n_pairs, LANES, LANES), F32)],
        compiler_params=pltpu.CompilerParams(
            dimension_semantics=("arbitrary", "arbitrary"), vmem_limit_bytes=VMEM_LIMIT),
        name="wkv",
    )(r, k, v, al, ld, gg, gyb, kk, ka, rk, gw, gb)


def _wkv_step_kernel(r_ref, k_ref, v_ref, al_ref, ld_ref, gg_ref, gyb_ref, kk_ref, ka_ref,
                     rk_ref, gw_ref, gb_ref, s_ref, y_ref, so_ref):
    hb, _, _, nb = s_ref.shape
    r_t, k_t, v_t, al_t, ld_t = [ref[...].T for ref in (r_ref, k_ref, v_ref, al_ref, ld_ref)]
    diag = (lax.broadcasted_iota(jnp.int32, (HEAD, HEAD, nb), 0)
            == lax.broadcasted_iota(jnp.int32, (HEAD, HEAD, nb), 1))
    ys = []
    for i in range(hb):
        rows = slice(i * HEAD, (i + 1) * HEAD)
        r, k, v, al = r_t[rows], k_t[rows], v_t[rows], al_t[rows]
        decay = jnp.exp(ld_t[rows])
        kkr = k * kk_ref[rows, :]
        nrm = jnp.sqrt(jnp.sum(kkr * kkr, axis=0, keepdims=True))
        kk = kkr / jnp.maximum(nrm, NORM_EPS)
        kmod = k * (1.0 + (al - 1.0) * ka_ref[rows, :])
        bv = kk * al
        s = s_ref[i]
        sa = jnp.sum(s * kk[None], axis=1, keepdims=True)
        v3 = jnp.sum(jnp.where(diag, v[None], 0.0), axis=1, keepdims=True)
        s_new = s * decay[None] - sa * bv[None] + v3 * kmod[None]
        so_ref[i] = s_new
        y3 = jnp.sum(s_new * r[None], axis=1, keepdims=True)
        y = jnp.sum(jnp.where(diag, y3, 0.0), axis=0)
        mean = jnp.mean(y, axis=0, keepdims=True)
        yc = y - mean
        var = jnp.mean(yc * yc, axis=0, keepdims=True)
        yn = yc * lax.rsqrt(var + GN_EPS) * gw_ref[rows, :] + gb_ref[rows, :]
        bonus = jnp.sum(r * kmod * rk_ref[rows, :], axis=0, keepdims=True) * v
        ys.append(yn + bonus)
    ya = jnp.concatenate(ys, axis=0).T
    y_ref[...] = (ya * gg_ref[...] + gyb_ref[...]).astype(BF16)


def _wkv_step(r, k, v, al, ld, gg, gyb, kk, ka, rk, gw, gb, s0, hb):
    nh, _, _, b = s0.shape
    d = r.shape[1]
    tok = pl.BlockSpec((b, hb * HEAD), lambda i: (0, i))
    par = pl.BlockSpec((hb * HEAD, 1), lambda i: (i, 0))
    st = pl.BlockSpec((hb, HEAD, HEAD, b), lambda i: (i, 0, 0, 0))
    return pl.pallas_call(
        _wkv_step_kernel,
        grid=(nh // hb,),
        in_specs=[tok] * 7 + [par] * 5 + [st],
        out_specs=[tok, st],
        out_shape=[jax.ShapeDtypeStruct((b, d), BF16), jax.ShapeDtypeStruct(s0.shape, F32)],
        compiler_params=pltpu.CompilerParams(
            dimension_semantics=("arbitrary",), vmem_limit_bytes=VMEM_LIMIT),
        name="wkv_step",
    )(r, k, v, al, ld, gg, gyb, kk, ka, rk, gw, gb, s0)


def _tail_kernel(x_ref, mg_ref, p_ref, wo_ref, nf_ref, wg_ref, wu_ref,
                 wd_ref, npl_ref, wpg_ref, wpp_ref, nfin_ref, o_ref, *, final):
    parts = _row_parts(x_ref.shape[0])
    xs = [x_ref[rows, :] + jnp.dot(mg_ref[rows, :], wo_ref[...], preferred_element_type=F32)
          for rows in parts]
    xs = _swiglu_half(xs, nf_ref[...], wg_ref, wu_ref, wd_ref)
    hps = [_rms(x, npl_ref[...]).astype(BF16) for x in xs]
    ple_gates = [jnp.dot(hp, wpg_ref[...], preferred_element_type=F32) for hp in hps]
    ples = [_bdot(p_ref[rows, :], wpp_ref[...]) for rows in parts]
    for rows, x, gate, ple in zip(parts, xs, ple_gates, ples):
        x = x + _sigmoid(gate) * ple
        o_ref[rows, :] = _rms(x, nfin_ref[...]) if final else x


def _tail(x, merged, p, wo, nf, wg, wu, wd, npl, wpg, wpp, nfin, *, final, tm):
    m, d = x.shape
    row = lambda w: pl.BlockSpec((tm, w), lambda i: (i, 0))
    weights = [wo, nf, wg, wu, wd, npl, wpg, wpp, nfin]
    return pl.pallas_call(
        functools.partial(_tail_kernel, final=final),
        grid=(m // tm,),
        in_specs=[row(d)] * 2 + [row(p.shape[1])] + [_const_spec(w.shape) for w in weights],
        out_specs=row(d),
        out_shape=jax.ShapeDtypeStruct((m, d), F32),
        compiler_params=pltpu.CompilerParams(
            dimension_semantics=("arbitrary",), vmem_limit_bytes=VMEM_LIMIT),
        name="tail",
    )(x, merged, p, *weights)


def _layer(x, p, s0, sh0, cv0, w, *, decode, batch, seq, final):
    d = x.shape[1]
    nh = d // HEAD
    x1 = _ffn(x, w["norm_ffn1"], w["ffn1_gate"], w["ffn1_up"], w["ffn1_down"],
              tm=min(512, x.shape[0]))
    if decode:
        sh_in, c0a, c0b = sh0, cv0[:, 0], cv0[:, 1]
    else:
        sh_in, c0a, c0b = sh0[:, None], cv0[:, 0:1], cv0[:, 1:2]
    r, k, v, al, ld, gg, gyb, sh_new, cv_new = _mix_in(
        x1, w["norm_mix"], w["w_in"], w["mu"], sh_in, c0a, c0b, w["wdec"], w["wicl"],
        w["wgat"], w["decay_bias"], w["iclr_bias"], w["conv_w"],
        decode=decode, batch=batch, seq=seq, tm=min(256, x.shape[0]))
    pars = [w["k_k"], w["k_a"], w["r_k"], w["gn_w"], w["gn_b"]]
    if decode:
        merged, s_new = _wkv_step(r, k, v, al, ld, gg, gyb, *[a.reshape(d, 1) for a in pars],
                                  jnp.transpose(s0, (1, 2, 3, 0)), hb=2)
        s_new = jnp.transpose(s_new, (3, 0, 1, 2))
        cv_new = jnp.stack([cv0[:, 1], cv_new], axis=1)
    else:
        merged, s_new = _wkv(r, k, v, al, ld, gg, gyb, *pars, batch=batch, seq=seq)
        sh_new = sh_new[:, 0]
    x_out = _tail(x1, merged, p, w["w_out"], w["norm_ffn2"], w["ffn2_gate"],
                  w["ffn2_up"], w["ffn2_down"], w["norm_ple"], w["ple_gate"], w["ple_proj"],
                  w["norm_final"], final=final, tm=min(512, x.shape[0]))
    return x_out, s_new, sh_new, cv_new


def kernel(x_prompt, x_sample, p_prompt, p_sample, state_wkv, state_shift, state_conv,
           norm_ffn1, w_ffn1_gate, w_ffn1_up, w_ffn1_down, norm_mix, w_in, mu_shift,
           w_decay_up, decay_bias, w_iclr_up, iclr_bias, w_gate_up, k_k, k_a, r_k, gn_w,
           gn_b, conv_w, w_out, norm_ffn2, w_ffn2_gate, w_ffn2_up, w_ffn2_down, norm_ple,
           w_ple_gate, w_ple_proj, norm_final):
    depth = w_in.shape[0]
    bp, tp, d = x_prompt.shape
    bs, ts, _ = x_sample.shape
    assert ts == 1 and tp % (WKV_CHUNKS_PER_STEP * CHUNK) == 0
    shift_cols = mu_shift.shape[1]
    n_dec, n_icl, n_gat = w_decay_up.shape[1], w_iclr_up.shape[1], w_gate_up.shape[1]
    assert 3 * d + n_dec + n_icl + n_gat == shift_cols

    xp = x_prompt.reshape(bp * tp, d)
    xs = x_sample.reshape(bs, d)
    vec = lambda a: a.reshape(1, -1)
    outs = [[] for _ in range(6)]
    for i in range(depth):
        n_tail = shift_cols - 3 * d
        pad = lambda a, lo: jnp.zeros((n_tail, d), BF16).at[lo:lo + a.shape[0]].set(a.astype(BF16))
        w = dict(
            norm_ffn1=vec(norm_ffn1[i]), ffn1_gate=w_ffn1_gate[i].astype(BF16),
            ffn1_up=w_ffn1_up[i].astype(BF16), ffn1_down=w_ffn1_down[i].astype(BF16),
            norm_mix=vec(norm_mix[i]), w_in=w_in[i].astype(BF16), mu=vec(mu_shift[i]),
            wdec=pad(w_decay_up[i], 0), wicl=pad(w_iclr_up[i], n_dec),
            wgat=pad(w_gate_up[i], n_dec + n_icl),
            decay_bias=vec(decay_bias[i]), iclr_bias=vec(iclr_bias[i]), conv_w=conv_w[i],
            k_k=vec(k_k[i]), k_a=vec(k_a[i]), r_k=vec(r_k[i]), gn_w=vec(gn_w[i]),
            gn_b=vec(gn_b[i]), w_out=w_out[i].astype(BF16), norm_ffn2=vec(norm_ffn2[i]),
            ffn2_gate=w_ffn2_gate[i].astype(BF16), ffn2_up=w_ffn2_up[i].astype(BF16),
            ffn2_down=w_ffn2_down[i].astype(BF16), norm_ple=vec(norm_ple[i]),
            ple_gate=w_ple_gate[i].astype(BF16), ple_proj=w_ple_proj[i].astype(BF16),
            norm_final=vec(norm_final))
        final = i == depth - 1
        xp, s_p, sh_p, cv_p = _layer(
            xp, p_prompt[i].reshape(bp * tp, -1), None,
            jnp.zeros((bp, shift_cols), F32), jnp.zeros((bp, 2, d), F32), w,
            decode=False, batch=bp, seq=tp, final=final)
        xs, s_s, sh_s, cv_s = _layer(
            xs, p_sample[i].reshape(bs, -1), state_wkv[i], state_shift[i], state_conv[i], w,
            decode=True, batch=bs, seq=1, final=final)
        for lst, val in zip(outs, (s_p, sh_p, cv_p, s_s, sh_s, cv_s)):
            lst.append(val)
    stk = [jnp.stack(o, 0) for o in outs]
    return (xp.reshape(bp, tp, d), xs.reshape(bs, ts, d), stk[0], stk[1], stk[2],
            stk[3], stk[4], stk[5])
```

```python
import functools

import jax
import jax.numpy as jnp
from jax import lax
from jax.experimental import pallas as pl
from jax.experimental.pallas import tpu as pltpu

F32 = jnp.float32
BF16 = jnp.bfloat16

HEAD = 64
LANES = 128
CHUNK = 64
MXU_DIM = 256
MIX_COL_BLOCK = MXU_DIM
WKV_CHUNKS_PER_STEP = 4
WKV_CHUNKS_PER_GROUP = 2
WKV_STAGE_LAG = 7
RMS_EPS = 1e-6
GN_EPS = 64e-5
NORM_EPS = 1e-12
VMEM_LIMIT = 56 * 1024 * 1024


def _rms(x, g):
    return x * lax.rsqrt(jnp.mean(x * x, axis=-1, keepdims=True) + RMS_EPS) * g


def _sigmoid(x):
    return 1.0 / (1.0 + jnp.exp(-x))


def _softplus(x):
    return jnp.maximum(x, 0.0) + jnp.log(1.0 + jnp.exp(-jnp.abs(x)))


def _bdot(a, b):
    return jnp.dot(a.astype(BF16), b, preferred_element_type=F32)


def _const_spec(shape):
    zeros = (0,) * len(shape)
    return pl.BlockSpec(shape, lambda *_: zeros, pipeline_mode=pl.Buffered(1))


def _cast_plan(arrays, n_steps, step_of):
    specs, shapes = [], []
    for a in arrays:
        rows, cols = a.shape
        blk = next(b for b in range(16, rows + 1, 16) if rows % b == 0 and rows // b <= n_steps)
        last = rows // blk - 1
        specs.append(pl.BlockSpec(
            (blk, cols), lambda *ids, last=last: (jnp.minimum(step_of(*ids), last), 0)))
        shapes.append(jax.ShapeDtypeStruct(a.shape, BF16))
    return specs, shapes


def _cast_blocks(srcs, dsts):
    for src, dst in zip(srcs, dsts):
        dst[...] = src[...].astype(BF16)


def _row_parts(tm):
    sub = tm // 2 if tm >= 2 * MXU_DIM else tm
    return [slice(s, s + sub) for s in range(0, tm, sub)]


def _swiglu_half(xs, g, wg_ref, wu_ref, wd_ref):
    hs = [_rms(x, g).astype(BF16) for x in xs]
    gates = [jnp.dot(h, wg_ref[...], preferred_element_type=F32) for h in hs]
    ups = [jnp.dot(h, wu_ref[...], preferred_element_type=F32) for h in hs]
    acts = [(gate * _sigmoid(gate) * up).astype(BF16) for gate, up in zip(gates, ups)]
    return [x + 0.5 * jnp.dot(act, wd_ref[...], preferred_element_type=F32)
            for x, act in zip(xs, acts)]


def _ffn_kernel(x_ref, g_ref, wg_ref, wu_ref, wd_ref, *rest, n_cast):
    cast_srcs, o_ref, cast_dsts = rest[:n_cast], rest[n_cast], rest[n_cast + 1:]
    _cast_blocks(cast_srcs, cast_dsts)
    parts = _row_parts(x_ref.shape[0])
    outs = _swiglu_half([x_ref[rows, :] for rows in parts], g_ref[...], wg_ref, wu_ref, wd_ref)
    for rows, out in zip(parts, outs):
        o_ref[rows, :] = out


def _ffn(x, g, wg, wu, wd, tm, cast=()):
    m, d = x.shape
    row = pl.BlockSpec((tm, d), lambda i: (i, 0))
    cast_specs, cast_shapes = _cast_plan(cast, m // tm, lambda i: i)
    out = pl.pallas_call(
        functools.partial(_ffn_kernel, n_cast=len(cast)),
        grid=(m // tm,),
        in_specs=[row, _const_spec(g.shape), _const_spec(wg.shape),
                  _const_spec(wu.shape), _const_spec(wd.shape)] + cast_specs,
        out_specs=[row] + cast_specs,
        out_shape=[jax.ShapeDtypeStruct((m, d), F32)] + cast_shapes,
        compiler_params=pltpu.CompilerParams(
            dimension_semantics=("arbitrary",), vmem_limit_bytes=VMEM_LIMIT),
        name="ffn",
    )(x, g, wg, wu, wd, *cast)
    return out[0], out[1:]


def _mix_in_kernel(*refs, decode, d, shift_cols, blk, n_cast):
    n = n_cast
    (x_ref, nm_ref, w_in_ref, mu_ref, sh0_ref, c0a_ref, c0b_ref,
     wdec_ref, wicl_ref, wgat_ref, dbias_ref, ibias_ref, cw_ref) = refs[:13]
    cast_srcs = refs[13:13 + n]
    (r_ref, k_ref, v_ref, al_ref, ld_ref, gg_ref, gyb_ref,
     sh_out_ref, cv_out_ref) = refs[13 + n:22 + n]
    cast_dsts = refs[22 + n:22 + 2 * n]
    scratch = refs[22 + 2 * n:]
    tm = x_ref.shape[0]
    h = _rms(x_ref[...], nm_ref[...]).astype(BF16)
    conv0, gate0 = shift_cols, shift_cols + 3 * d

    if not decode:
        carry_sh, carry_cu = scratch

        @pl.when(pl.program_id(1) == 0)
        def _():
            carry_sh[0:1, :] = sh0_ref[0]
            carry_cu[0:1, :] = c0a_ref[0]
            carry_cu[1:2, :] = c0b_ref[0]

        row = lax.broadcasted_iota(jnp.int32, (tm, 1), 0)

    def proj(c0, width):
        return jnp.dot(h, w_in_ref[:, c0:c0 + width], preferred_element_type=F32)

    def shifted(c0, width):
        cols = slice(c0, c0 + width)
        z = proj(c0, width)
        if decode:
            z_prev = sh0_ref[:, cols]
            sh_out_ref[:, cols] = z
        else:
            z_prev = jnp.where(row == 0, carry_sh[0:1, cols], pltpu.roll(z, 1, 0))
            carry_sh[0:1, cols] = z[tm - 1:tm, :]
            sh_out_ref[0, :, cols] = z[tm - 1:tm, :]
        return z + mu_ref[:, cols] * (z_prev - z)

    tail = shifted(3 * d, shift_cols - 3 * d)
    tail_tanh = jnp.tanh(tail).astype(BF16)
    tail_sig = _sigmoid(tail).astype(BF16)
    tail_raw = tail.astype(BF16)
    for c in range(0, d, blk):
        cols = slice(c, c + blk)
        if c == blk:
            _cast_blocks(cast_srcs, cast_dsts)
        r_ref[:, cols] = shifted(c, blk)
        k_ref[:, cols] = shifted(d + c, blk)
        v_ref[:, cols] = shifted(2 * d + c, blk)
        dec = dbias_ref[:, cols] + jnp.dot(tail_tanh, wdec_ref[:, cols],
                                           preferred_element_type=F32)
        ld_ref[:, cols] = -jnp.exp(-_softplus(-dec) - 0.5)
        al_ref[:, cols] = _sigmoid(ibias_ref[:, cols] + jnp.dot(
            tail_raw, wicl_ref[:, cols], preferred_element_type=F32))
        gate = jnp.dot(tail_sig, wgat_ref[:, cols], preferred_element_type=F32)
        gg_ref[:, cols] = gate * _sigmoid(proj(gate0 + c, blk))
        cu = proj(conv0 + d + c, blk) * proj(conv0 + 2 * d + c, blk)
        if decode:
            cu_m2 = c0a_ref[:, cols]
            cu_m1 = c0b_ref[:, cols]
            cv_out_ref[:, cols] = cu
        else:
            cu_m1 = jnp.where(row == 0, carry_cu[1:2, cols], pltpu.roll(cu, 1, 0))
            cu_m2 = jnp.where(row == 0, carry_cu[0:1, cols],
                              jnp.where(row == 1, carry_cu[1:2, cols], pltpu.roll(cu, 2, 0)))
            carry_cu[0:2, cols] = cu[tm - 2:tm, :]
            cv_out_ref[0, :, cols] = cu[tm - 2:tm, :]
        conv = cu_m2 * cw_ref[0:1, cols] + cu_m1 * cw_ref[1:2, cols] + cu * cw_ref[2:3, cols]
        gyb_ref[:, cols] = _sigmoid(proj(gate0 + d + c, blk)) * (proj(conv0 + c, blk) * conv)


def _mix_in(x, nm, w_in, mu, sh0, c0a, c0b, wdec, wicl, wgat, dbias, ibias, cw,
            *, decode, batch, seq, tm, cast=()):
    m, d = x.shape
    shift_cols = mu.shape[1]
    consts = [nm, w_in, mu]
    lora = [wdec, wicl, wgat, dbias, ibias, cw]
    if decode:
        grid = (1,)
        row = lambda w: pl.BlockSpec((m, w), lambda i: (0, 0))
        state_specs = [row(shift_cols), row(d), row(d)]
        state_out = [row(shift_cols), row(d)]
        state_shapes = [jax.ShapeDtypeStruct((m, shift_cols), F32),
                        jax.ShapeDtypeStruct((m, d), F32)]
        scratch = []
        sem = ("arbitrary",)
    else:
        nt = seq // tm
        grid = (batch, nt)
        row = lambda w: pl.BlockSpec((tm, w), lambda b, t: (b * nt + t, 0))
        per_b = lambda r, w: pl.BlockSpec((1, r, w), lambda b, t: (b, 0, 0))
        state_specs = [per_b(1, shift_cols), per_b(1, d), per_b(1, d)]
        state_out = [per_b(1, shift_cols), per_b(2, d)]
        state_shapes = [jax.ShapeDtypeStruct((batch, 1, shift_cols), F32),
                        jax.ShapeDtypeStruct((batch, 2, d), F32)]
        scratch = [pltpu.VMEM((8, shift_cols), F32), pltpu.VMEM((8, d), F32)]
        sem = ("arbitrary", "arbitrary")
    tok = jax.ShapeDtypeStruct((m, d), F32)
    assert not (decode and cast)
    cast_specs, cast_shapes = _cast_plan(
        cast, grid[0] * grid[-1], lambda b, t: b * grid[-1] + t)
    return pl.pallas_call(
        functools.partial(_mix_in_kernel, decode=decode, d=d, shift_cols=shift_cols,
                          blk=MIX_COL_BLOCK, n_cast=len(cast)),
        grid=grid,
        in_specs=([row(d)] + [_const_spec(a.shape) for a in consts] + state_specs
                  + [_const_spec(a.shape) for a in lora] + cast_specs),
        out_specs=[row(d)] * 7 + state_out + cast_specs,
        out_shape=[tok] * 7 + state_shapes + cast_shapes,
        scratch_shapes=scratch,
        compiler_params=pltpu.CompilerParams(
            dimension_semantics=sem, vmem_limit_bytes=VMEM_LIMIT),
        name="mix_in_decode" if decode else "mix_in",
    )(x, *consts, sh0, c0a, c0b, *lora, *cast)


def _head_sum(x, first):
    x1 = jnp.where(first, x, 0.0)
    s1 = jnp.sum(x1, axis=-1, keepdims=True)
    s2 = jnp.sum(x - x1, axis=-1, keepdims=True)
    return jnp.where(first, s1, s2)


def _stack_heads(x):
    x = x.astype(BF16)
    lane = lax.broadcasted_iota(jnp.int32, x.shape, 1) % LANES
    first = lane < HEAD
    zero = jnp.zeros_like(x)
    return jnp.concatenate([jnp.where(first, x, zero), jnp.where(first, zero, x)], axis=0)


def _apply(mat, x):
    return jnp.dot(mat.astype(BF16), _stack_heads(x), preferred_element_type=F32)


def _split3(x):
    hi = x.astype(BF16)
    rest = x - hi.astype(F32)
    mid = rest.astype(BF16)
    return hi, mid, (rest - mid.astype(F32)).astype(BF16)


def _wkv_kernel(r_ref, k_ref, v_ref, al_ref, ld_ref, gg_ref, gyb_ref, kk_ref, ka_ref, rk_ref,
                gw_ref, gb_ref, y_ref, st_ref, h_ref, *, group, lag):
    c = pl.program_id(1)
    n_pairs = h_ref.shape[0]

    @pl.when(c == 0)
    def _():
        h_ref[...] = jnp.zeros_like(h_ref)

    C = CHUNK
    row = lax.broadcasted_iota(jnp.int32, (C, LANES), 0)
    col = lax.broadcasted_iota(jnp.int32, (C, LANES), 1)
    first = col < HEAD
    j = col % HEAD
    strict = j < row
    incl = j <= row
    cs_r = lax.broadcasted_iota(jnp.int32, (C, 3 * C), 0)
    cs_c = lax.broadcasted_iota(jnp.int32, (C, 3 * C), 1) % C
    tri3 = (cs_c <= cs_r).astype(BF16)
    sq_r = lax.broadcasted_iota(jnp.int32, (LANES, LANES), 0)
    sq_c = lax.broadcasted_iota(jnp.int32, (LANES, LANES), 1)
    same_head = (sq_r // HEAD) == (sq_c // HEAD)
    eye = sq_r == sq_c

    def prefix(q, p):
        rows = slice(q * C, (q + 1) * C)
        sl = slice(p * LANES, (p + 1) * LANES)
        ld = ld_ref[rows, sl]
        lp = jnp.dot(tri3, jnp.concatenate(_split3(ld), axis=0), preferred_element_type=F32)
        return dict(p=p, rows=rows, sl=sl, ld=ld, lp=lp)

    def scores(st):
        rows, sl, ld, lp = st["rows"], st["sl"], st["ld"], st["lp"]
        r = r_ref[rows, sl]
        k = k_ref[rows, sl]
        v = v_ref[rows, sl]
        al = al_ref[rows, sl]
        lp_end = lp[C - 1:C, :]
        kkr = k * kk_ref[:, sl]
        nrm = jnp.sqrt(_head_sum(kkr * kkr, first))
        kk = kkr / jnp.maximum(nrm, NORM_EPS)
        kmod = k * (1.0 + (al - 1.0) * ka_ref[:, sl])
        bv = kk * al
        e_neg = jnp.exp(-lp)
        e_end = jnp.exp(lp_end - lp)
        st.update(v=v, at=kk * jnp.exp(lp - ld), rt=r * jnp.exp(lp),
                  bh=bv * e_end, kh=kmod * e_end, p_end=jnp.exp(lp_end),
                  bonus=_head_sum(r * kmod * rk_ref[:, sl], first) * v)
        sc = lax.dot_general(
            jnp.concatenate([st["at"], st["rt"]], axis=0).astype(BF16),
            jnp.concatenate([_stack_heads(bv * e_neg), _stack_heads(kmod * e_neg)], axis=0),
            (((1,), (1,)), ((), ())), preferred_element_type=F32)
        st["lab"] = jnp.where(strict, sc[0:C, 0:LANES], 0.0)
        st["lak"] = jnp.where(strict, sc[0:C, LANES:2 * LANES], 0.0)
        st["mrb"] = jnp.where(incl, sc[C:2 * C, 0:LANES], 0.0)
        st["mrk"] = jnp.where(incl, sc[C:2 * C, LANES:2 * LANES], 0.0)
        st["inv"] = (jnp.where(j == row, 1.0, 0.0)
                     - jnp.where((j // 2) == (row // 2), st["lab"], 0.0))
        lab_b = st["lab"].astype(BF16)
        st["lab2"] = jnp.concatenate([lab_b, lab_b], axis=0)

    tok_r = sq_r % HEAD
    tok_c = sq_c % HEAD

    def double_left(st, s):
        keep = same_head & ((tok_r // (2 * s)) == (tok_c // (2 * s))) & ((tok_r // s) != (tok_c // s))
        st["inv_b"] = st["inv"].astype(BF16)
        st["inv_off"] = jnp.dot(st["inv_b"], jnp.where(keep, st["lab2"], jnp.zeros_like(st["lab2"])),
                                preferred_element_type=F32)

    def double_right(st):
        inv2 = jnp.concatenate([st["inv_b"], st["inv_b"]], axis=0)
        st["inv"] = st["inv"] - jnp.dot(st["inv_off"].astype(BF16),
                                        jnp.where(same_head, inv2, jnp.zeros_like(inv2)),
                                        preferred_element_type=F32)

    def values(st):
        st["xv"] = _apply(jnp.concatenate([st["lak"], st["mrk"]], axis=0), st["v"])
        st["lhs_t"] = jnp.concatenate([st["kh"], -st["bh"]], axis=0).T.astype(BF16)

    def solve(st):
        st["uw"] = _apply(st["inv"], jnp.concatenate([st["at"], st["xv"][0:C]], axis=1))

    def outputs(st):
        uw, v = st["uw"], st["v"]
        mu = _apply(st["mrb"], uw)
        st["q"] = (st["rt"] - mu[:, 0:LANES]).astype(BF16)
        st["y0"] = st["xv"][C:2 * C] - mu[:, LANES:2 * LANES]
        rhs = jnp.concatenate(
            [jnp.concatenate([v, jnp.zeros_like(v)], axis=1),
             jnp.concatenate([uw[:, LANES:2 * LANES], -uw[:, 0:LANES]], axis=1)], axis=0)
        gb = jnp.dot(st["lhs_t"], rhs.astype(BF16), preferred_element_type=F32)
        st["g_c"] = jnp.where(same_head, gb[:, 0:LANES], 0.0)
        st["a_c"] = jnp.where(same_head, jnp.where(eye, st["p_end"], 0.0)
                              - gb[:, LANES:2 * LANES], 0.0).astype(BF16)

    def advance(st):
        p = st["p"]
        qa = jnp.dot(jnp.concatenate([st["q"], st["a_c"]], axis=0), h_ref[p].astype(BF16),
                     preferred_element_type=F32)
        st["y"] = qa[0:C] + st["y0"]
        h_ref[p] = qa[C:C + LANES] + st["g_c"]

    def normalize(st):
        sl, y = st["sl"], st["y"]
        mean = _head_sum(y, first) * (1.0 / HEAD)
        yc = y - mean
        var = _head_sum(yc * yc, first) * (1.0 / HEAD)
        yn = yc * lax.rsqrt(var + GN_EPS) * gw_ref[:, sl] + gb_ref[:, sl]
        merged = (yn + st["bonus"]) * gg_ref[st["rows"], sl] + gyb_ref[st["rows"], sl]
        y_ref[st["rows"], sl] = merged.astype(BF16)

    def program(chunks):
        units = []

        def s_prefix():
            units.extend(prefix(q, p) for q in chunks for p in range(n_pairs))

        def each(fn, *args):
            return lambda: [fn(st, *args) for st in units]

        stages = [s_prefix, each(scores), each(values)]
        s = 2
        while s < C:
            stages += [each(double_left, s), each(double_right)]
            s *= 2
        return stages + [each(solve), each(outputs), each(advance), each(normalize)]

    n_chunks = r_ref.shape[0] // C
    programs = [program(range(g, g + group)) for g in range(0, n_chunks, group)]
    for t in range(len(programs[0]) + lag * (len(programs) - 1)):
        for g, stages in enumerate(programs):
            if 0 <= t - g * lag < len(stages):
                stages[t - g * lag]()

    @pl.when(c == pl.num_programs(1) - 1)
    def _():
        for p in range(n_pairs):
            s_pair = h_ref[p].T
            st_ref[0, 2 * p] = s_pair[0:HEAD, 0:HEAD]
            st_ref[0, 2 * p + 1] = pltpu.roll(s_pair, HEAD, 1)[HEAD:2 * HEAD, 0:HEAD]


def _wkv(r, k, v, al, ld, gg, gyb, kk, ka, rk, gw, gb, *, batch, seq):
    m, d = r.shape
    rows = WKV_CHUNKS_PER_STEP * CHUNK
    nc = seq // rows
    n_pairs = d // LANES
    tok = pl.BlockSpec((rows, d), lambda b, c: (b * nc + c, 0))
    par = pl.BlockSpec((1, d), lambda b, c: (0, 0))
    return pl.pallas_call(
        functools.partial(_wkv_kernel, group=WKV_CHUNKS_PER_GROUP, lag=WKV_STAGE_LAG),
        grid=(batch, nc),
        in_specs=[tok] * 7 + [par] * 5,
        out_specs=[tok, pl.BlockSpec((1, 2 * n_pairs, HEAD, HEAD), lambda b, c: (b, 0, 0, 0))],
        out_shape=[jax.ShapeDtypeStruct((m, d), BF16),
                   jax.ShapeDtypeStruct((batch, 2 * n_pairs, HEAD, HEAD), F32)],
        scratch_shapes=[pltpu.VMEM((n_pairs, LANES, LANES), F32)],
        compiler_params=pltpu.CompilerParams(
            dimension_semantics=("arbitrary", "arbitrary"), vmem_limit_bytes=VMEM_LIMIT),
        name="wkv",
    )(r, k, v, al, ld, gg, gyb, kk, ka, rk, gw, gb)


def _wkv_step_kernel(r_ref, k_ref, v_ref, al_ref, ld_ref, gg_ref, gyb_ref, kk_ref, ka_ref,
                     rk_ref, gw_ref, gb_ref, s_ref, y_ref, so_ref):
    hb, _, _, nb = s_ref.shape
    r_t, k_t, v_t, al_t, ld_t = [ref[...].T for ref in (r_ref, k_ref, v_ref, al_ref, ld_ref)]
    diag = (lax.broadcasted_iota(jnp.int32, (HEAD, HEAD, nb), 0)
            == lax.broadcasted_iota(jnp.int32, (HEAD, HEAD, nb), 1))
    ys = []
    for i in range(hb):
        rows = slice(i * HEAD, (i + 1) * HEAD)
        r, k, v, al = r_t[rows], k_t[rows], v_t[rows], al_t[rows]
        decay = jnp.exp(ld_t[rows])
        kkr = k * kk_ref[rows, :]
        nrm = jnp.sqrt(jnp.sum(kkr * kkr, axis=0, keepdims=True))
        kk = kkr / jnp.maximum(nrm, NORM_EPS)
        kmod = k * (1.0 + (al - 1.0) * ka_ref[rows, :])
        bv = kk * al
        s = s_ref[i]
        sa = jnp.sum(s * kk[None], axis=1, keepdims=True)
        v3 = jnp.sum(jnp.where(diag, v[None], 0.0), axis=1, keepdims=True)
        s_new = s * decay[None] - sa * bv[None] + v3 * kmod[None]
        so_ref[i] = s_new
        y3 = jnp.sum(s_new * r[None], axis=1, keepdims=True)
        y = jnp.sum(jnp.where(diag, y3, 0.0), axis=0)
        mean = jnp.mean(y, axis=0, keepdims=True)
        yc = y - mean
        var = jnp.mean(yc * yc, axis=0, keepdims=True)
        yn = yc * lax.rsqrt(var + GN_EPS) * gw_ref[rows, :] + gb_ref[rows, :]
        bonus = jnp.sum(r * kmod * rk_ref[rows, :], axis=0, keepdims=True) * v
        ys.append(yn + bonus)
    ya = jnp.concatenate(ys, axis=0).T
    y_ref[...] = (ya * gg_ref[...] + gyb_ref[...]).astype(BF16)


def _wkv_step(r, k, v, al, ld, gg, gyb, kk, ka, rk, gw, gb, s0, hb):
    nh, _, _, b = s0.shape
    d = r.shape[1]
    tok = pl.BlockSpec((b, hb * HEAD), lambda i: (0, i))
    par = pl.BlockSpec((hb * HEAD, 1), lambda i: (i, 0))
    st = pl.BlockSpec((hb, HEAD, HEAD, b), lambda i: (i, 0, 0, 0))
    return pl.pallas_call(
        _wkv_step_kernel,
        grid=(nh // hb,),
        in_specs=[tok] * 7 + [par] * 5 + [st],
        out_specs=[tok, st],
        out_shape=[jax.ShapeDtypeStruct((b, d), BF16), jax.ShapeDtypeStruct(s0.shape, F32)],
        compiler_params=pltpu.CompilerParams(
            dimension_semantics=("arbitrary",), vmem_limit_bytes=VMEM_LIMIT),
        name="wkv_step",
    )(r, k, v, al, ld, gg, gyb, kk, ka, rk, gw, gb, s0)


def _tail_kernel(x_ref, mg_ref, p_ref, wo_ref, nf_ref, wg_ref, wu_ref,
                 wd_ref, npl_ref, wpg_ref, wpp_ref, nfin_ref, o_ref, *, final):
    parts = _row_parts(x_ref.shape[0])
    xs = [x_ref[rows, :] + jnp.dot(mg_ref[rows, :], wo_ref[...], preferred_element_type=F32)
          for rows in parts]
    xs = _swiglu_half(xs, nf_ref[...], wg_ref, wu_ref, wd_ref)
    hps = [_rms(x, npl_ref[...]).astype(BF16) for x in xs]
    ple_gates = [jnp.dot(hp, wpg_ref[...], preferred_element_type=F32) for hp in hps]
    ples = [_bdot(p_ref[rows, :], wpp_ref[...]) for rows in parts]
    for rows, x, gate, ple in zip(parts, xs, ple_gates, ples):
        x = x + _sigmoid(gate) * ple
        o_ref[rows, :] = _rms(x, nfin_ref[...]) if final else x


def _tail(x, merged, p, wo, nf, wg, wu, wd, npl, wpg, wpp, nfin, *, final, tm):
    m, d = x.shape
    row = lambda w: pl.BlockSpec((tm, w), lambda i: (i, 0))
    weights = [wo, nf, wg, wu, wd, npl, wpg, wpp, nfin]
    return pl.pallas_call(
        functools.partial(_tail_kernel, final=final),
        grid=(m // tm,),
        in_specs=[row(d)] * 2 + [row(p.shape[1])] + [_const_spec(w.shape) for w in weights],
        out_specs=row(d),
        out_shape=jax.ShapeDtypeStruct((m, d), F32),
        compiler_params=pltpu.CompilerParams(
            dimension_semantics=("arbitrary",), vmem_limit_bytes=VMEM_LIMIT),
        name="tail",
    )(x, merged, p, *weights)


CAST_IN_FFN = ("w_in",)
CAST_IN_MIX = ("w_out", "ffn2_gate", "ffn2_up", "ffn2_down", "ple_gate")


def _layer(x, p, s0, sh0, cv0, w, *, decode, batch, seq, final):
    d = x.shape[1]
    x1, cast_out = _ffn(x, w["norm_ffn1"], w["ffn1_gate"], w["ffn1_up"], w["ffn1_down"],
                        tm=min(512, x.shape[0]),
                        cast=() if decode else [w[name] for name in CAST_IN_FFN])
    w = dict(w, **dict(zip(CAST_IN_FFN, cast_out)))
    if decode:
        sh_in, c0a, c0b = sh0, cv0[:, 0], cv0[:, 1]
    else:
        sh_in, c0a, c0b = sh0[:, None], cv0[:, 0:1], cv0[:, 1:2]
    r, k, v, al, ld, gg, gyb, sh_new, cv_new, *cast_out = _mix_in(
        x1, w["norm_mix"], w["w_in"], w["mu"], sh_in, c0a, c0b, w["wdec"], w["wicl"],
        w["wgat"], w["decay_bias"], w["iclr_bias"], w["conv_w"],
        decode=decode, batch=batch, seq=seq, tm=min(256, x.shape[0]),
        cast=() if decode else [w[name] for name in CAST_IN_MIX])
    w = dict(w, **dict(zip(CAST_IN_MIX, cast_out)))
    pars = [w["k_k"], w["k_a"], w["r_k"], w["gn_w"], w["gn_b"]]
    if decode:
        merged, s_new = _wkv_step(r, k, v, al, ld, gg, gyb, *[a.reshape(d, 1) for a in pars],
                                  jnp.transpose(s0, (1, 2, 3, 0)), hb=2)
        s_new = jnp.transpose(s_new, (3, 0, 1, 2))
        cv_new = jnp.stack([cv0[:, 1], cv_new], axis=1)
    else:
        merged, s_new = _wkv(r, k, v, al, ld, gg, gyb, *pars, batch=batch, seq=seq)
        sh_new = sh_new[:, 0]
    x_out = _tail(x1, merged, p, w["w_out"], w["norm_ffn2"], w["ffn2_gate"],
                  w["ffn2_up"], w["ffn2_down"], w["norm_ple"], w["ple_gate"], w["ple_proj"],
                  w["norm_final"], final=final, tm=min(512, x.shape[0]))
    return x_out, s_new, sh_new, cv_new, w


def kernel(x_prompt, x_sample, p_prompt, p_sample, state_wkv, state_shift, state_conv,
           norm_ffn1, w_ffn1_gate, w_ffn1_up, w_ffn1_down, norm_mix, w_in, mu_shift,
           w_decay_up, decay_bias, w_iclr_up, iclr_bias, w_gate_up, k_k, k_a, r_k, gn_w,
           gn_b, conv_w, w_out, norm_ffn2, w_ffn2_gate, w_ffn2_up, w_ffn2_down, norm_ple,
           w_ple_gate, w_ple_proj, norm_final):
    depth = w_in.shape[0]
    bp, tp, d = x_prompt.shape
    bs, ts, _ = x_sample.shape
    assert ts == 1 and tp % (WKV_CHUNKS_PER_STEP * CHUNK) == 0
    shift_cols = mu_shift.shape[1]
    n_dec, n_icl, n_gat = w_decay_up.shape[1], w_iclr_up.shape[1], w_gate_up.shape[1]
    assert 3 * d + n_dec + n_icl + n_gat == shift_cols

    xp = x_prompt.reshape(bp * tp, d)
    xs = x_sample.reshape(bs, d)
    vec = lambda a: a.reshape(1, -1)
    outs = [[] for _ in range(6)]
    for i in range(depth):
        n_tail = shift_cols - 3 * d
        pad = lambda a, lo: jnp.zeros((n_tail, d), BF16).at[lo:lo + a.shape[0]].set(a.astype(BF16))
        w = dict(
            norm_ffn1=vec(norm_ffn1[i]), ffn1_gate=w_ffn1_gate[i].astype(BF16),
            ffn1_up=w_ffn1_up[i].astype(BF16), ffn1_down=w_ffn1_down[i].astype(BF16),
            norm_mix=vec(norm_mix[i]), w_in=w_in[i], mu=vec(mu_shift[i]),
            wdec=pad(w_decay_up[i], 0), wicl=pad(w_iclr_up[i], n_dec),
            wgat=pad(w_gate_up[i], n_dec + n_icl),
            decay_bias=vec(decay_bias[i]), iclr_bias=vec(iclr_bias[i]), conv_w=conv_w[i],
            k_k=vec(k_k[i]), k_a=vec(k_a[i]), r_k=vec(r_k[i]), gn_w=vec(gn_w[i]),
            gn_b=vec(gn_b[i]), w_out=w_out[i], norm_ffn2=vec(norm_ffn2[i]),
            ffn2_gate=w_ffn2_gate[i], ffn2_up=w_ffn2_up[i],
            ffn2_down=w_ffn2_down[i], norm_ple=vec(norm_ple[i]),
            ple_gate=w_ple_gate[i], ple_proj=w_ple_proj[i].astype(BF16),
            norm_final=vec(norm_final))
        final = i == depth - 1
        xp, s_p, sh_p, cv_p, w = _layer(
            xp, p_prompt[i].reshape(bp * tp, -1), None,
            jnp.zeros((bp, shift_cols), F32), jnp.zeros((bp, 2, d), F32), w,
            decode=False, batch=bp, seq=tp, final=final)
        xs, s_s, sh_s, cv_s, _ = _layer(
            xs, p_sample[i].reshape(bs, -1), state_wkv[i], state_shift[i], state_conv[i], w,
            decode=True, batch=bs, seq=1, final=final)
        for lst, val in zip(outs, (s_p, sh_p, cv_p, s_s, sh_s, cv_s)):
            lst.append(val)
    stk = [jnp.stack(o, 0) for o in outs]
    return (xp.reshape(bp, tp, d), xs.reshape(bs, ts, d), stk[0], stk[1], stk[2],
            stk[3], stk[4], stk[5])
```

```python
import functools

import jax
import jax.numpy as jnp
from jax import lax
from jax.experimental import pallas as pl
from jax.experimental.pallas import tpu as pltpu

F32 = jnp.float32
BF16 = jnp.bfloat16

HEAD = 64
LANES = 128
CHUNK = 64
MXU_DIM = 256
MIX_COL_BLOCK = MXU_DIM
WKV_CHUNKS_PER_STEP = 4
WKV_CHUNKS_PER_GROUP = 2
WKV_STAGE_LAG = 7
RMS_EPS = 1e-6
GN_EPS = 64e-5
NORM_EPS = 1e-12
VMEM_LIMIT = 56 * 1024 * 1024


def _rms(x, g):
    return x * lax.rsqrt(jnp.mean(x * x, axis=-1, keepdims=True) + RMS_EPS) * g


def _sigmoid(x):
    return 1.0 / (1.0 + jnp.exp(-x))


def _softplus(x):
    return jnp.maximum(x, 0.0) + jnp.log(1.0 + jnp.exp(-jnp.abs(x)))


def _bdot(a, b):
    return jnp.dot(a.astype(BF16), b, preferred_element_type=F32)


def _const_spec(shape):
    zeros = (0,) * len(shape)
    return pl.BlockSpec(shape, lambda *_: zeros, pipeline_mode=pl.Buffered(1))


def _cast_plan(arrays, n_steps, step_of):
    specs, shapes = [], []
    for a in arrays:
        rows, cols = a.shape
        blk = next(b for b in range(16, rows + 1, 16) if rows % b == 0 and rows // b <= n_steps)
        last = rows // blk - 1
        specs.append(pl.BlockSpec(
            (blk, cols), lambda *ids, last=last: (jnp.minimum(step_of(*ids), last), 0)))
        shapes.append(jax.ShapeDtypeStruct(a.shape, BF16))
    return specs, shapes


def _cast_blocks(srcs, dsts):
    for src, dst in zip(srcs, dsts):
        dst[...] = src[...].astype(BF16)


def _row_parts(tm):
    sub = tm // 2 if tm >= 2 * MXU_DIM else tm
    return [slice(s, s + sub) for s in range(0, tm, sub)]


def _swiglu_half(xs, g, wg_ref, wu_ref, wd_ref):
    hs = [_rms(x, g).astype(BF16) for x in xs]
    gates = [jnp.dot(h, wg_ref[...], preferred_element_type=F32) for h in hs]
    ups = [jnp.dot(h, wu_ref[...], preferred_element_type=F32) for h in hs]
    acts = [(gate * _sigmoid(gate) * up).astype(BF16) for gate, up in zip(gates, ups)]
    return [x + 0.5 * jnp.dot(act, wd_ref[...], preferred_element_type=F32)
            for x, act in zip(xs, acts)]


def _ffn_kernel(x_ref, g_ref, wg_ref, wu_ref, wd_ref, *rest, n_cast):
    cast_srcs, o_ref, cast_dsts = rest[:n_cast], rest[n_cast], rest[n_cast + 1:]
    _cast_blocks(cast_srcs, cast_dsts)
    parts = _row_parts(x_ref.shape[0])
    outs = _swiglu_half([x_ref[rows, :] for rows in parts], g_ref[...], wg_ref, wu_ref, wd_ref)
    for rows, out in zip(parts, outs):
        o_ref[rows, :] = out


def _ffn(x, g, wg, wu, wd, tm, cast=()):
    m, d = x.shape
    row = pl.BlockSpec((tm, d), lambda i: (i, 0))
    cast_specs, cast_shapes = _cast_plan(cast, m // tm, lambda i: i)
    out = pl.pallas_call(
        functools.partial(_ffn_kernel, n_cast=len(cast)),
        grid=(m // tm,),
        in_specs=[row, _const_spec(g.shape), _const_spec(wg.shape),
                  _const_spec(wu.shape), _const_spec(wd.shape)] + cast_specs,
        out_specs=[row] + cast_specs,
        out_shape=[jax.ShapeDtypeStruct((m, d), F32)] + cast_shapes,
        compiler_params=pltpu.CompilerParams(
            dimension_semantics=("arbitrary",), vmem_limit_bytes=VMEM_LIMIT),
        name="ffn",
    )(x, g, wg, wu, wd, *cast)
    return out[0], out[1:]


def _ffn_stream_kernel(x_ref, g_ref, wg_ref, wu_ref, wd_ref, o_ref, wgb_ref, wub_ref, wdb_ref,
                       h_ref, acc_ref):
    j = pl.program_id(0)

    @pl.when(j == 0)
    def _():
        h_ref[...] = _rms(x_ref[...], g_ref[...]).astype(BF16)
        acc_ref[...] = jnp.zeros_like(acc_ref)

    wg = wg_ref[...].astype(BF16)
    wu = wu_ref[...].astype(BF16)
    wd = wd_ref[...].astype(BF16)
    wgb_ref[...] = wg
    wub_ref[...] = wu
    wdb_ref[...] = wd
    h = h_ref[...]
    gate = jnp.dot(h, wg, preferred_element_type=F32)
    up = jnp.dot(h, wu, preferred_element_type=F32)
    acc_ref[...] += jnp.dot((gate * _sigmoid(gate) * up).astype(BF16), wd,
                            preferred_element_type=F32)

    @pl.when(j == pl.num_programs(0) - 1)
    def _():
        o_ref[...] = x_ref[...] + 0.5 * acc_ref[...]


def _ffn_stream(x, g, wg, wu, wd):
    m, d = x.shape
    d_ff = wg.shape[1]
    ck = MXU_DIM
    cols = pl.BlockSpec((d, ck), lambda j: (0, j))
    rows = pl.BlockSpec((ck, d), lambda j: (j, 0))
    whole = pl.BlockSpec((m, d), lambda j: (0, 0))
    return pl.pallas_call(
        _ffn_stream_kernel,
        grid=(d_ff // ck,),
        in_specs=[whole, _const_spec(g.shape), cols, cols, rows],
        out_specs=[whole, cols, cols, rows],
        out_shape=[jax.ShapeDtypeStruct((m, d), F32), jax.ShapeDtypeStruct(wg.shape, BF16),
                   jax.ShapeDtypeStruct(wu.shape, BF16), jax.ShapeDtypeStruct(wd.shape, BF16)],
        scratch_shapes=[pltpu.VMEM((m, d), BF16), pltpu.VMEM((m, d), F32)],
        compiler_params=pltpu.CompilerParams(
            dimension_semantics=("arbitrary",), vmem_limit_bytes=VMEM_LIMIT),
        name="ffn_stream",
    )(x, g, wg, wu, wd)


def _mix_in_kernel(*refs, decode, d, shift_cols, blk, n_cast):
    n = n_cast
    (x_ref, nm_ref, w_in_ref, mu_ref, sh0_ref, c0a_ref, c0b_ref,
     wdec_ref, wicl_ref, wgat_ref, dbias_ref, ibias_ref, cw_ref) = refs[:13]
    cast_srcs = refs[13:13 + n]
    (r_ref, k_ref, v_ref, al_ref, ld_ref, gg_ref, gyb_ref,
     sh_out_ref, cv_out_ref) = refs[13 + n:22 + n]
    cast_dsts = refs[22 + n:22 + 2 * n]
    scratch = refs[22 + 2 * n:]
    tm = x_ref.shape[0]
    h = _rms(x_ref[...], nm_ref[...]).astype(BF16)
    conv0, gate0 = shift_cols, shift_cols + 3 * d

    if not decode:
        carry_sh, carry_cu = scratch

        @pl.when(pl.program_id(1) == 0)
        def _():
            carry_sh[0:1, :] = sh0_ref[0]
            carry_cu[0:1, :] = c0a_ref[0]
            carry_cu[1:2, :] = c0b_ref[0]

        row = lax.broadcasted_iota(jnp.int32, (tm, 1), 0)

    def proj(c0, width):
        return jnp.dot(h, w_in_ref[:, c0:c0 + width], preferred_element_type=F32)

    def shifted(c0, width):
        cols = slice(c0, c0 + width)
        z = proj(c0, width)
        if decode:
            z_prev = sh0_ref[:, cols]
            sh_out_ref[:, cols] = z
        else:
            z_prev = jnp.where(row == 0, carry_sh[0:1, cols], pltpu.roll(z, 1, 0))
            carry_sh[0:1, cols] = z[tm - 1:tm, :]
            sh_out_ref[0, :, cols] = z[tm - 1:tm, :]
        return z + mu_ref[:, cols] * (z_prev - z)

    tail = shifted(3 * d, shift_cols - 3 * d)
    tail_tanh = jnp.tanh(tail).astype(BF16)
    tail_sig = _sigmoid(tail).astype(BF16)
    tail_raw = tail.astype(BF16)
    for c in range(0, d, blk):
        cols = slice(c, c + blk)
        if c == blk:
            _cast_blocks(cast_srcs, cast_dsts)
        r_ref[:, cols] = shifted(c, blk)
        k_ref[:, cols] = shifted(d + c, blk)
        v_ref[:, cols] = shifted(2 * d + c, blk)
        dec = dbias_ref[:, cols] + jnp.dot(tail_tanh, wdec_ref[:, cols],
                                           preferred_element_type=F32)
        ld_ref[:, cols] = -jnp.exp(-_softplus(-dec) - 0.5)
        al_ref[:, cols] = _sigmoid(ibias_ref[:, cols] + jnp.dot(
            tail_raw, wicl_ref[:, cols], preferred_element_type=F32))
        gate = jnp.dot(tail_sig, wgat_ref[:, cols], preferred_element_type=F32)
        gg_ref[:, cols] = gate * _sigmoid(proj(gate0 + c, blk))
        cu = proj(conv0 + d + c, blk) * proj(conv0 + 2 * d + c, blk)
        if decode:
            cu_m2 = c0a_ref[:, cols]
            cu_m1 = c0b_ref[:, cols]
            cv_out_ref[:, cols] = cu
        else:
            cu_m1 = jnp.where(row == 0, carry_cu[1:2, cols], pltpu.roll(cu, 1, 0))
            cu_m2 = jnp.where(row == 0, carry_cu[0:1, cols],
                              jnp.where(row == 1, carry_cu[1:2, cols], pltpu.roll(cu, 2, 0)))
            carry_cu[0:2, cols] = cu[tm - 2:tm, :]
            cv_out_ref[0, :, cols] = cu[tm - 2:tm, :]
        conv = cu_m2 * cw_ref[0:1, cols] + cu_m1 * cw_ref[1:2, cols] + cu * cw_ref[2:3, cols]
        gyb_ref[:, cols] = _sigmoid(proj(gate0 + d + c, blk)) * (proj(conv0 + c, blk) * conv)


def _mix_in(x, nm, w_in, mu, sh0, c0a, c0b, wdec, wicl, wgat, dbias, ibias, cw,
            *, decode, batch, seq, tm, cast=()):
    m, d = x.shape
    shift_cols = mu.shape[1]
    consts = [nm, w_in, mu]
    lora = [wdec, wicl, wgat, dbias, ibias, cw]
    if decode:
        grid = (1,)
        row = lambda w: pl.BlockSpec((m, w), lambda i: (0, 0))
        state_specs = [row(shift_cols), row(d), row(d)]
        state_out = [row(shift_cols), row(d)]
        state_shapes = [jax.ShapeDtypeStruct((m, shift_cols), F32),
                        jax.ShapeDtypeStruct((m, d), F32)]
        scratch = []
        sem = ("arbitrary",)
    else:
        nt = seq // tm
        grid = (batch, nt)
        row = lambda w: pl.BlockSpec((tm, w), lambda b, t: (b * nt + t, 0))
        per_b = lambda r, w: pl.BlockSpec((1, r, w), lambda b, t: (b, 0, 0))
        state_specs = [per_b(1, shift_cols), per_b(1, d), per_b(1, d)]
        state_out = [per_b(1, shift_cols), per_b(2, d)]
        state_shapes = [jax.ShapeDtypeStruct((batch, 1, shift_cols), F32),
                        jax.ShapeDtypeStruct((batch, 2, d), F32)]
        scratch = [pltpu.VMEM((8, shift_cols), F32), pltpu.VMEM((8, d), F32)]
        sem = ("arbitrary", "arbitrary")
    tok = jax.ShapeDtypeStruct((m, d), F32)
    assert not (decode and cast)
    cast_specs, cast_shapes = _cast_plan(
        cast, grid[0] * grid[-1], lambda b, t: b * grid[-1] + t)
    return pl.pallas_call(
        functools.partial(_mix_in_kernel, decode=decode, d=d, shift_cols=shift_cols,
                          blk=MIX_COL_BLOCK, n_cast=len(cast)),
        grid=grid,
        in_specs=([row(d)] + [_const_spec(a.shape) for a in consts] + state_specs
                  + [_const_spec(a.shape) for a in lora] + cast_specs),
        out_specs=[row(d)] * 7 + state_out + cast_specs,
        out_shape=[tok] * 7 + state_shapes + cast_shapes,
        scratch_shapes=scratch,
        compiler_params=pltpu.CompilerParams(
            dimension_semantics=sem, vmem_limit_bytes=VMEM_LIMIT),
        name="mix_in_decode" if decode else "mix_in",
    )(x, *consts, sh0, c0a, c0b, *lora, *cast)


def _head_sum(x, first):
    x1 = jnp.where(first, x, 0.0)
    s1 = jnp.sum(x1, axis=-1, keepdims=True)
    s2 = jnp.sum(x - x1, axis=-1, keepdims=True)
    return jnp.where(first, s1, s2)


def _stack_heads(x):
    x = x.astype(BF16)
    lane = lax.broadcasted_iota(jnp.int32, x.shape, 1) % LANES
    first = lane < HEAD
    zero = jnp.zeros_like(x)
    return jnp.concatenate([jnp.where(first, x, zero), jnp.where(first, zero, x)], axis=0)


def _apply(mat, x):
    return jnp.dot(mat.astype(BF16), _stack_heads(x), preferred_element_type=F32)


def _split3(x):
    hi = x.astype(BF16)
    rest = x - hi.astype(F32)
    mid = rest.astype(BF16)
    return hi, mid, (rest - mid.astype(F32)).astype(BF16)


def _wkv_kernel(r_ref, k_ref, v_ref, al_ref, ld_ref, gg_ref, gyb_ref, kk_ref, ka_ref, rk_ref,
                gw_ref, gb_ref, y_ref, st_ref, h_ref, *, group, lag):
    c = pl.program_id(1)
    n_pairs = h_ref.shape[0]

    @pl.when(c == 0)
    def _():
        h_ref[...] = jnp.zeros_like(h_ref)

    C = CHUNK
    row = lax.broadcasted_iota(jnp.int32, (C, LANES), 0)
    col = lax.broadcasted_iota(jnp.int32, (C, LANES), 1)
    first = col < HEAD
    j = col % HEAD
    strict = j < row
    incl = j <= row
    cs_r = lax.broadcasted_iota(jnp.int32, (C, 3 * C), 0)
    cs_c = lax.broadcasted_iota(jnp.int32, (C, 3 * C), 1) % C
    tri3 = (cs_c <= cs_r).astype(BF16)
    sq_r = lax.broadcasted_iota(jnp.int32, (LANES, LANES), 0)
    sq_c = lax.broadcasted_iota(jnp.int32, (LANES, LANES), 1)
    same_head = (sq_r // HEAD) == (sq_c // HEAD)
    eye = sq_r == sq_c

    def prefix(q, p):
        rows = slice(q * C, (q + 1) * C)
        sl = slice(p * LANES, (p + 1) * LANES)
        ld = ld_ref[rows, sl]
        lp = jnp.dot(tri3, jnp.concatenate(_split3(ld), axis=0), preferred_element_type=F32)
        return dict(p=p, rows=rows, sl=sl, ld=ld, lp=lp)

    def scores(st):
        rows, sl, ld, lp = st["rows"], st["sl"], st["ld"], st["lp"]
        r = r_ref[rows, sl]
        k = k_ref[rows, sl]
        v = v_ref[rows, sl]
        al = al_ref[rows, sl]
        lp_end = lp[C - 1:C, :]
        kkr = k * kk_ref[:, sl]
        nrm = jnp.sqrt(_head_sum(kkr * kkr, first))
        kk = kkr / jnp.maximum(nrm, NORM_EPS)
        kmod = k * (1.0 + (al - 1.0) * ka_ref[:, sl])
        bv = kk * al
        e_neg = jnp.exp(-lp)
        e_end = jnp.exp(lp_end - lp)
        st.update(v=v, at=kk * jnp.exp(lp - ld), rt=r * jnp.exp(lp),
                  bh=bv * e_end, kh=kmod * e_end, p_end=jnp.exp(lp_end),
                  bonus=_head_sum(r * kmod * rk_ref[:, sl], first) * v)
        sc = lax.dot_general(
            jnp.concatenate([st["at"], st["rt"]], axis=0).astype(BF16),
            jnp.concatenate([_stack_heads(bv * e_neg), _stack_heads(kmod * e_neg)], axis=0),
            (((1,), (1,)), ((), ())), preferred_element_type=F32)
        st["lab"] = jnp.where(strict, sc[0:C, 0:LANES], 0.0)
        st["lak"] = jnp.where(strict, sc[0:C, LANES:2 * LANES], 0.0)
        st["mrb"] = jnp.where(incl, sc[C:2 * C, 0:LANES], 0.0)
        st["mrk"] = jnp.where(incl, sc[C:2 * C, LANES:2 * LANES], 0.0)
        st["inv"] = (jnp.where(j == row, 1.0, 0.0)
                     - jnp.where((j // 2) == (row // 2), st["lab"], 0.0))
        lab_b = st["lab"].astype(BF16)
        st["lab2"] = jnp.concatenate([lab_b, lab_b], axis=0)

    tok_r = sq_r % HEAD
    tok_c = sq_c % HEAD

    def double_left(st, s):
        keep = same_head & ((tok_r // (2 * s)) == (tok_c // (2 * s))) & ((tok_r // s) != (tok_c // s))
        st["inv_b"] = st["inv"].astype(BF16)
        st["inv_off"] = jnp.dot(st["inv_b"], jnp.where(keep, st["lab2"], jnp.zeros_like(st["lab2"])),
                                preferred_element_type=F32)

    def double_right(st):
        inv2 = jnp.concatenate([st["inv_b"], st["inv_b"]], axis=0)
        st["inv"] = st["inv"] - jnp.dot(st["inv_off"].astype(BF16),
                                        jnp.where(same_head, inv2, jnp.zeros_like(inv2)),
                                        preferred_element_type=F32)

    def values(st):
        st["xv"] = _apply(jnp.concatenate([st["lak"], st["mrk"]], axis=0), st["v"])
        st["lhs_t"] = jnp.concatenate([st["kh"], -st["bh"]], axis=0).T.astype(BF16)

    def solve(st):
        st["uw"] = _apply(st["inv"], jnp.concatenate([st["at"], st["xv"][0:C]], axis=1))

    def outputs(st):
        uw, v = st["uw"], st["v"]
        mu = _apply(st["mrb"], uw)
        st["q"] = (st["rt"] - mu[:, 0:LANES]).astype(BF16)
        st["y0"] = st["xv"][C:2 * C] - mu[:, LANES:2 * LANES]
        rhs = jnp.concatenate(
            [jnp.concatenate([v, jnp.zeros_like(v)], axis=1),
             jnp.concatenate([uw[:, LANES:2 * LANES], -uw[:, 0:LANES]], axis=1)], axis=0)
        gb = jnp.dot(st["lhs_t"], rhs.astype(BF16), preferred_element_type=F32)
        st["g_c"] = jnp.where(same_head, gb[:, 0:LANES], 0.0)
        st["a_c"] = jnp.where(same_head, jnp.where(eye, st["p_end"], 0.0)
                              - gb[:, LANES:2 * LANES], 0.0).astype(BF16)

    def advance(st):
        p = st["p"]
        qa = jnp.dot(jnp.concatenate([st["q"], st["a_c"]], axis=0), h_ref[p].astype(BF16),
                     preferred_element_type=F32)
        st["y"] = qa[0:C] + st["y0"]
        h_ref[p] = qa[C:C + LANES] + st["g_c"]

    def normalize(st):
        sl, y = st["sl"], st["y"]
        mean = _head_sum(y, first) * (1.0 / HEAD)
        yc = y - mean
        var = _head_sum(yc * yc, first) * (1.0 / HEAD)
        yn = yc * lax.rsqrt(var + GN_EPS) * gw_ref[:, sl] + gb_ref[:, sl]
        merged = (yn + st["bonus"]) * gg_ref[st["rows"], sl] + gyb_ref[st["rows"], sl]
        y_ref[st["rows"], sl] = merged.astype(BF16)

    def program(chunks):
        units = []

        def s_prefix():
            units.extend(prefix(q, p) for q in chunks for p in range(n_pairs))

        def each(fn, *args):
            return lambda: [fn(st, *args) for st in units]

        stages = [s_prefix, each(scores), each(values)]
        s = 2
        while s < C:
            stages += [each(double_left, s), each(double_right)]
            s *= 2
        return stages + [each(solve), each(outputs), each(advance), each(normalize)]

    n_chunks = r_ref.shape[0] // C
    programs = [program(range(g, g + group)) for g in range(0, n_chunks, group)]
    for t in range(len(programs[0]) + lag * (len(programs) - 1)):
        for g, stages in enumerate(programs):
            if 0 <= t - g * lag < len(stages):
                stages[t - g * lag]()

    @pl.when(c == pl.num_programs(1) - 1)
    def _():
        for p in range(n_pairs):
            s_pair = h_ref[p].T
            st_ref[0, 2 * p] = s_pair[0:HEAD, 0:HEAD]
            st_ref[0, 2 * p + 1] = pltpu.roll(s_pair, HEAD, 1)[HEAD:2 * HEAD, 0:HEAD]


def _wkv(r, k, v, al, ld, gg, gyb, kk, ka, rk, gw, gb, *, batch, seq):
    m, d = r.shape
    rows = WKV_CHUNKS_PER_STEP * CHUNK
    nc = seq // rows
    n_pairs = d // LANES
    tok = pl.BlockSpec((rows, d), lambda b, c: (b * nc + c, 0))
    par = pl.BlockSpec((1, d), lambda b, c: (0, 0))
    return pl.pallas_call(
        functools.partial(_wkv_kernel, group=WKV_CHUNKS_PER_GROUP, lag=WKV_STAGE_LAG),
        grid=(batch, nc),
        in_specs=[tok] * 7 + [par] * 5,
        out_specs=[tok, pl.BlockSpec((1, 2 * n_pairs, HEAD, HEAD), lambda b, c: (b, 0, 0, 0))],
        out_shape=[jax.ShapeDtypeStruct((m, d), BF16),
                   jax.ShapeDtypeStruct((batch, 2 * n_pairs, HEAD, HEAD), F32)],
        scratch_shapes=[pltpu.VMEM((n_pairs, LANES, LANES), F32)],
        compiler_params=pltpu.CompilerParams(
            dimension_semantics=("arbitrary", "arbitrary"), vmem_limit_bytes=VMEM_LIMIT),
        name="wkv",
    )(r, k, v, al, ld, gg, gyb, kk, ka, rk, gw, gb)


def _wkv_step_kernel(r_ref, k_ref, v_ref, al_ref, ld_ref, gg_ref, gyb_ref, kk_ref, ka_ref,
                     rk_ref, gw_ref, gb_ref, s_ref, y_ref, so_ref):
    hb, _, _, nb = s_ref.shape
    r_t, k_t, v_t, al_t, ld_t = [ref[...].T for ref in (r_ref, k_ref, v_ref, al_ref, ld_ref)]
    diag = (lax.broadcasted_iota(jnp.int32, (HEAD, HEAD, nb), 0)
            == lax.broadcasted_iota(jnp.int32, (HEAD, HEAD, nb), 1))
    ys = []
    for i in range(hb):
        rows = slice(i * HEAD, (i + 1) * HEAD)
        r, k, v, al = r_t[rows], k_t[rows], v_t[rows], al_t[rows]
        decay = jnp.exp(ld_t[rows])
        kkr = k * kk_ref[rows, :]
        nrm = jnp.sqrt(jnp.sum(kkr * kkr, axis=0, keepdims=True))
        kk = kkr / jnp.maximum(nrm, NORM_EPS)
        kmod = k * (1.0 + (al - 1.0) * ka_ref[rows, :])
        bv = kk * al
        s = s_ref[i]
        sa = jnp.sum(s * kk[None], axis=1, keepdims=True)
        v3 = jnp.sum(jnp.where(diag, v[None], 0.0), axis=1, keepdims=True)
        s_new = s * decay[None] - sa * bv[None] + v3 * kmod[None]
        so_ref[i] = s_new
        y3 = jnp.sum(s_new * r[None], axis=1, keepdims=True)
        y = jnp.sum(jnp.where(diag, y3, 0.0), axis=0)
        mean = jnp.mean(y, axis=0, keepdims=True)
        yc = y - mean
        var = jnp.mean(yc * yc, axis=0, keepdims=True)
        yn = yc * lax.rsqrt(var + GN_EPS) * gw_ref[rows, :] + gb_ref[rows, :]
        bonus = jnp.sum(r * kmod * rk_ref[rows, :], axis=0, keepdims=True) * v
        ys.append(yn + bonus)
    ya = jnp.concatenate(ys, axis=0).T
    y_ref[...] = (ya * gg_ref[...] + gyb_ref[...]).astype(BF16)


def _wkv_step(r, k, v, al, ld, gg, gyb, kk, ka, rk, gw, gb, s0, hb):
    nh, _, _, b = s0.shape
    d = r.shape[1]
    tok = pl.BlockSpec((b, hb * HEAD), lambda i: (0, i))
    par = pl.BlockSpec((hb * HEAD, 1), lambda i: (i, 0))
    st = pl.BlockSpec((hb, HEAD, HEAD, b), lambda i: (i, 0, 0, 0))
    return pl.pallas_call(
        _wkv_step_kernel,
        grid=(nh // hb,),
        in_specs=[tok] * 7 + [par] * 5 + [st],
        out_specs=[tok, st],
        out_shape=[jax.ShapeDtypeStruct((b, d), BF16), jax.ShapeDtypeStruct(s0.shape, F32)],
        compiler_params=pltpu.CompilerParams(
            dimension_semantics=("arbitrary",), vmem_limit_bytes=VMEM_LIMIT),
        name="wkv_step",
    )(r, k, v, al, ld, gg, gyb, kk, ka, rk, gw, gb, s0)


def _tail_kernel(x_ref, mg_ref, p_ref, wo_ref, nf_ref, wg_ref, wu_ref,
                 wd_ref, npl_ref, wpg_ref, wpp_ref, nfin_ref, o_ref, *, final):
    parts = _row_parts(x_ref.shape[0])
    xs = [x_ref[rows, :] + jnp.dot(mg_ref[rows, :], wo_ref[...], preferred_element_type=F32)
          for rows in parts]
    xs = _swiglu_half(xs, nf_ref[...], wg_ref, wu_ref, wd_ref)
    hps = [_rms(x, npl_ref[...]).astype(BF16) for x in xs]
    ple_gates = [jnp.dot(hp, wpg_ref[...], preferred_element_type=F32) for hp in hps]
    ples = [_bdot(p_ref[rows, :], wpp_ref[...]) for rows in parts]
    for rows, x, gate, ple in zip(parts, xs, ple_gates, ples):
        x = x + _sigmoid(gate) * ple
        o_ref[rows, :] = _rms(x, nfin_ref[...]) if final else x


def _tail(x, merged, p, wo, nf, wg, wu, wd, npl, wpg, wpp, nfin, *, final, tm):
    m, d = x.shape
    row = lambda w: pl.BlockSpec((tm, w), lambda i: (i, 0))
    weights = [wo, nf, wg, wu, wd, npl, wpg, wpp, nfin]
    return pl.pallas_call(
        functools.partial(_tail_kernel, final=final),
        grid=(m // tm,),
        in_specs=[row(d)] * 2 + [row(p.shape[1])] + [_const_spec(w.shape) for w in weights],
        out_specs=row(d),
        out_shape=jax.ShapeDtypeStruct((m, d), F32),
        compiler_params=pltpu.CompilerParams(
            dimension_semantics=("arbitrary",), vmem_limit_bytes=VMEM_LIMIT),
        name="tail",
    )(x, merged, p, *weights)


CAST_IN_FFN = ("w_in",)
CAST_IN_MIX = ("w_out", "ffn2_gate", "ffn2_up", "ffn2_down", "ple_gate")


def _layer(x, x1, p, s0, sh0, cv0, w, *, decode, batch, seq, final):
    d = x.shape[1]
    if not decode:
        x1, cast_out = _ffn(x, w["norm_ffn1"], w["ffn1_gate"], w["ffn1_up"], w["ffn1_down"],
                            tm=512, cast=[w[name] for name in CAST_IN_FFN])
        w = dict(w, **dict(zip(CAST_IN_FFN, cast_out)))
    if decode:
        sh_in, c0a, c0b = sh0, cv0[:, 0], cv0[:, 1]
    else:
        sh_in, c0a, c0b = sh0[:, None], cv0[:, 0:1], cv0[:, 1:2]
    r, k, v, al, ld, gg, gyb, sh_new, cv_new, *cast_out = _mix_in(
        x1, w["norm_mix"], w["w_in"], w["mu"], sh_in, c0a, c0b, w["wdec"], w["wicl"],
        w["wgat"], w["decay_bias"], w["iclr_bias"], w["conv_w"],
        decode=decode, batch=batch, seq=seq, tm=min(256, x.shape[0]),
        cast=() if decode else [w[name] for name in CAST_IN_MIX])
    w = dict(w, **dict(zip(CAST_IN_MIX, cast_out)))
    pars = [w["k_k"], w["k_a"], w["r_k"], w["gn_w"], w["gn_b"]]
    if decode:
        merged, s_new = _wkv_step(r, k, v, al, ld, gg, gyb, *[a.reshape(d, 1) for a in pars],
                                  jnp.transpose(s0, (1, 2, 3, 0)), hb=2)
        s_new = jnp.transpose(s_new, (3, 0, 1, 2))
        cv_new = jnp.stack([cv0[:, 1], cv_new], axis=1)
    else:
        merged, s_new = _wkv(r, k, v, al, ld, gg, gyb, *pars, batch=batch, seq=seq)
        sh_new = sh_new[:, 0]
    x_out = _tail(x1, merged, p, w["w_out"], w["norm_ffn2"], w["ffn2_gate"],
                  w["ffn2_up"], w["ffn2_down"], w["norm_ple"], w["ple_gate"], w["ple_proj"],
                  w["norm_final"], final=final, tm=min(512, x.shape[0]))
    return x_out, s_new, sh_new, cv_new, w


def kernel(x_prompt, x_sample, p_prompt, p_sample, state_wkv, state_shift, state_conv,
           norm_ffn1, w_ffn1_gate, w_ffn1_up, w_ffn1_down, norm_mix, w_in, mu_shift,
           w_decay_up, decay_bias, w_iclr_up, iclr_bias, w_gate_up, k_k, k_a, r_k, gn_w,
           gn_b, conv_w, w_out, norm_ffn2, w_ffn2_gate, w_ffn2_up, w_ffn2_down, norm_ple,
           w_ple_gate, w_ple_proj, norm_final):
    depth = w_in.shape[0]
    bp, tp, d = x_prompt.shape
    bs, ts, _ = x_sample.shape
    assert ts == 1 and tp % (WKV_CHUNKS_PER_STEP * CHUNK) == 0
    shift_cols = mu_shift.shape[1]
    n_dec, n_icl, n_gat = w_decay_up.shape[1], w_iclr_up.shape[1], w_gate_up.shape[1]
    assert 3 * d + n_dec + n_icl + n_gat == shift_cols

    xp = x_prompt.reshape(bp * tp, d)
    xs = x_sample.reshape(bs, d)
    vec = lambda a: a.reshape(1, -1)
    outs = [[] for _ in range(6)]
    for i in range(depth):
        n_tail = shift_cols - 3 * d
        pad = lambda a, lo: jnp.zeros((n_tail, d), BF16).at[lo:lo + a.shape[0]].set(a.astype(BF16))
        xs1, ffn1_gate, ffn1_up, ffn1_down = _ffn_stream(
            xs, vec(norm_ffn1[i]), w_ffn1_gate[i], w_ffn1_up[i], w_ffn1_down[i])
        w = dict(
            norm_ffn1=vec(norm_ffn1[i]), ffn1_gate=ffn1_gate, ffn1_up=ffn1_up,
            ffn1_down=ffn1_down,
            norm_mix=vec(norm_mix[i]), w_in=w_in[i], mu=vec(mu_shift[i]),
            wdec=pad(w_decay_up[i], 0), wicl=pad(w_iclr_up[i], n_dec),
            wgat=pad(w_gate_up[i], n_dec + n_icl),
            decay_bias=vec(decay_bias[i]), iclr_bias=vec(iclr_bias[i]), conv_w=conv_w[i],
            k_k=vec(k_k[i]), k_a=vec(k_a[i]), r_k=vec(r_k[i]), gn_w=vec(gn_w[i]),
            gn_b=vec(gn_b[i]), w_out=w_out[i], norm_ffn2=vec(norm_ffn2[i]),
            ffn2_gate=w_ffn2_gate[i], ffn2_up=w_ffn2_up[i],
            ffn2_down=w_ffn2_down[i], norm_ple=vec(norm_ple[i]),
            ple_gate=w_ple_gate[i], ple_proj=w_ple_proj[i].astype(BF16),
            norm_final=vec(norm_final))
        final = i == depth - 1
        xp, s_p, sh_p, cv_p, w = _layer(
            xp, None, p_prompt[i].reshape(bp * tp, -1), None,
            jnp.zeros((bp, shift_cols), F32), jnp.zeros((bp, 2, d), F32), w,
            decode=False, batch=bp, seq=tp, final=final)
        xs, s_s, sh_s, cv_s, _ = _layer(
            xs, xs1, p_sample[i].reshape(bs, -1), state_wkv[i], state_shift[i], state_conv[i], w,
            decode=True, batch=bs, seq=1, final=final)
        for lst, val in zip(outs, (s_p, sh_p, cv_p, s_s, sh_s, cv_s)):
            lst.append(val)
    stk = [jnp.stack(o, 0) for o in outs]
    return (xp.reshape(bp, tp, d), xs.reshape(bs, ts, d), stk[0], stk[1], stk[2],
            stk[3], stk[4], stk[5])
```

```python
import functools

import jax
import jax.numpy as jnp
from jax import lax
from jax.experimental import pallas as pl
from jax.experimental.pallas import tpu as pltpu

F32 = jnp.float32
BF16 = jnp.bfloat16

HEAD = 64
LANES = 128
CHUNK = 64
MXU_DIM = 256
MIX_COL_BLOCK = MXU_DIM
WKV_CHUNKS_PER_STEP = 4
WKV_CHUNKS_PER_GROUP = 2
WKV_STAGE_LAG = 7
RMS_EPS = 1e-6
GN_EPS = 64e-5
NORM_EPS = 1e-12
VMEM_LIMIT = 56 * 1024 * 1024


def _rms(x, g):
    return x * lax.rsqrt(jnp.mean(x * x, axis=-1, keepdims=True) + RMS_EPS) * g


def _sigmoid(x):
    return 1.0 / (1.0 + jnp.exp(-x))


def _softplus(x):
    return jnp.maximum(x, 0.0) + jnp.log(1.0 + jnp.exp(-jnp.abs(x)))


def _bdot(a, b):
    return jnp.dot(a.astype(BF16), b, preferred_element_type=F32)


def _const_spec(shape):
    zeros = (0,) * len(shape)
    return pl.BlockSpec(shape, lambda *_: zeros, pipeline_mode=pl.Buffered(1))


def _cast_plan(arrays, n_steps, step_of):
    specs, shapes = [], []
    for a in arrays:
        rows, cols = a.shape
        blk = next(b for b in range(16, rows + 1, 16) if rows % b == 0 and rows // b <= n_steps)
        last = rows // blk - 1
        specs.append(pl.BlockSpec(
            (blk, cols), lambda *ids, last=last: (jnp.minimum(step_of(*ids), last), 0)))
        shapes.append(jax.ShapeDtypeStruct(a.shape, BF16))
    return specs, shapes


def _cast_blocks(srcs, dsts):
    for src, dst in zip(srcs, dsts):
        dst[...] = src[...].astype(BF16)


def _row_parts(tm):
    sub = tm // 2 if tm >= 2 * MXU_DIM else tm
    return [slice(s, s + sub) for s in range(0, tm, sub)]


def _swiglu_half(xs, g, wg_ref, wu_ref, wd_ref):
    hs = [_rms(x, g).astype(BF16) for x in xs]
    gates = [jnp.dot(h, wg_ref[...], preferred_element_type=F32) for h in hs]
    ups = [jnp.dot(h, wu_ref[...], preferred_element_type=F32) for h in hs]
    acts = [(gate * _sigmoid(gate) * up).astype(BF16) for gate, up in zip(gates, ups)]
    return [x + 0.5 * jnp.dot(act, wd_ref[...], preferred_element_type=F32)
            for x, act in zip(xs, acts)]


def _ffn_kernel(x_ref, g_ref, wg_ref, wu_ref, wd_ref, *rest, n_cast):
    cast_srcs, o_ref, cast_dsts = rest[:n_cast], rest[n_cast], rest[n_cast + 1:]
    _cast_blocks(cast_srcs, cast_dsts)
    parts = _row_parts(x_ref.shape[0])
    outs = _swiglu_half([x_ref[rows, :] for rows in parts], g_ref[...], wg_ref, wu_ref, wd_ref)
    for rows, out in zip(parts, outs):
        o_ref[rows, :] = out


def _ffn(x, g, wg, wu, wd, tm, cast=()):
    m, d = x.shape
    row = pl.BlockSpec((tm, d), lambda i: (i, 0))
    cast_specs, cast_shapes = _cast_plan(cast, m // tm, lambda i: i)
    out = pl.pallas_call(
        functools.partial(_ffn_kernel, n_cast=len(cast)),
        grid=(m // tm,),
        in_specs=[row, _const_spec(g.shape), _const_spec(wg.shape),
                  _const_spec(wu.shape), _const_spec(wd.shape)] + cast_specs,
        out_specs=[row] + cast_specs,
        out_shape=[jax.ShapeDtypeStruct((m, d), F32)] + cast_shapes,
        compiler_params=pltpu.CompilerParams(
            dimension_semantics=("arbitrary",), vmem_limit_bytes=VMEM_LIMIT),
        name="ffn",
    )(x, g, wg, wu, wd, *cast)
    return out[0], out[1:]


def _ffn_stream_kernel(x_ref, g_ref, wg_ref, wu_ref, wd_ref, o_ref, wgb_ref, wub_ref, wdb_ref,
                       h_ref, acc_ref):
    j = pl.program_id(0)

    @pl.when(j == 0)
    def _():
        h_ref[...] = _rms(x_ref[...], g_ref[...]).astype(BF16)
        acc_ref[...] = jnp.zeros_like(acc_ref)

    wg = wg_ref[...].astype(BF16)
    wu = wu_ref[...].astype(BF16)
    wd = wd_ref[...].astype(BF16)
    wgb_ref[...] = wg
    wub_ref[...] = wu
    wdb_ref[...] = wd
    h = h_ref[...]
    gate = jnp.dot(h, wg, preferred_element_type=F32)
    up = jnp.dot(h, wu, preferred_element_type=F32)
    acc_ref[...] += jnp.dot((gate * _sigmoid(gate) * up).astype(BF16), wd,
                            preferred_element_type=F32)

    @pl.when(j == pl.num_programs(0) - 1)
    def _():
        o_ref[...] = x_ref[...] + 0.5 * acc_ref[...]


def _ffn_stream(x, g, wg, wu, wd):
    m, d = x.shape
    d_ff = wg.shape[1]
    ck = MXU_DIM
    cols = pl.BlockSpec((d, ck), lambda j: (0, j))
    rows = pl.BlockSpec((ck, d), lambda j: (j, 0))
    whole = pl.BlockSpec((m, d), lambda j: (0, 0))
    return pl.pallas_call(
        _ffn_stream_kernel,
        grid=(d_ff // ck,),
        in_specs=[whole, _const_spec(g.shape), cols, cols, rows],
        out_specs=[whole, cols, cols, rows],
        out_shape=[jax.ShapeDtypeStruct((m, d), F32), jax.ShapeDtypeStruct(wg.shape, BF16),
                   jax.ShapeDtypeStruct(wu.shape, BF16), jax.ShapeDtypeStruct(wd.shape, BF16)],
        scratch_shapes=[pltpu.VMEM((m, d), BF16), pltpu.VMEM((m, d), F32)],
        compiler_params=pltpu.CompilerParams(
            dimension_semantics=("arbitrary",), vmem_limit_bytes=VMEM_LIMIT),
        name="ffn_stream",
    )(x, g, wg, wu, wd)


def _mix_in_kernel(*refs, decode, d, shift_cols, blk, n_cast):
    n = n_cast
    (x_ref, nm_ref, w_in_ref, mu_ref, sh0_ref, c0a_ref, c0b_ref,
     wdec_ref, wicl_ref, wgat_ref, dbias_ref, ibias_ref, cw_ref) = refs[:13]
    cast_srcs = refs[13:13 + n]
    (r_ref, k_ref, v_ref, al_ref, ld_ref, gg_ref, gyb_ref,
     sh_out_ref, cv_out_ref) = refs[13 + n:22 + n]
    cast_dsts = refs[22 + n:22 + 2 * n]
    scratch = refs[22 + 2 * n:]
    tm = x_ref.shape[0]
    h = _rms(x_ref[...], nm_ref[...]).astype(BF16)
    conv0, gate0 = shift_cols, shift_cols + 3 * d

    if not decode:
        carry_sh, carry_cu = scratch

        @pl.when(pl.program_id(1) == 0)
        def _():
            carry_sh[0:1, :] = sh0_ref[0]
            carry_cu[0:1, :] = c0a_ref[0]
            carry_cu[1:2, :] = c0b_ref[0]

        row = lax.broadcasted_iota(jnp.int32, (tm, 1), 0)

    def proj(c0, width):
        return jnp.dot(h, w_in_ref[:, c0:c0 + width], preferred_element_type=F32)

    def shifted(c0, width):
        cols = slice(c0, c0 + width)
        z = proj(c0, width)
        if decode:
            z_prev = sh0_ref[:, cols]
            sh_out_ref[:, cols] = z
        else:
            z_prev = jnp.where(row == 0, carry_sh[0:1, cols], pltpu.roll(z, 1, 0))
            carry_sh[0:1, cols] = z[tm - 1:tm, :]
            sh_out_ref[0, :, cols] = z[tm - 1:tm, :]
        return z + mu_ref[:, cols] * (z_prev - z)

    tail = shifted(3 * d, shift_cols - 3 * d)
    tail_tanh = jnp.tanh(tail).astype(BF16)
    tail_sig = _sigmoid(tail).astype(BF16)
    tail_raw = tail.astype(BF16)
    for c in range(0, d, blk):
        cols = slice(c, c + blk)
        if c == blk:
            _cast_blocks(cast_srcs, cast_dsts)
        r_ref[:, cols] = shifted(c, blk)
        k_ref[:, cols] = shifted(d + c, blk)
        v_ref[:, cols] = shifted(2 * d + c, blk)
        dec = dbias_ref[:, cols] + jnp.dot(tail_tanh, wdec_ref[:, cols],
                                           preferred_element_type=F32)
        ld_ref[:, cols] = -jnp.exp(-_softplus(-dec) - 0.5)
        al_ref[:, cols] = _sigmoid(ibias_ref[:, cols] + jnp.dot(
            tail_raw, wicl_ref[:, cols], preferred_element_type=F32))
        gate = jnp.dot(tail_sig, wgat_ref[:, cols], preferred_element_type=F32)
        gg_ref[:, cols] = gate * _sigmoid(proj(gate0 + c, blk))
        cu = proj(conv0 + d + c, blk) * proj(conv0 + 2 * d + c, blk)
        if decode:
            cu_m2 = c0a_ref[:, cols]
            cu_m1 = c0b_ref[:, cols]
            cv_out_ref[:, cols] = cu
        else:
            cu_m1 = jnp.where(row == 0, carry_cu[1:2, cols], pltpu.roll(cu, 1, 0))
            cu_m2 = jnp.where(row == 0, carry_cu[0:1, cols],
                              jnp.where(row == 1, carry_cu[1:2, cols], pltpu.roll(cu, 2, 0)))
            carry_cu[0:2, cols] = cu[tm - 2:tm, :]
            cv_out_ref[0, :, cols] = cu[tm - 2:tm, :]
        conv = cu_m2 * cw_ref[0:1, cols] + cu_m1 * cw_ref[1:2, cols] + cu * cw_ref[2:3, cols]
        gyb_ref[:, cols] = _sigmoid(proj(gate0 + d + c, blk)) * (proj(conv0 + c, blk) * conv)


def _mix_in(x, nm, w_in, mu, sh0, c0a, c0b, wdec, wicl, wgat, dbias, ibias, cw,
            *, decode, batch, seq, tm, cast=()):
    m, d = x.shape
    shift_cols = mu.shape[1]
    consts = [nm, w_in, mu]
    lora = [wdec, wicl, wgat, dbias, ibias, cw]
    if decode:
        grid = (1,)
        row = lambda w: pl.BlockSpec((m, w), lambda i: (0, 0))
        state_specs = [row(shift_cols), row(d), row(d)]
        state_out = [row(shift_cols), row(d)]
        state_shapes = [jax.ShapeDtypeStruct((m, shift_cols), F32),
                        jax.ShapeDtypeStruct((m, d), F32)]
        scratch = []
        sem = ("arbitrary",)
    else:
        nt = seq // tm
        grid = (batch, nt)
        row = lambda w: pl.BlockSpec((tm, w), lambda b, t: (b * nt + t, 0))
        per_b = lambda r, w: pl.BlockSpec((1, r, w), lambda b, t: (b, 0, 0))
        state_specs = [per_b(1, shift_cols), per_b(1, d), per_b(1, d)]
        state_out = [per_b(1, shift_cols), per_b(2, d)]
        state_shapes = [jax.ShapeDtypeStruct((batch, 1, shift_cols), F32),
                        jax.ShapeDtypeStruct((batch, 2, d), F32)]
        scratch = [pltpu.VMEM((8, shift_cols), F32), pltpu.VMEM((8, d), F32)]
        sem = ("arbitrary", "arbitrary")
    tok = jax.ShapeDtypeStruct((m, d), F32)
    assert not (decode and cast)
    cast_specs, cast_shapes = _cast_plan(
        cast, grid[0] * grid[-1], lambda b, t: b * grid[-1] + t)
    return pl.pallas_call(
        functools.partial(_mix_in_kernel, decode=decode, d=d, shift_cols=shift_cols,
                          blk=MIX_COL_BLOCK, n_cast=len(cast)),
        grid=grid,
        in_specs=([row(d)] + [_const_spec(a.shape) for a in consts] + state_specs
                  + [_const_spec(a.shape) for a in lora] + cast_specs),
        out_specs=[row(d)] * 7 + state_out + cast_specs,
        out_shape=[tok] * 7 + state_shapes + cast_shapes,
        scratch_shapes=scratch,
        compiler_params=pltpu.CompilerParams(
            dimension_semantics=sem, vmem_limit_bytes=VMEM_LIMIT),
        name="mix_in_decode" if decode else "mix_in",
    )(x, *consts, sh0, c0a, c0b, *lora, *cast)


def _head_sum(x, first):
    x1 = jnp.where(first, x, 0.0)
    s1 = jnp.sum(x1, axis=-1, keepdims=True)
    s2 = jnp.sum(x - x1, axis=-1, keepdims=True)
    return jnp.where(first, s1, s2)


def _stack_heads(x):
    x = x.astype(BF16)
    lane = lax.broadcasted_iota(jnp.int32, x.shape, 1) % LANES
    first = lane < HEAD
    zero = jnp.zeros_like(x)
    return jnp.concatenate([jnp.where(first, x, zero), jnp.where(first, zero, x)], axis=0)


def _apply(mat, x):
    return jnp.dot(mat.astype(BF16), _stack_heads(x), preferred_element_type=F32)


def _split3(x):
    hi = x.astype(BF16)
    rest = x - hi.astype(F32)
    mid = rest.astype(BF16)
    return hi, mid, (rest - mid.astype(F32)).astype(BF16)


def _wkv_kernel(r_ref, k_ref, v_ref, al_ref, ld_ref, gg_ref, gyb_ref, kk_ref, ka_ref, rk_ref,
                gw_ref, gb_ref, y_ref, st_ref, h_ref, *, group, lag):
    c = pl.program_id(1)
    n_pairs = h_ref.shape[0]

    @pl.when(c == 0)
    def _():
        h_ref[...] = jnp.zeros_like(h_ref)

    C = CHUNK
    row = lax.broadcasted_iota(jnp.int32, (C, LANES), 0)
    col = lax.broadcasted_iota(jnp.int32, (C, LANES), 1)
    first = col < HEAD
    j = col % HEAD
    strict = j < row
    incl = j <= row
    cs_r = lax.broadcasted_iota(jnp.int32, (C, 3 * C), 0)
    cs_c = lax.broadcasted_iota(jnp.int32, (C, 3 * C), 1) % C
    tri3 = (cs_c <= cs_r).astype(BF16)
    sq_r = lax.broadcasted_iota(jnp.int32, (LANES, LANES), 0)
    sq_c = lax.broadcasted_iota(jnp.int32, (LANES, LANES), 1)
    same_head = (sq_r // HEAD) == (sq_c // HEAD)
    eye = sq_r == sq_c

    def prefix(q, p):
        rows = slice(q * C, (q + 1) * C)
        sl = slice(p * LANES, (p + 1) * LANES)
        ld = ld_ref[rows, sl]
        lp = jnp.dot(tri3, jnp.concatenate(_split3(ld), axis=0), preferred_element_type=F32)
        return dict(p=p, rows=rows, sl=sl, ld=ld, lp=lp)

    def scores(st):
        rows, sl, ld, lp = st["rows"], st["sl"], st["ld"], st["lp"]
        r = r_ref[rows, sl]
        k = k_ref[rows, sl]
        v = v_ref[rows, sl]
        al = al_ref[rows, sl]
        lp_end = lp[C - 1:C, :]
        kkr = k * kk_ref[:, sl]
        nrm = jnp.sqrt(_head_sum(kkr * kkr, first))
        kk = kkr / jnp.maximum(nrm, NORM_EPS)
        kmod = k * (1.0 + (al - 1.0) * ka_ref[:, sl])
        bv = kk * al
        e_neg = jnp.exp(-lp)
        e_end = jnp.exp(lp_end - lp)
        rt = r * jnp.exp(lp)
        st.update(v=v.astype(BF16), at=(kk * jnp.exp(lp - ld)).astype(BF16), rt=rt,
                  bh=bv * e_end, kh=kmod * e_end, p_end=jnp.exp(lp_end),
                  bonus=_head_sum(r * kmod * rk_ref[:, sl], first) * v)
        sc = lax.dot_general(
            jnp.concatenate([st["at"], rt.astype(BF16)], axis=0),
            jnp.concatenate([_stack_heads(bv * e_neg), _stack_heads(kmod * e_neg)], axis=0),
            (((1,), (1,)), ((), ())), preferred_element_type=F32)
        lab = jnp.where(strict, sc[0:C, 0:LANES], 0.0)
        st["lak_mrk"] = jnp.concatenate(
            [jnp.where(strict, sc[0:C, LANES:2 * LANES], 0.0).astype(BF16),
             jnp.where(incl, sc[C:2 * C, LANES:2 * LANES], 0.0).astype(BF16)], axis=0)
        st["mrb"] = jnp.where(incl, sc[C:2 * C, 0:LANES], 0.0).astype(BF16)
        st["inv"] = (jnp.where(j == row, 1.0, 0.0)
                     - jnp.where((j // 2) == (row // 2), lab, 0.0))
        lab_b = lab.astype(BF16)
        st["lab2"] = jnp.concatenate([lab_b, lab_b], axis=0)

    tok_r = sq_r % HEAD
    tok_c = sq_c % HEAD

    def double_left(st, s):
        keep = same_head & ((tok_r // (2 * s)) == (tok_c // (2 * s))) & ((tok_r // s) != (tok_c // s))
        st["inv_b"] = st["inv"].astype(BF16)
        st["inv_off"] = jnp.dot(st["inv_b"], jnp.where(keep, st["lab2"], jnp.zeros_like(st["lab2"])),
                                preferred_element_type=F32)

    def double_right(st):
        inv2 = jnp.concatenate([st["inv_b"], st["inv_b"]], axis=0)
        st["inv"] = st["inv"] - jnp.dot(st["inv_off"].astype(BF16),
                                        jnp.where(same_head, inv2, jnp.zeros_like(inv2)),
                                        preferred_element_type=F32)

    def values(st):
        xv = _apply(st["lak_mrk"], st["v"])
        st["x_b"] = xv[0:C].astype(BF16)
        st["mrk_v"] = xv[C:2 * C]
        st["lhs_t"] = jnp.concatenate([st["kh"], -st["bh"]], axis=0).T.astype(BF16)

    def solve(st):
        uw = _apply(st["inv"], jnp.concatenate([st["at"], st["x_b"]], axis=1))
        st["uw"] = uw.astype(BF16)

    def outputs(st):
        uw, v = st["uw"], st["v"]
        mu = _apply(st["mrb"], uw)
        st["q"] = (st["rt"] - mu[:, 0:LANES]).astype(BF16)
        st["y0"] = st["mrk_v"] - mu[:, LANES:2 * LANES]
        rhs = jnp.concatenate(
            [jnp.concatenate([v, jnp.zeros_like(v)], axis=1),
             jnp.concatenate([uw[:, LANES:2 * LANES], uw[:, 0:LANES]], axis=1)], axis=0)
        gb = jnp.dot(st["lhs_t"], rhs, preferred_element_type=F32)
        st["g_c"] = jnp.where(same_head, gb[:, 0:LANES], 0.0)
        st["a_c"] = jnp.where(same_head, jnp.where(eye, st["p_end"], 0.0)
                              + gb[:, LANES:2 * LANES], 0.0).astype(BF16)

    def advance(st):
        p = st["p"]
        qa = jnp.dot(jnp.concatenate([st["q"], st["a_c"]], axis=0), h_ref[p].astype(BF16),
                     preferred_element_type=F32)
        st["y"] = qa[0:C] + st["y0"]
        h_ref[p] = qa[C:C + LANES] + st["g_c"]

    def normalize(st):
        sl, y = st["sl"], st["y"]
        mean = _head_sum(y, first) * (1.0 / HEAD)
        yc = y - mean
        var = _head_sum(yc * yc, first) * (1.0 / HEAD)
        yn = yc * lax.rsqrt(var + GN_EPS) * gw_ref[:, sl] + gb_ref[:, sl]
        merged = (yn + st["bonus"]) * gg_ref[st["rows"], sl] + gyb_ref[st["rows"], sl]
        y_ref[st["rows"], sl] = merged.astype(BF16)

    def program(chunks):
        units = []

        def s_prefix():
            units.extend(prefix(q, p) for q in chunks for p in range(n_pairs))

        def each(fn, *args):
            return lambda: [fn(st, *args) for st in units]

        stages = [s_prefix, each(scores), each(values)]
        s = 2
        while s < C:
            stages += [each(double_left, s), each(double_right)]
            s *= 2
        return stages + [each(solve), each(outputs), each(advance), each(normalize)]

    n_chunks = r_ref.shape[0] // C
    programs = [program(range(g, g + group)) for g in range(0, n_chunks, group)]
    for t in range(len(programs[0]) + lag * (len(programs) - 1)):
        for g, stages in enumerate(programs):
            if 0 <= t - g * lag < len(stages):
                stages[t - g * lag]()

    @pl.when(c == pl.num_programs(1) - 1)
    def _():
        for p in range(n_pairs):
            s_pair = h_ref[p].T
            st_ref[0, 2 * p] = s_pair[0:HEAD, 0:HEAD]
            st_ref[0, 2 * p + 1] = pltpu.roll(s_pair, HEAD, 1)[HEAD:2 * HEAD, 0:HEAD]


def _wkv(r, k, v, al, ld, gg, gyb, kk, ka, rk, gw, gb, *, batch, seq):
    m, d = r.shape
    rows = WKV_CHUNKS_PER_STEP * CHUNK
    nc = seq // rows
    n_pairs = d // LANES
    tok = pl.BlockSpec((rows, d), lambda b, c: (b * nc + c, 0))
    par = pl.BlockSpec((1, d), lambda b, c: (0, 0))
    return pl.pallas_call(
        functools.partial(_wkv_kernel, group=WKV_CHUNKS_PER_GROUP, lag=WKV_STAGE_LAG),
        grid=(batch, nc),
        in_specs=[tok] * 7 + [par] * 5,
        out_specs=[tok, pl.BlockSpec((1, 2 * n_pairs, HEAD, HEAD), lambda b, c: (b, 0, 0, 0))],
        out_shape=[jax.ShapeDtypeStruct((m, d), BF16),
                   jax.ShapeDtypeStruct((batch, 2 * n_pairs, HEAD, HEAD), F32)],
        scratch_shapes=[pltpu.VMEM((n_pairs, LANES, LANES), F32)],
        compiler_params=pltpu.CompilerParams(
            dimension_semantics=("arbitrary", "arbitrary"), vmem_limit_bytes=VMEM_LIMIT),
        name="wkv",
    )(r, k, v, al, ld, gg, gyb, kk, ka, rk, gw, gb)


def _wkv_step_kernel(r_ref, k_ref, v_ref, al_ref, ld_ref, gg_ref, gyb_ref, kk_ref, ka_ref,
                     rk_ref, gw_ref, gb_ref, s_ref, y_ref, so_ref):
    hb, _, _, nb = s_ref.shape
    r_t, k_t, v_t, al_t, ld_t = [ref[...].T for ref in (r_ref, k_ref, v_ref, al_ref, ld_ref)]
    diag = (lax.broadcasted_iota(jnp.int32, (HEAD, HEAD, nb), 0)
            == lax.broadcasted_iota(jnp.int32, (HEAD, HEAD, nb), 1))
    ys = []
    for i in range(hb):
        rows = slice(i * HEAD, (i + 1) * HEAD)
        r, k, v, al = r_t[rows], k_t[rows], v_t[rows], al_t[rows]
        decay = jnp.exp(ld_t[rows])
        kkr = k * kk_ref[rows, :]
        nrm = jnp.sqrt(jnp.sum(kkr * kkr, axis=0, keepdims=True))
        kk = kkr / jnp.maximum(nrm, NORM_EPS)
        kmod = k * (1.0 + (al - 1.0) * ka_ref[rows, :])
        bv = kk * al
        s = s_ref[i]
        sa = jnp.sum(s * kk[None], axis=1, keepdims=True)
        v3 = jnp.sum(jnp.where(diag, v[None], 0.0), axis=1, keepdims=True)
        s_new = s * decay[None] - sa * bv[None] + v3 * kmod[None]
        so_ref[i] = s_new
        y3 = jnp.sum(s_new * r[None], axis=1, keepdims=True)
        y = jnp.sum(jnp.where(diag, y3, 0.0), axis=0)
        mean = jnp.mean(y, axis=0, keepdims=True)
        yc = y - mean
        var = jnp.mean(yc * yc, axis=0, keepdims=True)
        yn = yc * lax.rsqrt(var + GN_EPS) * gw_ref[rows, :] + gb_ref[rows, :]
        bonus = jnp.sum(r * kmod * rk_ref[rows, :], axis=0, keepdims=True) * v
        ys.append(yn + bonus)
    ya = jnp.concatenate(ys, axis=0).T
    y_ref[...] = (ya * gg_ref[...] + gyb_ref[...]).astype(BF16)


def _wkv_step(r, k, v, al, ld, gg, gyb, kk, ka, rk, gw, gb, s0, hb):
    nh, _, _, b = s0.shape
    d = r.shape[1]
    tok = pl.BlockSpec((b, hb * HEAD), lambda i: (0, i))
    par = pl.BlockSpec((hb * HEAD, 1), lambda i: (i, 0))
    st = pl.BlockSpec((hb, HEAD, HEAD, b), lambda i: (i, 0, 0, 0))
    return pl.pallas_call(
        _wkv_step_kernel,
        grid=(nh // hb,),
        in_specs=[tok] * 7 + [par] * 5 + [st],
        out_specs=[tok, st],
        out_shape=[jax.ShapeDtypeStruct((b, d), BF16), jax.ShapeDtypeStruct(s0.shape, F32)],
        compiler_params=pltpu.CompilerParams(
            dimension_semantics=("arbitrary",), vmem_limit_bytes=VMEM_LIMIT),
        name="wkv_step",
    )(r, k, v, al, ld, gg, gyb, kk, ka, rk, gw, gb, s0)


def _tail_kernel(x_ref, mg_ref, p_ref, xs_ref, mgs_ref, ps_ref, wo_ref, nf_ref, wg_ref, wu_ref,
                 wd_ref, npl_ref, wpg_ref, wpp_ref, nfin_ref, o_ref, os_ref, *, final):
    def rows_through(x_ref, mg_ref, p_ref, o_ref):
        parts = _row_parts(x_ref.shape[0])
        xs = [x_ref[rows, :] + jnp.dot(mg_ref[rows, :], wo_ref[...], preferred_element_type=F32)
              for rows in parts]
        xs = _swiglu_half(xs, nf_ref[...], wg_ref, wu_ref, wd_ref)
        hps = [_rms(x, npl_ref[...]).astype(BF16) for x in xs]
        ple_gates = [jnp.dot(hp, wpg_ref[...], preferred_element_type=F32) for hp in hps]
        ples = [_bdot(p_ref[rows, :], wpp_ref[...]) for rows in parts]
        for rows, x, gate, ple in zip(parts, xs, ple_gates, ples):
            x = x + _sigmoid(gate) * ple
            o_ref[rows, :] = _rms(x, nfin_ref[...]) if final else x

    last = pl.num_programs(0) - 1

    @pl.when(pl.program_id(0) < last)
    def _():
        rows_through(x_ref, mg_ref, p_ref, o_ref)

    @pl.when(pl.program_id(0) == last)
    def _():
        rows_through(xs_ref, mgs_ref, ps_ref, os_ref)


def _tail(x, merged, p, xs, merged_s, ps, wo, nf, wg, wu, wd, npl, wpg, wpp, nfin, *, final, tm):
    m, d = x.shape
    n = m // tm
    row = lambda w: pl.BlockSpec((tm, w), lambda i: (jnp.minimum(i, n - 1), 0))
    whole = lambda a: pl.BlockSpec(a.shape, lambda i: (0, 0))
    weights = [wo, nf, wg, wu, wd, npl, wpg, wpp, nfin]
    return pl.pallas_call(
        functools.partial(_tail_kernel, final=final),
        grid=(n + 1,),
        in_specs=([row(d)] * 2 + [row(p.shape[1])] + [whole(a) for a in (xs, merged_s, ps)]
                  + [_const_spec(w.shape) for w in weights]),
        out_specs=[row(d), whole(xs)],
        out_shape=[jax.ShapeDtypeStruct((m, d), F32), jax.ShapeDtypeStruct(xs.shape, F32)],
        compiler_params=pltpu.CompilerParams(
            dimension_semantics=("arbitrary",), vmem_limit_bytes=VMEM_LIMIT),
        name="tail",
    )(x, merged, p, xs, merged_s, ps, *weights)


CAST_IN_FFN = ("w_in",)
CAST_IN_MIX = ("w_out", "ffn2_gate", "ffn2_up", "ffn2_down", "ple_gate")


def _mixer(x, x1, s0, sh0, cv0, w, *, decode, batch, seq):
    d = x.shape[1]
    if not decode:
        x1, cast_out = _ffn(x, w["norm_ffn1"], w["ffn1_gate"], w["ffn1_up"], w["ffn1_down"],
                            tm=512, cast=[w[name] for name in CAST_IN_FFN])
        w = dict(w, **dict(zip(CAST_IN_FFN, cast_out)))
    if decode:
        sh_in, c0a, c0b = sh0, cv0[:, 0], cv0[:, 1]
    else:
        sh_in, c0a, c0b = sh0[:, None], cv0[:, 0:1], cv0[:, 1:2]
    r, k, v, al, ld, gg, gyb, sh_new, cv_new, *cast_out = _mix_in(
        x1, w["norm_mix"], w["w_in"], w["mu"], sh_in, c0a, c0b, w["wdec"], w["wicl"],
        w["wgat"], w["decay_bias"], w["iclr_bias"], w["conv_w"],
        decode=decode, batch=batch, seq=seq, tm=min(256, x.shape[0]),
        cast=() if decode else [w[name] for name in CAST_IN_MIX])
    w = dict(w, **dict(zip(CAST_IN_MIX, cast_out)))
    pars = [w["k_k"], w["k_a"], w["r_k"], w["gn_w"], w["gn_b"]]
    if decode:
        merged, s_new = _wkv_step(r, k, v, al, ld, gg, gyb, *[a.reshape(d, 1) for a in pars],
                                  jnp.transpose(s0, (1, 2, 3, 0)), hb=2)
        s_new = jnp.transpose(s_new, (3, 0, 1, 2))
        cv_new = jnp.stack([cv0[:, 1], cv_new], axis=1)
    else:
        merged, s_new = _wkv(r, k, v, al, ld, gg, gyb, *pars, batch=batch, seq=seq)
        sh_new = sh_new[:, 0]
    return x1, merged, s_new, sh_new, cv_new, w


def kernel(x_prompt, x_sample, p_prompt, p_sample, state_wkv, state_shift, state_conv,
           norm_ffn1, w_ffn1_gate, w_ffn1_up, w_ffn1_down, norm_mix, w_in, mu_shift,
           w_decay_up, decay_bias, w_iclr_up, iclr_bias, w_gate_up, k_k, k_a, r_k, gn_w,
           gn_b, conv_w, w_out, norm_ffn2, w_ffn2_gate, w_ffn2_up, w_ffn2_down, norm_ple,
           w_ple_gate, w_ple_proj, norm_final):
    depth = w_in.shape[0]
    bp, tp, d = x_prompt.shape
    bs, ts, _ = x_sample.shape
    assert ts == 1 and tp % (WKV_CHUNKS_PER_STEP * CHUNK) == 0
    shift_cols = mu_shift.shape[1]
    n_dec, n_icl, n_gat = w_decay_up.shape[1], w_iclr_up.shape[1], w_gate_up.shape[1]
    assert 3 * d + n_dec + n_icl + n_gat == shift_cols

    xp = x_prompt.reshape(bp * tp, d)
    xs = x_sample.reshape(bs, d)
    vec = lambda a: a.reshape(1, -1)
    outs = [[] for _ in range(6)]
    for i in range(depth):
        n_tail = shift_cols - 3 * d
        pad = lambda a, lo: jnp.zeros((n_tail, d), BF16).at[lo:lo + a.shape[0]].set(a.astype(BF16))
        xs1, ffn1_gate, ffn1_up, ffn1_down = _ffn_stream(
            xs, vec(norm_ffn1[i]), w_ffn1_gate[i], w_ffn1_up[i], w_ffn1_down[i])
        w = dict(
            norm_ffn1=vec(norm_ffn1[i]), ffn1_gate=ffn1_gate, ffn1_up=ffn1_up,
            ffn1_down=ffn1_down,
            norm_mix=vec(norm_mix[i]), w_in=w_in[i], mu=vec(mu_shift[i]),
            wdec=pad(w_decay_up[i], 0), wicl=pad(w_iclr_up[i], n_dec),
            wgat=pad(w_gate_up[i], n_dec + n_icl),
            decay_bias=vec(decay_bias[i]), iclr_bias=vec(iclr_bias[i]), conv_w=conv_w[i],
            k_k=vec(k_k[i]), k_a=vec(k_a[i]), r_k=vec(r_k[i]), gn_w=vec(gn_w[i]),
            gn_b=vec(gn_b[i]), w_out=w_out[i], norm_ffn2=vec(norm_ffn2[i]),
            ffn2_gate=w_ffn2_gate[i], ffn2_up=w_ffn2_up[i],
            ffn2_down=w_ffn2_down[i], norm_ple=vec(norm_ple[i]),
            ple_gate=w_ple_gate[i], ple_proj=w_ple_proj[i].astype(BF16),
            norm_final=vec(norm_final))
        xp1, merged_p, s_p, sh_p, cv_p, w = _mixer(
            xp, None, None, jnp.zeros((bp, shift_cols), F32), jnp.zeros((bp, 2, d), F32), w,
            decode=False, batch=bp, seq=tp)
        _, merged_s, s_s, sh_s, cv_s, _ = _mixer(
            xs, xs1, state_wkv[i], state_shift[i], state_conv[i], w,
            decode=True, batch=bs, seq=1)
        xp, xs = _tail(xp1, merged_p, p_prompt[i].reshape(bp * tp, -1),
                       xs1, merged_s, p_sample[i].reshape(bs, -1),
                       w["w_out"], w["norm_ffn2"], w["ffn2_gate"], w["ffn2_up"], w["ffn2_down"],
                       w["norm_ple"], w["ple_gate"], w["ple_proj"], w["norm_final"],
                       final=i == depth - 1, tm=512)
        for lst, val in zip(outs, (s_p, sh_p, cv_p, s_s, sh_s, cv_s)):
            lst.append(val)
    stk = [jnp.stack(o, 0) for o in outs]
    return (xp.reshape(bp, tp, d), xs.reshape(bs, ts, d), stk[0], stk[1], stk[2],
            stk[3], stk[4], stk[5])
```

```python
import functools

import jax
import jax.numpy as jnp
from jax import lax
from jax.experimental import pallas as pl
from jax.experimental.pallas import tpu as pltpu

F32 = jnp.float32
BF16 = jnp.bfloat16

HEAD = 64
LANES = 128
CHUNK = 64
MXU_DIM = 256
MIX_COL_BLOCK = MXU_DIM
WKV_CHUNKS_PER_STEP = 8
WKV_CHUNKS_PER_GROUP = 2
WKV_STAGE_LAG = 7
RMS_EPS = 1e-6
GN_EPS = 64e-5
NORM_EPS = 1e-12
VMEM_LIMIT = 56 * 1024 * 1024


def _rms(x, g):
    return x * lax.rsqrt(jnp.mean(x * x, axis=-1, keepdims=True) + RMS_EPS) * g


def _sigmoid(x):
    return 1.0 / (1.0 + jnp.exp(-x))


def _softplus(x):
    return jnp.maximum(x, 0.0) + jnp.log(1.0 + jnp.exp(-jnp.abs(x)))


def _bdot(a, b):
    return jnp.dot(a.astype(BF16), b, preferred_element_type=F32)


def _const_spec(shape):
    zeros = (0,) * len(shape)
    return pl.BlockSpec(shape, lambda *_: zeros, pipeline_mode=pl.Buffered(1))


def _cast_plan(arrays, n_steps, step_of):
    specs, shapes = [], []
    for a in arrays:
        rows, cols = a.shape
        blk = next(b for b in range(16, rows + 1, 16) if rows % b == 0 and rows // b <= n_steps)
        last = rows // blk - 1
        specs.append(pl.BlockSpec(
            (blk, cols), lambda *ids, last=last: (jnp.minimum(step_of(*ids), last), 0)))
        shapes.append(jax.ShapeDtypeStruct(a.shape, BF16))
    return specs, shapes


def _cast_blocks(srcs, dsts):
    for src, dst in zip(srcs, dsts):
        dst[...] = src[...].astype(BF16)


def _row_parts(tm):
    sub = tm // 2 if tm >= 2 * MXU_DIM else tm
    return [slice(s, s + sub) for s in range(0, tm, sub)]


def _swiglu_half(xs, g, wg_ref, wu_ref, wd_ref):
    hs = [_rms(x, g).astype(BF16) for x in xs]
    gates = [jnp.dot(h, wg_ref[...], preferred_element_type=F32) for h in hs]
    ups = [jnp.dot(h, wu_ref[...], preferred_element_type=F32) for h in hs]
    acts = [(gate * _sigmoid(gate) * up).astype(BF16) for gate, up in zip(gates, ups)]
    return [x + 0.5 * jnp.dot(act, wd_ref[...], preferred_element_type=F32)
            for x, act in zip(xs, acts)]


def _ffn_kernel(x_ref, g_ref, wg_ref, wu_ref, wd_ref, *rest, n_cast):
    cast_srcs, o_ref, cast_dsts = rest[:n_cast], rest[n_cast], rest[n_cast + 1:]
    _cast_blocks(cast_srcs, cast_dsts)
    parts = _row_parts(x_ref.shape[0])
    outs = _swiglu_half([x_ref[rows, :] for rows in parts], g_ref[...], wg_ref, wu_ref, wd_ref)
    for rows, out in zip(parts, outs):
        o_ref[rows, :] = out


def _ffn(x, g, wg, wu, wd, tm, cast=()):
    m, d = x.shape
    row = pl.BlockSpec((tm, d), lambda i: (i, 0))
    cast_specs, cast_shapes = _cast_plan(cast, m // tm, lambda i: i)
    out = pl.pallas_call(
        functools.partial(_ffn_kernel, n_cast=len(cast)),
        grid=(m // tm,),
        in_specs=[row, _const_spec(g.shape), _const_spec(wg.shape),
                  _const_spec(wu.shape), _const_spec(wd.shape)] + cast_specs,
        out_specs=[row] + cast_specs,
        out_shape=[jax.ShapeDtypeStruct((m, d), F32)] + cast_shapes,
        compiler_params=pltpu.CompilerParams(
            dimension_semantics=("arbitrary",), vmem_limit_bytes=VMEM_LIMIT),
        name="ffn",
    )(x, g, wg, wu, wd, *cast)
    return out[0], out[1:]


def _ffn_stream_kernel(x_ref, g_ref, wg_ref, wu_ref, wd_ref, o_ref, wgb_ref, wub_ref, wdb_ref,
                       h_ref, acc_ref):
    j = pl.program_id(0)

    @pl.when(j == 0)
    def _():
        h_ref[...] = _rms(x_ref[...], g_ref[...]).astype(BF16)
        acc_ref[...] = jnp.zeros_like(acc_ref)

    wg = wg_ref[...].astype(BF16)
    wu = wu_ref[...].astype(BF16)
    wd = wd_ref[...].astype(BF16)
    wgb_ref[...] = wg
    wub_ref[...] = wu
    wdb_ref[...] = wd
    h = h_ref[...]
    gate = jnp.dot(h, wg, preferred_element_type=F32)
    up = jnp.dot(h, wu, preferred_element_type=F32)
    acc_ref[...] += jnp.dot((gate * _sigmoid(gate) * up).astype(BF16), wd,
                            preferred_element_type=F32)

    @pl.when(j == pl.num_programs(0) - 1)
    def _():
        o_ref[...] = x_ref[...] + 0.5 * acc_ref[...]


def _ffn_stream(x, g, wg, wu, wd):
    m, d = x.shape
    d_ff = wg.shape[1]
    ck = MXU_DIM
    cols = pl.BlockSpec((d, ck), lambda j: (0, j))
    rows = pl.BlockSpec((ck, d), lambda j: (j, 0))
    whole = pl.BlockSpec((m, d), lambda j: (0, 0))
    return pl.pallas_call(
        _ffn_stream_kernel,
        grid=(d_ff // ck,),
        in_specs=[whole, _const_spec(g.shape), cols, cols, rows],
        out_specs=[whole, cols, cols, rows],
        out_shape=[jax.ShapeDtypeStruct((m, d), F32), jax.ShapeDtypeStruct(wg.shape, BF16),
                   jax.ShapeDtypeStruct(wu.shape, BF16), jax.ShapeDtypeStruct(wd.shape, BF16)],
        scratch_shapes=[pltpu.VMEM((m, d), BF16), pltpu.VMEM((m, d), F32)],
        compiler_params=pltpu.CompilerParams(
            dimension_semantics=("arbitrary",), vmem_limit_bytes=VMEM_LIMIT),
        name="ffn_stream",
    )(x, g, wg, wu, wd)


def _mix_in_kernel(*refs, decode, d, shift_cols, blk, n_cast):
    n = n_cast
    (x_ref, nm_ref, w_in_ref, mu_ref, sh0_ref, c0a_ref, c0b_ref,
     wdec_ref, wicl_ref, wgat_ref, dbias_ref, ibias_ref, cw_ref) = refs[:13]
    cast_srcs = refs[13:13 + n]
    (r_ref, k_ref, v_ref, al_ref, ld_ref, gg_ref, gyb_ref,
     sh_out_ref, cv_out_ref) = refs[13 + n:22 + n]
    cast_dsts = refs[22 + n:22 + 2 * n]
    scratch = refs[22 + 2 * n:]
    tm = x_ref.shape[0]
    h = _rms(x_ref[...], nm_ref[...]).astype(BF16)
    conv0, gate0 = shift_cols, shift_cols + 3 * d

    if not decode:
        carry_sh, carry_cu = scratch

        @pl.when(pl.program_id(1) == 0)
        def _():
            carry_sh[0:1, :] = sh0_ref[0]
            carry_cu[0:1, :] = c0a_ref[0]
            carry_cu[1:2, :] = c0b_ref[0]

        row = lax.broadcasted_iota(jnp.int32, (tm, 1), 0)

    def proj(c0, width):
        return jnp.dot(h, w_in_ref[:, c0:c0 + width], preferred_element_type=F32)

    def shifted(c0, width):
        cols = slice(c0, c0 + width)
        z = proj(c0, width)
        if decode:
            z_prev = sh0_ref[:, cols]
            sh_out_ref[:, cols] = z
        else:
            z_prev = jnp.where(row == 0, carry_sh[0:1, cols], pltpu.roll(z, 1, 0))
            carry_sh[0:1, cols] = z[tm - 1:tm, :]
            sh_out_ref[0, :, cols] = z[tm - 1:tm, :]
        return z + mu_ref[:, cols] * (z_prev - z)

    tail = shifted(3 * d, shift_cols - 3 * d)
    tail_tanh = jnp.tanh(tail).astype(BF16)
    tail_sig = _sigmoid(tail).astype(BF16)
    tail_raw = tail.astype(BF16)
    for c in range(0, d, blk):
        cols = slice(c, c + blk)
        if c == blk:
            _cast_blocks(cast_srcs, cast_dsts)
        r_ref[:, cols] = shifted(c, blk)
        k_ref[:, cols] = shifted(d + c, blk)
        v_ref[:, cols] = shifted(2 * d + c, blk)
        dec = dbias_ref[:, cols] + jnp.dot(tail_tanh, wdec_ref[:, cols],
                                           preferred_element_type=F32)
        ld_ref[:, cols] = -jnp.exp(-_softplus(-dec) - 0.5)
        al_ref[:, cols] = _sigmoid(ibias_ref[:, cols] + jnp.dot(
            tail_raw, wicl_ref[:, cols], preferred_element_type=F32))
        gate = jnp.dot(tail_sig, wgat_ref[:, cols], preferred_element_type=F32)
        gg_ref[:, cols] = gate * _sigmoid(proj(gate0 + c, blk))
        cu = proj(conv0 + d + c, blk) * proj(conv0 + 2 * d + c, blk)
        if decode:
            cu_m2 = c0a_ref[:, cols]
            cu_m1 = c0b_ref[:, cols]
            cv_out_ref[:, cols] = cu
        else:
            cu_m1 = jnp.where(row == 0, carry_cu[1:2, cols], pltpu.roll(cu, 1, 0))
            cu_m2 = jnp.where(row == 0, carry_cu[0:1, cols],
                              jnp.where(row == 1, carry_cu[1:2, cols], pltpu.roll(cu, 2, 0)))
            carry_cu[0:2, cols] = cu[tm - 2:tm, :]
            cv_out_ref[0, :, cols] = cu[tm - 2:tm, :]
        conv = cu_m2 * cw_ref[0:1, cols] + cu_m1 * cw_ref[1:2, cols] + cu * cw_ref[2:3, cols]
        gyb_ref[:, cols] = _sigmoid(proj(gate0 + d + c, blk)) * (proj(conv0 + c, blk) * conv)


def _mix_in(x, nm, w_in, mu, sh0, c0a, c0b, wdec, wicl, wgat, dbias, ibias, cw,
            *, decode, batch, seq, tm, cast=()):
    m, d = x.shape
    shift_cols = mu.shape[1]
    consts = [nm, w_in, mu]
    lora = [wdec, wicl, wgat, dbias, ibias, cw]
    if decode:
        grid = (1,)
        row = lambda w: pl.BlockSpec((m, w), lambda i: (0, 0))
        state_specs = [row(shift_cols), row(d), row(d)]
        state_out = [row(shift_cols), row(d)]
        state_shapes = [jax.ShapeDtypeStruct((m, shift_cols), F32),
                        jax.ShapeDtypeStruct((m, d), F32)]
        scratch = []
        sem = ("arbitrary",)
    else:
        nt = seq // tm
        grid = (batch, nt)
        row = lambda w: pl.BlockSpec((tm, w), lambda b, t: (b * nt + t, 0))
        per_b = lambda r, w: pl.BlockSpec((1, r, w), lambda b, t: (b, 0, 0))
        state_specs = [per_b(1, shift_cols), per_b(1, d), per_b(1, d)]
        state_out = [per_b(1, shift_cols), per_b(2, d)]
        state_shapes = [jax.ShapeDtypeStruct((batch, 1, shift_cols), F32),
                        jax.ShapeDtypeStruct((batch, 2, d), F32)]
        scratch = [pltpu.VMEM((8, shift_cols), F32), pltpu.VMEM((8, d), F32)]
        sem = ("arbitrary", "arbitrary")
    tok = jax.ShapeDtypeStruct((m, d), F32)
    assert not (decode and cast)
    cast_specs, cast_shapes = _cast_plan(
        cast, grid[0] * grid[-1], lambda b, t: b * grid[-1] + t)
    return pl.pallas_call(
        functools.partial(_mix_in_kernel, decode=decode, d=d, shift_cols=shift_cols,
                          blk=MIX_COL_BLOCK, n_cast=len(cast)),
        grid=grid,
        in_specs=([row(d)] + [_const_spec(a.shape) for a in consts] + state_specs
                  + [_const_spec(a.shape) for a in lora] + cast_specs),
        out_specs=[row(d)] * 7 + state_out + cast_specs,
        out_shape=[tok] * 7 + state_shapes + cast_shapes,
        scratch_shapes=scratch,
        compiler_params=pltpu.CompilerParams(
            dimension_semantics=sem, vmem_limit_bytes=VMEM_LIMIT),
        name="mix_in_decode" if decode else "mix_in",
    )(x, *consts, sh0, c0a, c0b, *lora, *cast)


def _head_sum(x, first):
    x1 = jnp.where(first, x, 0.0)
    s1 = jnp.sum(x1, axis=-1, keepdims=True)
    s2 = jnp.sum(x - x1, axis=-1, keepdims=True)
    return jnp.where(first, s1, s2)


def _stack_heads(x):
    x = x.astype(BF16)
    lane = lax.broadcasted_iota(jnp.int32, x.shape, 1) % LANES
    first = lane < HEAD
    zero = jnp.zeros_like(x)
    return jnp.concatenate([jnp.where(first, x, zero), jnp.where(first, zero, x)], axis=0)


def _apply(mat, x):
    return jnp.dot(mat.astype(BF16), _stack_heads(x), preferred_element_type=F32)


def _split3(x):
    hi = x.astype(BF16)
    rest = x - hi.astype(F32)
    mid = rest.astype(BF16)
    return hi, mid, (rest - mid.astype(F32)).astype(BF16)


def _wkv_kernel(r_ref, k_ref, v_ref, al_ref, ld_ref, gg_ref, gyb_ref, kk_ref, ka_ref, rk_ref,
                gw_ref, gb_ref, y_ref, st_ref, h_ref, *, group, lag):
    c = pl.program_id(1)
    n_pairs = h_ref.shape[0]

    @pl.when(c == 0)
    def _():
        h_ref[...] = jnp.zeros_like(h_ref)

    C = CHUNK
    row = lax.broadcasted_iota(jnp.int32, (C, LANES), 0)
    col = lax.broadcasted_iota(jnp.int32, (C, LANES), 1)
    first = col < HEAD
    j = col % HEAD
    strict = j < row
    incl = j <= row
    cs_r = lax.broadcasted_iota(jnp.int32, (C, 3 * C), 0)
    cs_c = lax.broadcasted_iota(jnp.int32, (C, 3 * C), 1) % C
    tri3 = (cs_c <= cs_r).astype(BF16)
    sq_r = lax.broadcasted_iota(jnp.int32, (LANES, LANES), 0)
    sq_c = lax.broadcasted_iota(jnp.int32, (LANES, LANES), 1)
    same_head = (sq_r // HEAD) == (sq_c // HEAD)
    eye = sq_r == sq_c

    def prefix(q, p):
        rows = slice(q * C, (q + 1) * C)
        sl = slice(p * LANES, (p + 1) * LANES)
        ld = ld_ref[rows, sl]
        lp = jnp.dot(tri3, jnp.concatenate(_split3(ld), axis=0), preferred_element_type=F32)
        return dict(p=p, rows=rows, sl=sl, ld=ld, lp=lp)

    def scores(st):
        rows, sl, ld, lp = st["rows"], st["sl"], st["ld"], st["lp"]
        r = r_ref[rows, sl]
        k = k_ref[rows, sl]
        v = v_ref[rows, sl]
        al = al_ref[rows, sl]
        lp_end = lp[C - 1:C, :]
        kkr = k * kk_ref[:, sl]
        nrm = jnp.sqrt(_head_sum(kkr * kkr, first))
        kk = kkr / jnp.maximum(nrm, NORM_EPS)
        kmod = k * (1.0 + (al - 1.0) * ka_ref[:, sl])
        bv = kk * al
        e_neg = jnp.exp(-lp)
        e_end = jnp.exp(lp_end - lp)
        rt = r * jnp.exp(lp)
        st.update(v=v.astype(BF16), at=(kk * jnp.exp(lp - ld)).astype(BF16), rt=rt,
                  bh=bv * e_end, kh=kmod * e_end, p_end=jnp.exp(lp_end),
                  bonus=_head_sum(r * kmod * rk_ref[:, sl], first) * v)
        sc = lax.dot_general(
            jnp.concatenate([st["at"], rt.astype(BF16)], axis=0),
            jnp.concatenate([_stack_heads(bv * e_neg), _stack_heads(kmod * e_neg)], axis=0),
            (((1,), (1,)), ((), ())), preferred_element_type=F32)
        lab = jnp.where(strict, sc[0:C, 0:LANES], 0.0)
        st["lak_mrk"] = jnp.concatenate(
            [jnp.where(strict, sc[0:C, LANES:2 * LANES], 0.0).astype(BF16),
             jnp.where(incl, sc[C:2 * C, LANES:2 * LANES], 0.0).astype(BF16)], axis=0)
        st["mrb"] = jnp.where(incl, sc[C:2 * C, 0:LANES], 0.0).astype(BF16)
        st["inv"] = (jnp.where(j == row, 1.0, 0.0)
                     - jnp.where((j // 2) == (row // 2), lab, 0.0))
        lab_b = lab.astype(BF16)
        st["lab2"] = jnp.concatenate([lab_b, lab_b], axis=0)

    tok_r = sq_r % HEAD
    tok_c = sq_c % HEAD

    def double_left(st, s):
        keep = same_head & ((tok_r // (2 * s)) == (tok_c // (2 * s))) & ((tok_r // s) != (tok_c // s))
        st["inv_b"] = st["inv"].astype(BF16)
        st["inv_off"] = jnp.dot(st["inv_b"], jnp.where(keep, st["lab2"], jnp.zeros_like(st["lab2"])),
                                preferred_element_type=F32)

    def double_right(st):
        inv2 = jnp.concatenate([st["inv_b"], st["inv_b"]], axis=0)
        st["inv"] = st["inv"] - jnp.dot(st["inv_off"].astype(BF16),
                                        jnp.where(same_head, inv2, jnp.zeros_like(inv2)),
                                        preferred_element_type=F32)

    def values(st):
        xv = _apply(st["lak_mrk"], st["v"])
        st["x_b"] = xv[0:C].astype(BF16)
        st["mrk_v"] = xv[C:2 * C]
        st["lhs_t"] = jnp.concatenate([st["kh"], -st["bh"]], axis=0).T.astype(BF16)

    def solve(st):
        uw = _apply(st["inv"], jnp.concatenate([st["at"], st["x_b"]], axis=1))
        st["uw"] = uw.astype(BF16)

    def outputs(st):
        uw, v = st["uw"], st["v"]
        mu = _apply(st["mrb"], uw)
        st["q"] = (st["rt"] - mu[:, 0:LANES]).astype(BF16)
        st["y0"] = st["mrk_v"] - mu[:, LANES:2 * LANES]
        rhs = jnp.concatenate(
            [jnp.concatenate([v, jnp.zeros_like(v)], axis=1),
             jnp.concatenate([uw[:, LANES:2 * LANES], uw[:, 0:LANES]], axis=1)], axis=0)
        gb = jnp.dot(st["lhs_t"], rhs, preferred_element_type=F32)
        st["g_c"] = jnp.where(same_head, gb[:, 0:LANES], 0.0)
        st["a_c"] = jnp.where(same_head, jnp.where(eye, st["p_end"], 0.0)
                              + gb[:, LANES:2 * LANES], 0.0).astype(BF16)

    def advance(st):
        p = st["p"]
        qa = jnp.dot(jnp.concatenate([st["q"], st["a_c"]], axis=0), h_ref[p].astype(BF16),
                     preferred_element_type=F32)
        st["y"] = qa[0:C] + st["y0"]
        h_ref[p] = qa[C:C + LANES] + st["g_c"]

    def normalize(st):
        sl, y = st["sl"], st["y"]
        mean = _head_sum(y, first) * (1.0 / HEAD)
        yc = y - mean
        var = _head_sum(yc * yc, first) * (1.0 / HEAD)
        yn = yc * lax.rsqrt(var + GN_EPS) * gw_ref[:, sl] + gb_ref[:, sl]
        merged = (yn + st["bonus"]) * gg_ref[st["rows"], sl] + gyb_ref[st["rows"], sl]
        y_ref[st["rows"], sl] = merged.astype(BF16)

    def program(chunks):
        units = []

        def s_prefix():
            units.extend(prefix(q, p) for q in chunks for p in range(n_pairs))

        def each(fn, *args):
            return lambda: [fn(st, *args) for st in units]

        stages = [s_prefix, each(scores), each(values)]
        s = 2
        while s < C:
            stages += [each(double_left, s), each(double_right)]
            s *= 2
        return stages + [each(solve), each(outputs), each(advance), each(normalize)]

    n_chunks = r_ref.shape[0] // C
    programs = [program(range(g, g + group)) for g in range(0, n_chunks, group)]
    for t in range(len(programs[0]) + lag * (len(programs) - 1)):
        for g, stages in enumerate(programs):
            if 0 <= t - g * lag < len(stages):
                stages[t - g * lag]()

    @pl.when(c == pl.num_programs(1) - 1)
    def _():
        for p in range(n_pairs):
            s_pair = h_ref[p].T
            st_ref[0, 2 * p] = s_pair[0:HEAD, 0:HEAD]
            st_ref[0, 2 * p + 1] = pltpu.roll(s_pair, HEAD, 1)[HEAD:2 * HEAD, 0:HEAD]


def _wkv(r, k, v, al, ld, gg, gyb, kk, ka, rk, gw, gb, *, batch, seq):
    m, d = r.shape
    rows = WKV_CHUNKS_PER_STEP * CHUNK
    nc = seq // rows
    n_pairs = d // LANES
    tok = pl.BlockSpec((rows, d), lambda b, c: (b * nc + c, 0))
    par = pl.BlockSpec((1, d), lambda b, c: (0, 0))
    return pl.pallas_call(
        functools.partial(_wkv_kernel, group=WKV_CHUNKS_PER_GROUP, lag=WKV_STAGE_LAG),
        grid=(batch, nc),
        in_specs=[tok] * 7 + [par] * 5,
        out_specs=[tok, pl.BlockSpec((1, 2 * n_pairs, HEAD, HEAD), lambda b, c: (b, 0, 0, 0))],
        out_shape=[jax.ShapeDtypeStruct((m, d), BF16),
                   jax.ShapeDtypeStruct((batch, 2 * n_pairs, HEAD, HEAD), F32)],
        scratch_shapes=[pltpu.VMEM((n_pairs, LANES, LANES), F32)],
        compiler_params=pltpu.CompilerParams(
            dimension_semantics=("arbitrary", "arbitrary"), vmem_limit_bytes=VMEM_LIMIT),
        name="wkv",
    )(r, k, v, al, ld, gg, gyb, kk, ka, rk, gw, gb)


def _wkv_step_kernel(r_ref, k_ref, v_ref, al_ref, ld_ref, gg_ref, gyb_ref, kk_ref, ka_ref,
                     rk_ref, gw_ref, gb_ref, s_ref, y_ref, so_ref):
    hb, _, _, nb = s_ref.shape
    r_t, k_t, v_t, al_t, ld_t = [ref[...].T for ref in (r_ref, k_ref, v_ref, al_ref, ld_ref)]
    diag = (lax.broadcasted_iota(jnp.int32, (HEAD, HEAD, nb), 0)
            == lax.broadcasted_iota(jnp.int32, (HEAD, HEAD, nb), 1))
    ys = []
    for i in range(hb):
        rows = slice(i * HEAD, (i + 1) * HEAD)
        r, k, v, al = r_t[rows], k_t[rows], v_t[rows], al_t[rows]
        decay = jnp.exp(ld_t[rows])
        kkr = k * kk_ref[rows, :]
        nrm = jnp.sqrt(jnp.sum(kkr * kkr, axis=0, keepdims=True))
        kk = kkr / jnp.maximum(nrm, NORM_EPS)
        kmod = k * (1.0 + (al - 1.0) * ka_ref[rows, :])
        bv = kk * al
        s = s_ref[i]
        sa = jnp.sum(s * kk[None], axis=1, keepdims=True)
        v3 = jnp.sum(jnp.where(diag, v[None], 0.0), axis=1, keepdims=True)
        s_new = s * decay[None] - sa * bv[None] + v3 * kmod[None]
        so_ref[i] = s_new
        y3 = jnp.sum(s_new * r[None], axis=1, keepdims=True)
        y = jnp.sum(jnp.where(diag, y3, 0.0), axis=0)
        mean = jnp.mean(y, axis=0, keepdims=True)
        yc = y - mean
        var = jnp.mean(yc * yc, axis=0, keepdims=True)
        yn = yc * lax.rsqrt(var + GN_EPS) * gw_ref[rows, :] + gb_ref[rows, :]
        bonus = jnp.sum(r * kmod * rk_ref[rows, :], axis=0, keepdims=True) * v
        ys.append(yn + bonus)
    ya = jnp.concatenate(ys, axis=0).T
    y_ref[...] = (ya * gg_ref[...] + gyb_ref[...]).astype(BF16)


def _wkv_step(r, k, v, al, ld, gg, gyb, kk, ka, rk, gw, gb, s0, hb):
    nh, _, _, b = s0.shape
    d = r.shape[1]
    tok = pl.BlockSpec((b, hb * HEAD), lambda i: (0, i))
    par = pl.BlockSpec((hb * HEAD, 1), lambda i: (i, 0))
    st = pl.BlockSpec((hb, HEAD, HEAD, b), lambda i: (i, 0, 0, 0))
    return pl.pallas_call(
        _wkv_step_kernel,
        grid=(nh // hb,),
        in_specs=[tok] * 7 + [par] * 5 + [st],
        out_specs=[tok, st],
        out_shape=[jax.ShapeDtypeStruct((b, d), BF16), jax.ShapeDtypeStruct(s0.shape, F32)],
        compiler_params=pltpu.CompilerParams(
            dimension_semantics=("arbitrary",), vmem_limit_bytes=VMEM_LIMIT),
        name="wkv_step",
    )(r, k, v, al, ld, gg, gyb, kk, ka, rk, gw, gb, s0)


def _tail_kernel(x_ref, mg_ref, p_ref, xs_ref, mgs_ref, ps_ref, wo_ref, nf_ref, wg_ref, wu_ref,
                 wd_ref, npl_ref, wpg_ref, wpp_ref, nfin_ref, o_ref, os_ref, *, final):
    def rows_through(x_ref, mg_ref, p_ref, o_ref):
        parts = _row_parts(x_ref.shape[0])
        xs = [x_ref[rows, :] + jnp.dot(mg_ref[rows, :], wo_ref[...], preferred_element_type=F32)
              for rows in parts]
        xs = _swiglu_half(xs, nf_ref[...], wg_ref, wu_ref, wd_ref)
        hps = [_rms(x, npl_ref[...]).astype(BF16) for x in xs]
        ple_gates = [jnp.dot(hp, wpg_ref[...], preferred_element_type=F32) for hp in hps]
        ples = [_bdot(p_ref[rows, :], wpp_ref[...]) for rows in parts]
        for rows, x, gate, ple in zip(parts, xs, ple_gates, ples):
            x = x + _sigmoid(gate) * ple
            o_ref[rows, :] = _rms(x, nfin_ref[...]) if final else x

    last = pl.num_programs(0) - 1

    @pl.when(pl.program_id(0) < last)
    def _():
        rows_through(x_ref, mg_ref, p_ref, o_ref)

    @pl.when(pl.program_id(0) == last)
    def _():
        rows_through(xs_ref, mgs_ref, ps_ref, os_ref)


def _tail(x, merged, p, xs, merged_s, ps, wo, nf, wg, wu, wd, npl, wpg, wpp, nfin, *, final, tm):
    m, d = x.shape
    n = m // tm
    row = lambda w: pl.BlockSpec((tm, w), lambda i: (jnp.minimum(i, n - 1), 0))
    whole = lambda a: pl.BlockSpec(a.shape, lambda i: (0, 0))
    weights = [wo, nf, wg, wu, wd, npl, wpg, wpp, nfin]
    return pl.pallas_call(
        functools.partial(_tail_kernel, final=final),
        grid=(n + 1,),
        in_specs=([row(d)] * 2 + [row(p.shape[1])] + [whole(a) for a in (xs, merged_s, ps)]
                  + [_const_spec(w.shape) for w in weights]),
        out_specs=[row(d), whole(xs)],
        out_shape=[jax.ShapeDtypeStruct((m, d), F32), jax.ShapeDtypeStruct(xs.shape, F32)],
        compiler_params=pltpu.CompilerParams(
            dimension_semantics=("arbitrary",), vmem_limit_bytes=VMEM_LIMIT),
        name="tail",
    )(x, merged, p, xs, merged_s, ps, *weights)


CAST_IN_FFN = ("w_in",)
CAST_IN_MIX = ("w_out", "ffn2_gate", "ffn2_up", "ffn2_down", "ple_gate")


def _mixer(x, x1, s0, sh0, cv0, w, *, decode, batch, seq):
    d = x.shape[1]
    if not decode:
        x1, cast_out = _ffn(x, w["norm_ffn1"], w["ffn1_gate"], w["ffn1_up"], w["ffn1_down"],
                            tm=512, cast=[w[name] for name in CAST_IN_FFN])
        w = dict(w, **dict(zip(CAST_IN_FFN, cast_out)))
    if decode:
        sh_in, c0a, c0b = sh0, cv0[:, 0], cv0[:, 1]
    else:
        sh_in, c0a, c0b = sh0[:, None], cv0[:, 0:1], cv0[:, 1:2]
    r, k, v, al, ld, gg, gyb, sh_new, cv_new, *cast_out = _mix_in(
        x1, w["norm_mix"], w["w_in"], w["mu"], sh_in, c0a, c0b, w["wdec"], w["wicl"],
        w["wgat"], w["decay_bias"], w["iclr_bias"], w["conv_w"],
        decode=decode, batch=batch, seq=seq, tm=min(256, x.shape[0]),
        cast=() if decode else [w[name] for name in CAST_IN_MIX])
    w = dict(w, **dict(zip(CAST_IN_MIX, cast_out)))
    pars = [w["k_k"], w["k_a"], w["r_k"], w["gn_w"], w["gn_b"]]
    if decode:
        merged, s_new = _wkv_step(r, k, v, al, ld, gg, gyb, *[a.reshape(d, 1) for a in pars],
                                  jnp.transpose(s0, (1, 2, 3, 0)), hb=2)
        s_new = jnp.transpose(s_new, (3, 0, 1, 2))
        cv_new = jnp.stack([cv0[:, 1], cv_new], axis=1)
    else:
        merged, s_new = _wkv(r, k, v, al, ld, gg, gyb, *pars, batch=batch, seq=seq)
        sh_new = sh_new[:, 0]
    return x1, merged, s_new, sh_new, cv_new, w


def kernel(x_prompt, x_sample, p_prompt, p_sample, state_wkv, state_shift, state_conv,
           norm_ffn1, w_ffn1_gate, w_ffn1_up, w_ffn1_down, norm_mix, w_in, mu_shift,
           w_decay_up, decay_bias, w_iclr_up, iclr_bias, w_gate_up, k_k, k_a, r_k, gn_w,
           gn_b, conv_w, w_out, norm_ffn2, w_ffn2_gate, w_ffn2_up, w_ffn2_down, norm_ple,
           w_ple_gate, w_ple_proj, norm_final):
    depth = w_in.shape[0]
    bp, tp, d = x_prompt.shape
    bs, ts, _ = x_sample.shape
    assert ts == 1 and tp % (WKV_CHUNKS_PER_STEP * CHUNK) == 0
    shift_cols = mu_shift.shape[1]
    n_dec, n_icl, n_gat = w_decay_up.shape[1], w_iclr_up.shape[1], w_gate_up.shape[1]
    assert 3 * d + n_dec + n_icl + n_gat == shift_cols

    xp = x_prompt.reshape(bp * tp, d)
    xs = x_sample.reshape(bs, d)
    vec = lambda a: a.reshape(1, -1)
    outs = [[] for _ in range(6)]
    for i in range(depth):
        n_tail = shift_cols - 3 * d
        pad = lambda a, lo: jnp.zeros((n_tail, d), BF16).at[lo:lo + a.shape[0]].set(a.astype(BF16))
        xs1, ffn1_gate, ffn1_up, ffn1_down = _ffn_stream(
            xs, vec(norm_ffn1[i]), w_ffn1_gate[i], w_ffn1_up[i], w_ffn1_down[i])
        w = dict(
            norm_ffn1=vec(norm_ffn1[i]), ffn1_gate=ffn1_gate, ffn1_up=ffn1_up,
            ffn1_down=ffn1_down,
            norm_mix=vec(norm_mix[i]), w_in=w_in[i], mu=vec(mu_shift[i]),
            wdec=pad(w_decay_up[i], 0), wicl=pad(w_iclr_up[i], n_dec),
            wgat=pad(w_gate_up[i], n_dec + n_icl),
            decay_bias=vec(decay_bias[i]), iclr_bias=vec(iclr_bias[i]), conv_w=conv_w[i],
            k_k=vec(k_k[i]), k_a=vec(k_a[i]), r_k=vec(r_k[i]), gn_w=vec(gn_w[i]),
            gn_b=vec(gn_b[i]), w_out=w_out[i], norm_ffn2=vec(norm_ffn2[i]),
            ffn2_gate=w_ffn2_gate[i], ffn2_up=w_ffn2_up[i],
            ffn2_down=w_ffn2_down[i], norm_ple=vec(norm_ple[i]),
            ple_gate=w_ple_gate[i], ple_proj=w_ple_proj[i].astype(BF16),
            norm_final=vec(norm_final))
        xp1, merged_p, s_p, sh_p, cv_p, w = _mixer(
            xp, None, None, jnp.zeros((bp, shift_cols), F32), jnp.zeros((bp, 2, d), F32), w,
            decode=False, batch=bp, seq=tp)
        _, merged_s, s_s, sh_s, cv_s, _ = _mixer(
            xs, xs1, state_wkv[i], state_shift[i], state_conv[i], w,
            decode=True, batch=bs, seq=1)
        xp, xs = _tail(xp1, merged_p, p_prompt[i].reshape(bp * tp, -1),
                       xs1, merged_s, p_sample[i].reshape(bs, -1),
                       w["w_out"], w["norm_ffn2"], w["ffn2_gate"], w["ffn2_up"], w["ffn2_down"],
                       w["norm_ple"], w["ple_gate"], w["ple_proj"], w["norm_final"],
                       final=i == depth - 1, tm=512)
        for lst, val in zip(outs, (s_p, sh_p, cv_p, s_s, sh_s, cv_s)):
            lst.append(val)
    stk = [jnp.stack(o, 0) for o in outs]
    return (xp.reshape(bp, tp, d), xs.reshape(bs, ts, d), stk[0], stk[1], stk[2],
            stk[3], stk[4], stk[5])
```

```python
import functools

import jax
import jax.numpy as jnp
from jax import lax
from jax.experimental import pallas as pl
from jax.experimental.pallas import tpu as pltpu

F32 = jnp.float32
BF16 = jnp.bfloat16

HEAD = 64
LANES = 128
CHUNK = 64
MXU_DIM = 256
MIX_COL_BLOCK = MXU_DIM
WKV_CHUNKS_PER_STEP = 8
WKV_CHUNKS_PER_GROUP = 2
WKV_STAGE_LAG = 7
RMS_EPS = 1e-6
GN_EPS = 64e-5
NORM_EPS = 1e-12
VMEM_LIMIT = 56 * 1024 * 1024


def _rms(x, g):
    return x * lax.rsqrt(jnp.mean(x * x, axis=-1, keepdims=True) + RMS_EPS) * g


def _sigmoid(x):
    return 1.0 / (1.0 + jnp.exp(-x))


def _softplus(x):
    return jnp.maximum(x, 0.0) + jnp.log(1.0 + jnp.exp(-jnp.abs(x)))


def _bdot(a, b):
    return jnp.dot(a.astype(BF16), b, preferred_element_type=F32)


def _const_spec(shape):
    zeros = (0,) * len(shape)
    return pl.BlockSpec(shape, lambda *_: zeros, pipeline_mode=pl.Buffered(1))


def _cast_plan(arrays, n_steps, step_of):
    specs, shapes = [], []
    for a in arrays:
        rows, cols = a.shape
        blk = next(b for b in range(16, rows + 1, 16) if rows % b == 0 and rows // b <= n_steps)
        last = rows // blk - 1
        specs.append(pl.BlockSpec(
            (blk, cols), lambda *ids, last=last: (jnp.minimum(step_of(*ids), last), 0)))
        shapes.append(jax.ShapeDtypeStruct(a.shape, BF16))
    return specs, shapes


def _cast_blocks(srcs, dsts):
    for src, dst in zip(srcs, dsts):
        dst[...] = src[...].astype(BF16)


def _row_parts(tm):
    sub = tm // 2 if tm >= 2 * MXU_DIM else tm
    return [slice(s, s + sub) for s in range(0, tm, sub)]


def _swiglu_half(xs, g, wg_ref, wu_ref, wd_ref):
    hs = [_rms(x, g).astype(BF16) for x in xs]
    gates = [jnp.dot(h, wg_ref[...], preferred_element_type=F32) for h in hs]
    ups = [jnp.dot(h, wu_ref[...], preferred_element_type=F32) for h in hs]
    acts = [(gate * _sigmoid(gate) * up).astype(BF16) for gate, up in zip(gates, ups)]
    return [x + 0.5 * jnp.dot(act, wd_ref[...], preferred_element_type=F32)
            for x, act in zip(xs, acts)]


def _ffn_kernel(x_ref, g_ref, wg_ref, wu_ref, wd_ref, *rest, n_cast):
    cast_srcs, o_ref, cast_dsts = rest[:n_cast], rest[n_cast], rest[n_cast + 1:]
    _cast_blocks(cast_srcs, cast_dsts)
    parts = _row_parts(x_ref.shape[0])
    outs = _swiglu_half([x_ref[rows, :] for rows in parts], g_ref[...], wg_ref, wu_ref, wd_ref)
    for rows, out in zip(parts, outs):
        o_ref[rows, :] = out


def _ffn(x, g, wg, wu, wd, tm, cast=()):
    m, d = x.shape
    row = pl.BlockSpec((tm, d), lambda i: (i, 0))
    cast_specs, cast_shapes = _cast_plan(cast, m // tm, lambda i: i)
    out = pl.pallas_call(
        functools.partial(_ffn_kernel, n_cast=len(cast)),
        grid=(m // tm,),
        in_specs=[row, _const_spec(g.shape), _const_spec(wg.shape),
                  _const_spec(wu.shape), _const_spec(wd.shape)] + cast_specs,
        out_specs=[row] + cast_specs,
        out_shape=[jax.ShapeDtypeStruct((m, d), F32)] + cast_shapes,
        compiler_params=pltpu.CompilerParams(
            dimension_semantics=("arbitrary",), vmem_limit_bytes=VMEM_LIMIT),
        name="ffn",
    )(x, g, wg, wu, wd, *cast)
    return out[0], out[1:]


def _ffn_stream_kernel(x_ref, g_ref, wg_ref, wu_ref, wd_ref, o_ref, wgb_ref, wub_ref, wdb_ref,
                       h_ref, acc_ref):
    j = pl.program_id(0)

    @pl.when(j == 0)
    def _():
        h_ref[...] = _rms(x_ref[...], g_ref[...]).astype(BF16)
        acc_ref[...] = jnp.zeros_like(acc_ref)

    wg = wg_ref[...].astype(BF16)
    wu = wu_ref[...].astype(BF16)
    wd = wd_ref[...].astype(BF16)
    wgb_ref[...] = wg
    wub_ref[...] = wu
    wdb_ref[...] = wd
    h = h_ref[...]
    gate = jnp.dot(h, wg, preferred_element_type=F32)
    up = jnp.dot(h, wu, preferred_element_type=F32)
    acc_ref[...] += jnp.dot((gate * _sigmoid(gate) * up).astype(BF16), wd,
                            preferred_element_type=F32)

    @pl.when(j == pl.num_programs(0) - 1)
    def _():
        o_ref[...] = x_ref[...] + 0.5 * acc_ref[...]


def _ffn_stream(x, g, wg, wu, wd):
    m, d = x.shape
    d_ff = wg.shape[1]
    ck = MXU_DIM
    cols = pl.BlockSpec((d, ck), lambda j: (0, j))
    rows = pl.BlockSpec((ck, d), lambda j: (j, 0))
    whole = pl.BlockSpec((m, d), lambda j: (0, 0))
    return pl.pallas_call(
        _ffn_stream_kernel,
        grid=(d_ff // ck,),
        in_specs=[whole, _const_spec(g.shape), cols, cols, rows],
        out_specs=[whole, cols, cols, rows],
        out_shape=[jax.ShapeDtypeStruct((m, d), F32), jax.ShapeDtypeStruct(wg.shape, BF16),
                   jax.ShapeDtypeStruct(wu.shape, BF16), jax.ShapeDtypeStruct(wd.shape, BF16)],
        scratch_shapes=[pltpu.VMEM((m, d), BF16), pltpu.VMEM((m, d), F32)],
        compiler_params=pltpu.CompilerParams(
            dimension_semantics=("arbitrary",), vmem_limit_bytes=VMEM_LIMIT),
        name="ffn_stream",
    )(x, g, wg, wu, wd)


def _mix_in_kernel(*refs, decode, d, shift_cols, blk, n_cast):
    n = n_cast
    (x_ref, nm_ref, w_in_ref, mu_ref, sh0_ref, c0a_ref, c0b_ref,
     wdec_ref, wicl_ref, wgat_ref, dbias_ref, ibias_ref, cw_ref) = refs[:13]
    cast_srcs = refs[13:13 + n]
    (r_ref, k_ref, v_ref, al_ref, ld_ref, gg_ref, gyb_ref,
     sh_out_ref, cv_out_ref) = refs[13 + n:22 + n]
    cast_dsts = refs[22 + n:22 + 2 * n]
    scratch = refs[22 + 2 * n:]
    tm = x_ref.shape[0]
    h = _rms(x_ref[...], nm_ref[...]).astype(BF16)
    conv0, gate0 = shift_cols, shift_cols + 3 * d

    if not decode:
        carry_sh, carry_cu = scratch

        @pl.when(pl.program_id(1) == 0)
        def _():
            carry_sh[0:1, :] = sh0_ref[0]
            carry_cu[0:1, :] = c0a_ref[0]
            carry_cu[1:2, :] = c0b_ref[0]

        row = lax.broadcasted_iota(jnp.int32, (tm, 1), 0)

    def proj(c0, width):
        return jnp.dot(h, w_in_ref[:, c0:c0 + width], preferred_element_type=F32)

    def shifted(c0, width):
        cols = slice(c0, c0 + width)
        z = proj(c0, width)
        if decode:
            z_prev = sh0_ref[:, cols]
            sh_out_ref[:, cols] = z
        else:
            z_prev = jnp.where(row == 0, carry_sh[0:1, cols], pltpu.roll(z, 1, 0))
            carry_sh[0:1, cols] = z[tm - 1:tm, :]
            sh_out_ref[0, :, cols] = z[tm - 1:tm, :]
        return z + mu_ref[:, cols] * (z_prev - z)

    tail = shifted(3 * d, shift_cols - 3 * d)
    tail_tanh = jnp.tanh(tail).astype(BF16)
    tail_sig = _sigmoid(tail).astype(BF16)
    tail_raw = tail.astype(BF16)
    for c in range(0, d, blk):
        cols = slice(c, c + blk)
        if c == blk:
            _cast_blocks(cast_srcs, cast_dsts)
        r_ref[:, cols] = shifted(c, blk)
        k_ref[:, cols] = shifted(d + c, blk)
        v_ref[:, cols] = shifted(2 * d + c, blk)
        dec = dbias_ref[:, cols] + jnp.dot(tail_tanh, wdec_ref[:, cols],
                                           preferred_element_type=F32)
        ld_ref[:, cols] = -jnp.exp(-_softplus(-dec) - 0.5)
        al_ref[:, cols] = _sigmoid(ibias_ref[:, cols] + jnp.dot(
            tail_raw, wicl_ref[:, cols], preferred_element_type=F32))
        gate = jnp.dot(tail_sig, wgat_ref[:, cols], preferred_element_type=F32)
        gg_ref[:, cols] = gate * _sigmoid(proj(gate0 + c, blk))
        cu = proj(conv0 + d + c, blk) * proj(conv0 + 2 * d + c, blk)
        if decode:
            cu_m2 = c0a_ref[:, cols]
            cu_m1 = c0b_ref[:, cols]
            cv_out_ref[:, cols] = cu
        else:
            cu_m1 = jnp.where(row == 0, carry_cu[1:2, cols], pltpu.roll(cu, 1, 0))
            cu_m2 = jnp.where(row == 0, carry_cu[0:1, cols],
                              jnp.where(row == 1, carry_cu[1:2, cols], pltpu.roll(cu, 2, 0)))
            carry_cu[0:2, cols] = cu[tm - 2:tm, :]
            cv_out_ref[0, :, cols] = cu[tm - 2:tm, :]
        conv = cu_m2 * cw_ref[0:1, cols] + cu_m1 * cw_ref[1:2, cols] + cu * cw_ref[2:3, cols]
        gyb_ref[:, cols] = _sigmoid(proj(gate0 + d + c, blk)) * (proj(conv0 + c, blk) * conv)


def _mix_in(x, nm, w_in, mu, sh0, c0a, c0b, wdec, wicl, wgat, dbias, ibias, cw,
            *, decode, batch, seq, tm, cast=()):
    m, d = x.shape
    shift_cols = mu.shape[1]
    consts = [nm, w_in, mu]
    lora = [wdec, wicl, wgat, dbias, ibias, cw]
    if decode:
        grid = (1,)
        row = lambda w: pl.BlockSpec((m, w), lambda i: (0, 0))
        state_specs = [row(shift_cols), row(d), row(d)]
        state_out = [row(shift_cols), row(d)]
        state_shapes = [jax.ShapeDtypeStruct((m, shift_cols), F32),
                        jax.ShapeDtypeStruct((m, d), F32)]
        scratch = []
        sem = ("arbitrary",)
    else:
        nt = seq // tm
        grid = (batch, nt)
        row = lambda w: pl.BlockSpec((tm, w), lambda b, t: (b * nt + t, 0))
        per_b = lambda r, w: pl.BlockSpec((1, r, w), lambda b, t: (b, 0, 0))
        state_specs = [per_b(1, shift_cols), per_b(1, d), per_b(1, d)]
        state_out = [per_b(1, shift_cols), per_b(2, d)]
        state_shapes = [jax.ShapeDtypeStruct((batch, 1, shift_cols), F32),
                        jax.ShapeDtypeStruct((batch, 2, d), F32)]
        scratch = [pltpu.VMEM((8, shift_cols), F32), pltpu.VMEM((8, d), F32)]
        sem = ("arbitrary", "arbitrary")
    tok = jax.ShapeDtypeStruct((m, d), F32)
    assert not (decode and cast)
    cast_specs, cast_shapes = _cast_plan(
        cast, grid[0] * grid[-1], lambda b, t: b * grid[-1] + t)
    return pl.pallas_call(
        functools.partial(_mix_in_kernel, decode=decode, d=d, shift_cols=shift_cols,
                          blk=MIX_COL_BLOCK, n_cast=len(cast)),
        grid=grid,
        in_specs=([row(d)] + [_const_spec(a.shape) for a in consts] + state_specs
                  + [_const_spec(a.shape) for a in lora] + cast_specs),
        out_specs=[row(d)] * 7 + state_out + cast_specs,
        out_shape=[tok] * 7 + state_shapes + cast_shapes,
        scratch_shapes=scratch,
        compiler_params=pltpu.CompilerParams(
            dimension_semantics=sem, vmem_limit_bytes=VMEM_LIMIT),
        name="mix_in_decode" if decode else "mix_in",
    )(x, *consts, sh0, c0a, c0b, *lora, *cast)


def _head_sum(x, first):
    x1 = jnp.where(first, x, 0.0)
    s1 = jnp.sum(x1, axis=-1, keepdims=True)
    s2 = jnp.sum(x - x1, axis=-1, keepdims=True)
    return jnp.where(first, s1, s2)


def _stack_heads(x):
    x = x.astype(BF16)
    lane = lax.broadcasted_iota(jnp.int32, x.shape, 1) % LANES
    first = lane < HEAD
    zero = jnp.zeros_like(x)
    return jnp.concatenate([jnp.where(first, x, zero), jnp.where(first, zero, x)], axis=0)


def _apply(mat, x):
    return jnp.dot(mat.astype(BF16), _stack_heads(x), preferred_element_type=F32)


def _split3(x):
    hi = x.astype(BF16)
    rest = x - hi.astype(F32)
    mid = rest.astype(BF16)
    return hi, mid, (rest - mid.astype(F32)).astype(BF16)


def _wkv_kernel(r_ref, k_ref, v_ref, al_ref, ld_ref, gg_ref, gyb_ref, kk_ref, ka_ref, rk_ref,
                gw_ref, gb_ref, y_ref, st_ref, h_ref, *, group, lag):
    c = pl.program_id(1)
    n_pairs = h_ref.shape[0]

    @pl.when(c == 0)
    def _():
        h_ref[...] = jnp.zeros_like(h_ref)

    C = CHUNK
    row = lax.broadcasted_iota(jnp.int32, (C, LANES), 0)
    col = lax.broadcasted_iota(jnp.int32, (C, LANES), 1)
    first = col < HEAD
    j = col % HEAD
    strict = j < row
    incl = j <= row
    cs_r = lax.broadcasted_iota(jnp.int32, (C, 3 * C), 0)
    cs_c = lax.broadcasted_iota(jnp.int32, (C, 3 * C), 1) % C
    tri3 = (cs_c <= cs_r).astype(BF16)
    sq_r = lax.broadcasted_iota(jnp.int32, (LANES, LANES), 0)
    sq_c = lax.broadcasted_iota(jnp.int32, (LANES, LANES), 1)
    same_head = (sq_r // HEAD) == (sq_c // HEAD)
    eye = sq_r == sq_c

    def prefix(q, p):
        rows = slice(q * C, (q + 1) * C)
        sl = slice(p * LANES, (p + 1) * LANES)
        ld = ld_ref[rows, sl]
        lp = jnp.dot(tri3, jnp.concatenate(_split3(ld), axis=0), preferred_element_type=F32)
        return dict(p=p, rows=rows, sl=sl, ld=ld, lp=lp)

    def scores(st):
        rows, sl, ld, lp = st["rows"], st["sl"], st["ld"], st["lp"]
        r = r_ref[rows, sl]
        k = k_ref[rows, sl]
        v = v_ref[rows, sl]
        al = al_ref[rows, sl]
        lp_end = lp[C - 1:C, :]
        kkr = k * kk_ref[:, sl]
        nrm = jnp.sqrt(_head_sum(kkr * kkr, first))
        kk = kkr / jnp.maximum(nrm, NORM_EPS)
        kmod = k * (1.0 + (al - 1.0) * ka_ref[:, sl])
        bv = kk * al
        e_neg = jnp.exp(-lp)
        e_end = jnp.exp(lp_end - lp)
        rt = r * jnp.exp(lp)
        st.update(v=v.astype(BF16), at=(kk * jnp.exp(lp - ld)).astype(BF16), rt=rt,
                  bh=bv * e_end, kh=kmod * e_end, p_end=jnp.exp(lp_end),
                  bonus=_head_sum(r * kmod * rk_ref[:, sl], first) * v)
        sc = lax.dot_general(
            jnp.concatenate([st["at"], rt.astype(BF16)], axis=0),
            jnp.concatenate([_stack_heads(bv * e_neg), _stack_heads(kmod * e_neg)], axis=0),
            (((1,), (1,)), ((), ())), preferred_element_type=F32)
        lab = jnp.where(strict, sc[0:C, 0:LANES], 0.0)
        st["lak_mrk"] = jnp.concatenate(
            [jnp.where(strict, sc[0:C, LANES:2 * LANES], 0.0).astype(BF16),
             jnp.where(incl, sc[C:2 * C, LANES:2 * LANES], 0.0).astype(BF16)], axis=0)
        st["mrb"] = jnp.where(incl, sc[C:2 * C, 0:LANES], 0.0).astype(BF16)
        st["inv"] = (jnp.where(j == row, 1.0, 0.0)
                     - jnp.where((j // 2) == (row // 2), lab, 0.0))
        lab_b = lab.astype(BF16)
        st["lab2"] = jnp.concatenate([lab_b, lab_b], axis=0)

    tok_r = sq_r % HEAD
    tok_c = sq_c % HEAD

    def double_left(st, s):
        keep = same_head & ((tok_r // (2 * s)) == (tok_c // (2 * s))) & ((tok_r // s) != (tok_c // s))
        st["inv_b"] = st["inv"].astype(BF16)
        st["inv_off"] = jnp.dot(st["inv_b"], jnp.where(keep, st["lab2"], jnp.zeros_like(st["lab2"])),
                                preferred_element_type=F32)

    def double_right(st):
        inv2 = jnp.concatenate([st["inv_b"], st["inv_b"]], axis=0)
        st["inv"] = st["inv"] - jnp.dot(st["inv_off"].astype(BF16),
                                        jnp.where(same_head, inv2, jnp.zeros_like(inv2)),
                                        preferred_element_type=F32)

    def values(st):
        xv = _apply(st["lak_mrk"], st["v"])
        st["x_b"] = xv[0:C].astype(BF16)
        st["mrk_v"] = xv[C:2 * C]
        st["lhs_t"] = jnp.concatenate([st["kh"], -st["bh"]], axis=0).T.astype(BF16)

    def solve(st):
        uw = _apply(st["inv"], jnp.concatenate([st["at"], st["x_b"]], axis=1))
        st["uw"] = uw.astype(BF16)

    def outputs(st):
        uw, v = st["uw"], st["v"]
        mu = _apply(st["mrb"], uw)
        st["q"] = (st["rt"] - mu[:, 0:LANES]).astype(BF16)
        st["y0"] = st["mrk_v"] - mu[:, LANES:2 * LANES]
        rhs = jnp.concatenate(
            [jnp.concatenate([v, jnp.zeros_like(v)], axis=1),
             jnp.concatenate([uw[:, LANES:2 * LANES], uw[:, 0:LANES]], axis=1)], axis=0)
        gb = jnp.dot(st["lhs_t"], rhs, preferred_element_type=F32)
        st["g_c"] = jnp.where(same_head, gb[:, 0:LANES], 0.0)
        st["a_c"] = jnp.where(same_head, gb[:, LANES:2 * LANES], 0.0).astype(BF16)
        st["p_col"] = jnp.sum(jnp.where(eye, st["p_end"], 0.0), axis=1, keepdims=True)

    def advance(st):
        p = st["p"]
        h0 = h_ref[p]
        qa = jnp.dot(jnp.concatenate([st["q"], st["a_c"]], axis=0), h0.astype(BF16),
                     preferred_element_type=F32)
        st["y"] = qa[0:C] + st["y0"]
        h_ref[p] = st["p_col"] * h0 + qa[C:C + LANES] + st["g_c"]

    def normalize(st):
        sl, y = st["sl"], st["y"]
        mean = _head_sum(y, first) * (1.0 / HEAD)
        yc = y - mean
        var = _head_sum(yc * yc, first) * (1.0 / HEAD)
        yn = yc * lax.rsqrt(var + GN_EPS) * gw_ref[:, sl] + gb_ref[:, sl]
        merged = (yn + st["bonus"]) * gg_ref[st["rows"], sl] + gyb_ref[st["rows"], sl]
        y_ref[st["rows"], sl] = merged.astype(BF16)

    def program(chunks):
        units = []

        def s_prefix():
            units.extend(prefix(q, p) for q in chunks for p in range(n_pairs))

        def each(fn, *args):
            return lambda: [fn(st, *args) for st in units]

        stages = [s_prefix, each(scores), each(values)]
        s = 2
        while s < C:
            stages += [each(double_left, s), each(double_right)]
            s *= 2
        return stages + [each(solve), each(outputs), each(advance), each(normalize)]

    n_chunks = r_ref.shape[0] // C
    programs = [program(range(g, g + group)) for g in range(0, n_chunks, group)]
    for t in range(len(programs[0]) + lag * (len(programs) - 1)):
        for g, stages in enumerate(programs):
            if 0 <= t - g * lag < len(stages):
                stages[t - g * lag]()

    @pl.when(c == pl.num_programs(1) - 1)
    def _():
        for p in range(n_pairs):
            s_pair = h_ref[p].T
            st_ref[0, 2 * p] = s_pair[0:HEAD, 0:HEAD]
            st_ref[0, 2 * p + 1] = pltpu.roll(s_pair, HEAD, 1)[HEAD:2 * HEAD, 0:HEAD]


def _wkv(r, k, v, al, ld, gg, gyb, kk, ka, rk, gw, gb, *, batch, seq):
    m, d = r.shape
    rows = WKV_CHUNKS_PER_STEP * CHUNK
    nc = seq // rows
    n_pairs = d // LANES
    tok = pl.BlockSpec((rows, d), lambda b, c: (b * nc + c, 0))
    par = pl.BlockSpec((1, d), lambda b, c: (0, 0))
    return pl.pallas_call(
        functools.partial(_wkv_kernel, group=WKV_CHUNKS_PER_GROUP, lag=WKV_STAGE_LAG),
        grid=(batch, nc),
        in_specs=[tok] * 7 + [par] * 5,
        out_specs=[tok, pl.BlockSpec((1, 2 * n_pairs, HEAD, HEAD), lambda b, c: (b, 0, 0, 0))],
        out_shape=[jax.ShapeDtypeStruct((m, d), BF16),
                   jax.ShapeDtypeStruct((batch, 2 * n_pairs, HEAD, HEAD), F32)],
        scratch_shapes=[pltpu.VMEM((n_pairs, LANES, LANES), F32)],
        compiler_params=pltpu.CompilerParams(
            dimension_semantics=("arbitrary", "arbitrary"), vmem_limit_bytes=VMEM_LIMIT),
        name="wkv",
    )(r, k, v, al, ld, gg, gyb, kk, ka, rk, gw, gb)


def _wkv_step_kernel(r_ref, k_ref, v_ref, al_ref, ld_ref, gg_ref, gyb_ref, kk_ref, ka_ref,
                     rk_ref, gw_ref, gb_ref, s_ref, y_ref, so_ref):
    hb, _, _, nb = s_ref.shape
    r_t, k_t, v_t, al_t, ld_t = [ref[...].T for ref in (r_ref, k_ref, v_ref, al_ref, ld_ref)]
    diag = (lax.broadcasted_iota(jnp.int32, (HEAD, HEAD, nb), 0)
            == lax.broadcasted_iota(jnp.int32, (HEAD, HEAD, nb), 1))
    ys = []
    for i in range(hb):
        rows = slice(i * HEAD, (i + 1) * HEAD)
        r, k, v, al = r_t[rows], k_t[rows], v_t[rows], al_t[rows]
        decay = jnp.exp(ld_t[rows])
        kkr = k * kk_ref[rows, :]
        nrm = jnp.sqrt(jnp.sum(kkr * kkr, axis=0, keepdims=True))
        kk = kkr / jnp.maximum(nrm, NORM_EPS)
        kmod = k * (1.0 + (al - 1.0) * ka_ref[rows, :])
        bv = kk * al
        s = s_ref[i]
        sa = jnp.sum(s * kk[None], axis=1, keepdims=True)
        v3 = jnp.sum(jnp.where(diag, v[None], 0.0), axis=1, keepdims=True)
        s_new = s * decay[None] - sa * bv[None] + v3 * kmod[None]
        so_ref[i] = s_new
        y3 = jnp.sum(s_new * r[None], axis=1, keepdims=True)
        y = jnp.sum(jnp.where(diag, y3, 0.0), axis=0)
        mean = jnp.mean(y, axis=0, keepdims=True)
        yc = y - mean
        var = jnp.mean(yc * yc, axis=0, keepdims=True)
        yn = yc * lax.rsqrt(var + GN_EPS) * gw_ref[rows, :] + gb_ref[rows, :]
        bonus = jnp.sum(r * kmod * rk_ref[rows, :], axis=0, keepdims=True) * v
        ys.append(yn + bonus)
    ya = jnp.concatenate(ys, axis=0).T
    y_ref[...] = (ya * gg_ref[...] + gyb_ref[...]).astype(BF16)


def _wkv_step(r, k, v, al, ld, gg, gyb, kk, ka, rk, gw, gb, s0, hb):
    nh, _, _, b = s0.shape
    d = r.shape[1]
    tok = pl.BlockSpec((b, hb * HEAD), lambda i: (0, i))
    par = pl.BlockSpec((hb * HEAD, 1), lambda i: (i, 0))
    st = pl.BlockSpec((hb, HEAD, HEAD, b), lambda i: (i, 0, 0, 0))
    return pl.pallas_call(
        _wkv_step_kernel,
        grid=(nh // hb,),
        in_specs=[tok] * 7 + [par] * 5 + [st],
        out_specs=[tok, st],
        out_shape=[jax.ShapeDtypeStruct((b, d), BF16), jax.ShapeDtypeStruct(s0.shape, F32)],
        compiler_params=pltpu.CompilerParams(
            dimension_semantics=("arbitrary",), vmem_limit_bytes=VMEM_LIMIT),
        name="wkv_step",
    )(r, k, v, al, ld, gg, gyb, kk, ka, rk, gw, gb, s0)


def _tail_kernel(x_ref, mg_ref, p_ref, xs_ref, mgs_ref, ps_ref, wo_ref, nf_ref, wg_ref, wu_ref,
                 wd_ref, npl_ref, wpg_ref, wpp_ref, nfin_ref, o_ref, os_ref, *, final):
    def rows_through(x_ref, mg_ref, p_ref, o_ref):
        parts = _row_parts(x_ref.shape[0])
        xs = [x_ref[rows, :] + jnp.dot(mg_ref[rows, :], wo_ref[...], preferred_element_type=F32)
              for rows in parts]
        xs = _swiglu_half(xs, nf_ref[...], wg_ref, wu_ref, wd_ref)
        hps = [_rms(x, npl_ref[...]).astype(BF16) for x in xs]
        ple_gates = [jnp.dot(hp, wpg_ref[...], preferred_element_type=F32) for hp in hps]
        ples = [_bdot(p_ref[rows, :], wpp_ref[...]) for rows in parts]
        for rows, x, gate, ple in zip(parts, xs, ple_gates, ples):
            x = x + _sigmoid(gate) * ple
            o_ref[rows, :] = _rms(x, nfin_ref[...]) if final else x

    last = pl.num_programs(0) - 1

    @pl.when(pl.program_id(0) < last)
    def _():
        rows_through(x_ref, mg_ref, p_ref, o_ref)

    @pl.when(pl.program_id(0) == last)
    def _():
        rows_through(xs_ref, mgs_ref, ps_ref, os_ref)


def _tail(x, merged, p, xs, merged_s, ps, wo, nf, wg, wu, wd, npl, wpg, wpp, nfin, *, final, tm):
    m, d = x.shape
    n = m // tm
    row = lambda w: pl.BlockSpec((tm, w), lambda i: (jnp.minimum(i, n - 1), 0))
    whole = lambda a: pl.BlockSpec(a.shape, lambda i: (0, 0))
    weights = [wo, nf, wg, wu, wd, npl, wpg, wpp, nfin]
    return pl.pallas_call(
        functools.partial(_tail_kernel, final=final),
        grid=(n + 1,),
        in_specs=([row(d)] * 2 + [row(p.shape[1])] + [whole(a) for a in (xs, merged_s, ps)]
                  + [_const_spec(w.shape) for w in weights]),
        out_specs=[row(d), whole(xs)],
        out_shape=[jax.ShapeDtypeStruct((m, d), F32), jax.ShapeDtypeStruct(xs.shape, F32)],
        compiler_params=pltpu.CompilerParams(
            dimension_semantics=("arbitrary",), vmem_limit_bytes=VMEM_LIMIT),
        name="tail",
    )(x, merged, p, xs, merged_s, ps, *weights)


CAST_IN_FFN = ("w_in",)
CAST_IN_MIX = ("w_out", "ffn2_gate", "ffn2_up", "ffn2_down", "ple_gate")


def _mixer(x, x1, s0, sh0, cv0, w, *, decode, batch, seq):
    d = x.shape[1]
    if not decode:
        x1, cast_out = _ffn(x, w["norm_ffn1"], w["ffn1_gate"], w["ffn1_up"], w["ffn1_down"],
                            tm=512, cast=[w[name] for name in CAST_IN_FFN])
        w = dict(w, **dict(zip(CAST_IN_FFN, cast_out)))
    if decode:
        sh_in, c0a, c0b = sh0, cv0[:, 0], cv0[:, 1]
    else:
        sh_in, c0a, c0b = sh0[:, None], cv0[:, 0:1], cv0[:, 1:2]
    r, k, v, al, ld, gg, gyb, sh_new, cv_new, *cast_out = _mix_in(
        x1, w["norm_mix"], w["w_in"], w["mu"], sh_in, c0a, c0b, w["wdec"], w["wicl"],
        w["wgat"], w["decay_bias"], w["iclr_bias"], w["conv_w"],
        decode=decode, batch=batch, seq=seq, tm=min(256, x.shape[0]),
        cast=() if decode else [w[name] for name in CAST_IN_MIX])
    w = dict(w, **dict(zip(CAST_IN_MIX, cast_out)))
    pars = [w["k_k"], w["k_a"], w["r_k"], w["gn_w"], w["gn_b"]]
    if decode:
        merged, s_new = _wkv_step(r, k, v, al, ld, gg, gyb, *[a.reshape(d, 1) for a in pars],
                                  jnp.transpose(s0, (1, 2, 3, 0)), hb=2)
        s_new = jnp.transpose(s_new, (3, 0, 1, 2))
        cv_new = jnp.stack([cv0[:, 1], cv_new], axis=1)
    else:
        merged, s_new = _wkv(r, k, v, al, ld, gg, gyb, *pars, batch=batch, seq=seq)
        sh_new = sh_new[:, 0]
    return x1, merged, s_new, sh_new, cv_new, w


def kernel(x_prompt, x_sample, p_prompt, p_sample, state_wkv, state_shift, state_conv,
           norm_ffn1, w_ffn1_gate, w_ffn1_up, w_ffn1_down, norm_mix, w_in, mu_shift,
           w_decay_up, decay_bias, w_iclr_up, iclr_bias, w_gate_up, k_k, k_a, r_k, gn_w,
           gn_b, conv_w, w_out, norm_ffn2, w_ffn2_gate, w_ffn2_up, w_ffn2_down, norm_ple,
           w_ple_gate, w_ple_proj, norm_final):
    depth = w_in.shape[0]
    bp, tp, d = x_prompt.shape
    bs, ts, _ = x_sample.shape
    assert ts == 1 and tp % (WKV_CHUNKS_PER_STEP * CHUNK) == 0
    shift_cols = mu_shift.shape[1]
    n_dec, n_icl, n_gat = w_decay_up.shape[1], w_iclr_up.shape[1], w_gate_up.shape[1]
    assert 3 * d + n_dec + n_icl + n_gat == shift_cols

    xp = x_prompt.reshape(bp * tp, d)
    xs = x_sample.reshape(bs, d)
    vec = lambda a: a.reshape(1, -1)
    outs = [[] for _ in range(6)]
    for i in range(depth):
        n_tail = shift_cols - 3 * d
        pad = lambda a, lo: jnp.zeros((n_tail, d), BF16).at[lo:lo + a.shape[0]].set(a.astype(BF16))
        xs1, ffn1_gate, ffn1_up, ffn1_down = _ffn_stream(
            xs, vec(norm_ffn1[i]), w_ffn1_gate[i], w_ffn1_up[i], w_ffn1_down[i])
        w = dict(
            norm_ffn1=vec(norm_ffn1[i]), ffn1_gate=ffn1_gate, ffn1_up=ffn1_up,
            ffn1_down=ffn1_down,
            norm_mix=vec(norm_mix[i]), w_in=w_in[i], mu=vec(mu_shift[i]),
            wdec=pad(w_decay_up[i], 0), wicl=pad(w_iclr_up[i], n_dec),
            wgat=pad(w_gate_up[i], n_dec + n_icl),
            decay_bias=vec(decay_bias[i]), iclr_bias=vec(iclr_bias[i]), conv_w=conv_w[i],
            k_k=vec(k_k[i]), k_a=vec(k_a[i]), r_k=vec(r_k[i]), gn_w=vec(gn_w[i]),
            gn_b=vec(gn_b[i]), w_out=w_out[i], norm_ffn2=vec(norm_ffn2[i]),
            ffn2_gate=w_ffn2_gate[i], ffn2_up=w_ffn2_up[i],
            ffn2_down=w_ffn2_down[i], norm_ple=vec(norm_ple[i]),
            ple_gate=w_ple_gate[i], ple_proj=w_ple_proj[i].astype(BF16),
            norm_final=vec(norm_final))
        xp1, merged_p, s_p, sh_p, cv_p, w = _mixer(
            xp, None, None, jnp.zeros((bp, shift_cols), F32), jnp.zeros((bp, 2, d), F32), w,
            decode=False, batch=bp, seq=tp)
        _, merged_s, s_s, sh_s, cv_s, _ = _mixer(
            xs, xs1, state_wkv[i], state_shift[i], state_conv[i], w,
            decode=True, batch=bs, seq=1)
        xp, xs = _tail(xp1, merged_p, p_prompt[i].reshape(bp * tp, -1),
                       xs1, merged_s, p_sample[i].reshape(bs, -1),
                       w["w_out"], w["norm_ffn2"], w["ffn2_gate"], w["ffn2_up"], w["ffn2_down"],
                       w["norm_ple"], w["ple_gate"], w["ple_proj"], w["norm_final"],
                       final=i == depth - 1, tm=512)
        for lst, val in zip(outs, (s_p, sh_p, cv_p, s_s, sh_s, cv_s)):
            lst.append(val)
    stk = [jnp.stack(o, 0) for o in outs]
    return (xp.reshape(bp, tp, d), xs.reshape(bs, ts, d), stk[0], stk[1], stk[2],
            stk[3], stk[4], stk[5])
```

```python
import functools

import jax
import jax.numpy as jnp
from jax import lax
from jax.experimental import pallas as pl
from jax.experimental.pallas import tpu as pltpu

F32 = jnp.float32
BF16 = jnp.bfloat16

HEAD = 64
LANES = 128
CHUNK = 64
MXU_DIM = 256
MIX_COL_BLOCK = MXU_DIM
WKV_CHUNKS_PER_STEP = 4
WKV_CHUNKS_PER_GROUP = 2
WKV_STAGE_LAG = 7
RMS_EPS = 1e-6
GN_EPS = 64e-5
NORM_EPS = 1e-12
VMEM_LIMIT = 56 * 1024 * 1024


def _rms(x, g):
    return x * lax.rsqrt(jnp.mean(x * x, axis=-1, keepdims=True) + RMS_EPS) * g


def _sigmoid(x):
    return 1.0 / (1.0 + jnp.exp(-x))


def _softplus(x):
    return jnp.maximum(x, 0.0) + jnp.log(1.0 + jnp.exp(-jnp.abs(x)))


def _bdot(a, b):
    return jnp.dot(a.astype(BF16), b, preferred_element_type=F32)


def _const_spec(shape):
    zeros = (0,) * len(shape)
    return pl.BlockSpec(shape, lambda *_: zeros, pipeline_mode=pl.Buffered(1))


def _cast_plan(arrays, n_steps, step_of):
    specs, shapes = [], []
    for a in arrays:
        rows, cols = a.shape
        blk = next(b for b in range(16, rows + 1, 16) if rows % b == 0 and rows // b <= n_steps)
        last = rows // blk - 1
        specs.append(pl.BlockSpec(
            (blk, cols), lambda *ids, last=last: (jnp.minimum(step_of(*ids), last), 0)))
        shapes.append(jax.ShapeDtypeStruct(a.shape, BF16))
    return specs, shapes


def _cast_blocks(srcs, dsts):
    for src, dst in zip(srcs, dsts):
        dst[...] = src[...].astype(BF16)


def _row_parts(tm):
    sub = tm // 2 if tm >= 2 * MXU_DIM else tm
    return [slice(s, s + sub) for s in range(0, tm, sub)]


def _swiglu_half(xs, g, wg_ref, wu_ref, wd_ref):
    hs = [_rms(x, g).astype(BF16) for x in xs]
    gates = [jnp.dot(h, wg_ref[...], preferred_element_type=F32) for h in hs]
    ups = [jnp.dot(h, wu_ref[...], preferred_element_type=F32) for h in hs]
    acts = [(gate * _sigmoid(gate) * up).astype(BF16) for gate, up in zip(gates, ups)]
    return [x + 0.5 * jnp.dot(act, wd_ref[...], preferred_element_type=F32)
            for x, act in zip(xs, acts)]


def _ffn_kernel(x_ref, g_ref, wg_ref, wu_ref, wd_ref, *rest, n_cast):
    cast_srcs, o_ref, cast_dsts = rest[:n_cast], rest[n_cast], rest[n_cast + 1:]
    _cast_blocks(cast_srcs, cast_dsts)
    parts = _row_parts(x_ref.shape[0])
    outs = _swiglu_half([x_ref[rows, :] for rows in parts], g_ref[...], wg_ref, wu_ref, wd_ref)
    for rows, out in zip(parts, outs):
        o_ref[rows, :] = out


def _ffn(x, g, wg, wu, wd, tm, cast=()):
    m, d = x.shape
    row = pl.BlockSpec((tm, d), lambda i: (i, 0))
    cast_specs, cast_shapes = _cast_plan(cast, m // tm, lambda i: i)
    out = pl.pallas_call(
        functools.partial(_ffn_kernel, n_cast=len(cast)),
        grid=(m // tm,),
        in_specs=[row, _const_spec(g.shape), _const_spec(wg.shape),
                  _const_spec(wu.shape), _const_spec(wd.shape)] + cast_specs,
        out_specs=[row] + cast_specs,
        out_shape=[jax.ShapeDtypeStruct((m, d), F32)] + cast_shapes,
        compiler_params=pltpu.CompilerParams(
            dimension_semantics=("arbitrary",), vmem_limit_bytes=VMEM_LIMIT),
        name="ffn",
    )(x, g, wg, wu, wd, *cast)
    return out[0], out[1:]


def _ffn_stream_kernel(x_ref, g_ref, wg_ref, wu_ref, wd_ref, o_ref, wgb_ref, wub_ref, wdb_ref,
                       h_ref, acc_ref):
    j = pl.program_id(0)

    @pl.when(j == 0)
    def _():
        h_ref[...] = _rms(x_ref[...], g_ref[...]).astype(BF16)
        acc_ref[...] = jnp.zeros_like(acc_ref)

    wg = wg_ref[...].astype(BF16)
    wu = wu_ref[...].astype(BF16)
    wd = wd_ref[...].astype(BF16)
    wgb_ref[...] = wg
    wub_ref[...] = wu
    wdb_ref[...] = wd
    h = h_ref[...]
    gate = jnp.dot(h, wg, preferred_element_type=F32)
    up = jnp.dot(h, wu, preferred_element_type=F32)
    acc_ref[...] += jnp.dot((gate * _sigmoid(gate) * up).astype(BF16), wd,
                            preferred_element_type=F32)

    @pl.when(j == pl.num_programs(0) - 1)
    def _():
        o_ref[...] = x_ref[...] + 0.5 * acc_ref[...]


def _ffn_stream(x, g, wg, wu, wd):
    m, d = x.shape
    d_ff = wg.shape[1]
    ck = MXU_DIM
    cols = pl.BlockSpec((d, ck), lambda j: (0, j))
    rows = pl.BlockSpec((ck, d), lambda j: (j, 0))
    whole = pl.BlockSpec((m, d), lambda j: (0, 0))
    return pl.pallas_call(
        _ffn_stream_kernel,
        grid=(d_ff // ck,),
        in_specs=[whole, _const_spec(g.shape), cols, cols, rows],
        out_specs=[whole, cols, cols, rows],
        out_shape=[jax.ShapeDtypeStruct((m, d), F32), jax.ShapeDtypeStruct(wg.shape, BF16),
                   jax.ShapeDtypeStruct(wu.shape, BF16), jax.ShapeDtypeStruct(wd.shape, BF16)],
        scratch_shapes=[pltpu.VMEM((m, d), BF16), pltpu.VMEM((m, d), F32)],
        compiler_params=pltpu.CompilerParams(
            dimension_semantics=("arbitrary",), vmem_limit_bytes=VMEM_LIMIT),
        name="ffn_stream",
    )(x, g, wg, wu, wd)


def _mix_in_kernel(*refs, decode, d, shift_cols, blk, n_cast):
    n = n_cast
    (x_ref, nm_ref, w_in_ref, mu_ref, sh0_ref, c0a_ref, c0b_ref,
     wdec_ref, wicl_ref, wgat_ref, dbias_ref, ibias_ref, cw_ref) = refs[:13]
    cast_srcs = refs[13:13 + n]
    (r_ref, k_ref, v_ref, al_ref, ld_ref, gg_ref, gyb_ref,
     sh_out_ref, cv_out_ref) = refs[13 + n:22 + n]
    cast_dsts = refs[22 + n:22 + 2 * n]
    scratch = refs[22 + 2 * n:]
    tm = x_ref.shape[0]
    h = _rms(x_ref[...], nm_ref[...]).astype(BF16)
    conv0, gate0 = shift_cols, shift_cols + 3 * d

    if not decode:
        carry_sh, carry_cu = scratch

        @pl.when(pl.program_id(1) == 0)
        def _():
            carry_sh[0:1, :] = sh0_ref[0]
            carry_cu[0:1, :] = c0a_ref[0]
            carry_cu[1:2, :] = c0b_ref[0]

        row = lax.broadcasted_iota(jnp.int32, (tm, 1), 0)

    def proj(c0, width):
        return jnp.dot(h, w_in_ref[:, c0:c0 + width], preferred_element_type=F32)

    def shifted(c0, width):
        cols = slice(c0, c0 + width)
        z = proj(c0, width)
        if decode:
            z_prev = sh0_ref[:, cols]
            sh_out_ref[:, cols] = z
        else:
            z_prev = jnp.where(row == 0, carry_sh[0:1, cols], pltpu.roll(z, 1, 0))
            carry_sh[0:1, cols] = z[tm - 1:tm, :]
            sh_out_ref[0, :, cols] = z[tm - 1:tm, :]
        return z + mu_ref[:, cols] * (z_prev - z)

    tail = shifted(3 * d, shift_cols - 3 * d)
    tail_tanh = jnp.tanh(tail).astype(BF16)
    tail_sig = _sigmoid(tail).astype(BF16)
    tail_raw = tail.astype(BF16)
    for c in range(0, d, blk):
        cols = slice(c, c + blk)
        if c == blk:
            _cast_blocks(cast_srcs, cast_dsts)
        r_ref[:, cols] = shifted(c, blk)
        k_ref[:, cols] = shifted(d + c, blk)
        v_ref[:, cols] = shifted(2 * d + c, blk)
        dec = dbias_ref[:, cols] + jnp.dot(tail_tanh, wdec_ref[:, cols],
                                           preferred_element_type=F32)
        ld_ref[:, cols] = -jnp.exp(-_softplus(-dec) - 0.5)
        al_ref[:, cols] = _sigmoid(ibias_ref[:, cols] + jnp.dot(
            tail_raw, wicl_ref[:, cols], preferred_element_type=F32))
        gate = jnp.dot(tail_sig, wgat_ref[:, cols], preferred_element_type=F32)
        gg_ref[:, cols] = gate * _sigmoid(proj(gate0 + c, blk))
        cu = proj(conv0 + d + c, blk) * proj(conv0 + 2 * d + c, blk)
        if decode:
            cu_m2 = c0a_ref[:, cols]
            cu_m1 = c0b_ref[:, cols]
            cv_out_ref[:, cols] = cu
        else:
            cu_m1 = jnp.where(row == 0, carry_cu[1:2, cols], pltpu.roll(cu, 1, 0))
            cu_m2 = jnp.where(row == 0, carry_cu[0:1, cols],
                              jnp.where(row == 1, carry_cu[1:2, cols], pltpu.roll(cu, 2, 0)))
            carry_cu[0:2, cols] = cu[tm - 2:tm, :]
            cv_out_ref[0, :, cols] = cu[tm - 2:tm, :]
        conv = cu_m2 * cw_ref[0:1, cols] + cu_m1 * cw_ref[1:2, cols] + cu * cw_ref[2:3, cols]
        gyb_ref[:, cols] = _sigmoid(proj(gate0 + d + c, blk)) * (proj(conv0 + c, blk) * conv)


def _mix_in(x, nm, w_in, mu, sh0, c0a, c0b, wdec, wicl, wgat, dbias, ibias, cw,
            *, decode, batch, seq, tm, cast=()):
    m, d = x.shape
    shift_cols = mu.shape[1]
    consts = [nm, w_in, mu]
    lora = [wdec, wicl, wgat, dbias, ibias, cw]
    if decode:
        grid = (1,)
        row = lambda w: pl.BlockSpec((m, w), lambda i: (0, 0))
        state_specs = [row(shift_cols), row(d), row(d)]
        state_out = [row(shift_cols), row(d)]
        state_shapes = [jax.ShapeDtypeStruct((m, shift_cols), F32),
                        jax.ShapeDtypeStruct((m, d), F32)]
        scratch = []
        sem = ("arbitrary",)
    else:
        nt = seq // tm
        grid = (batch, nt)
        row = lambda w: pl.BlockSpec((tm, w), lambda b, t: (b * nt + t, 0))
        per_b = lambda r, w: pl.BlockSpec((1, r, w), lambda b, t: (b, 0, 0))
        state_specs = [per_b(1, shift_cols), per_b(1, d), per_b(1, d)]
        state_out = [per_b(1, shift_cols), per_b(2, d)]
        state_shapes = [jax.ShapeDtypeStruct((batch, 1, shift_cols), F32),
                        jax.ShapeDtypeStruct((batch, 2, d), F32)]
        scratch = [pltpu.VMEM((8, shift_cols), F32), pltpu.VMEM((8, d), F32)]
        sem = ("arbitrary", "arbitrary")
    tok = jax.ShapeDtypeStruct((m, d), F32)
    assert not (decode and cast)
    cast_specs, cast_shapes = _cast_plan(
        cast, grid[0] * grid[-1], lambda b, t: b * grid[-1] + t)
    return pl.pallas_call(
        functools.partial(_mix_in_kernel, decode=decode, d=d, shift_cols=shift_cols,
                          blk=MIX_COL_BLOCK, n_cast=len(cast)),
        grid=grid,
        in_specs=([row(d)] + [_const_spec(a.shape) for a in consts] + state_specs
                  + [_const_spec(a.shape) for a in lora] + cast_specs),
        out_specs=[row(d)] * 7 + state_out + cast_specs,
        out_shape=[tok] * 7 + state_shapes + cast_shapes,
        scratch_shapes=scratch,
        compiler_params=pltpu.CompilerParams(
            dimension_semantics=sem, vmem_limit_bytes=VMEM_LIMIT),
        name="mix_in_decode" if decode else "mix_in",
    )(x, *consts, sh0, c0a, c0b, *lora, *cast)


def _head_sum(x, first):
    x1 = jnp.where(first, x, 0.0)
    s1 = jnp.sum(x1, axis=-1, keepdims=True)
    s2 = jnp.sum(x - x1, axis=-1, keepdims=True)
    return jnp.where(first, s1, s2)


def _stack_heads(x):
    x = x.astype(BF16)
    lane = lax.broadcasted_iota(jnp.int32, x.shape, 1) % LANES
    first = lane < HEAD
    zero = jnp.zeros_like(x)
    return jnp.concatenate([jnp.where(first, x, zero), jnp.where(first, zero, x)], axis=0)


def _apply(mat, x):
    return jnp.dot(mat.astype(BF16), _stack_heads(x), preferred_element_type=F32)


def _split3(x):
    hi = x.astype(BF16)
    rest = x - hi.astype(F32)
    mid = rest.astype(BF16)
    return hi, mid, (rest - mid.astype(F32)).astype(BF16)


def _wkv_kernel(r_ref, k_ref, v_ref, al_ref, ld_ref, gg_ref, gyb_ref, kk_ref, ka_ref, rk_ref,
                gw_ref, gb_ref, y_ref, st_ref, h_ref, *, group, lag):
    c = pl.program_id(1)
    n_pairs = h_ref.shape[0]

    @pl.when(c == 0)
    def _():
        h_ref[...] = jnp.zeros_like(h_ref)

    C = CHUNK
    row = lax.broadcasted_iota(jnp.int32, (C, LANES), 0)
    col = lax.broadcasted_iota(jnp.int32, (C, LANES), 1)
    first = col < HEAD
    j = col % HEAD
    strict = j < row
    incl = j <= row
    cs_r = lax.broadcasted_iota(jnp.int32, (C, 3 * C), 0)
    cs_c = lax.broadcasted_iota(jnp.int32, (C, 3 * C), 1) % C
    tri3 = (cs_c <= cs_r).astype(BF16)
    sq_r = lax.broadcasted_iota(jnp.int32, (LANES, LANES), 0)
    sq_c = lax.broadcasted_iota(jnp.int32, (LANES, LANES), 1)
    same_head = (sq_r // HEAD) == (sq_c // HEAD)
    eye = sq_r == sq_c

    def prefix(q, p):
        rows = slice(q * C, (q + 1) * C)
        sl = slice(p * LANES, (p + 1) * LANES)
        ld = ld_ref[rows, sl]
        lp = jnp.dot(tri3, jnp.concatenate(_split3(ld), axis=0), preferred_element_type=F32)
        return dict(p=p, rows=rows, sl=sl, ld=ld, lp=lp)

    def scores(st):
        rows, sl, ld, lp = st["rows"], st["sl"], st["ld"], st["lp"]
        r = r_ref[rows, sl]
        k = k_ref[rows, sl]
        v = v_ref[rows, sl]
        al = al_ref[rows, sl]
        lp_end = lp[C - 1:C, :]
        kkr = k * kk_ref[:, sl]
        nrm = jnp.sqrt(_head_sum(kkr * kkr, first))
        kk = kkr / jnp.maximum(nrm, NORM_EPS)
        kmod = k * (1.0 + (al - 1.0) * ka_ref[:, sl])
        bv = kk * al
        e_neg = jnp.exp(-lp)
        e_end = jnp.exp(lp_end - lp)
        rt = r * jnp.exp(lp)
        st.update(v=v.astype(BF16), at=(kk * jnp.exp(lp - ld)).astype(BF16), rt=rt,
                  bh=bv * e_end, kh=kmod * e_end, p_end=jnp.exp(lp_end),
                  bonus=_head_sum(r * kmod * rk_ref[:, sl], first) * v)
        sc = lax.dot_general(
            jnp.concatenate([st["at"], rt.astype(BF16)], axis=0),
            jnp.concatenate([_stack_heads(bv * e_neg), _stack_heads(kmod * e_neg)], axis=0),
            (((1,), (1,)), ((), ())), preferred_element_type=F32)
        lab = jnp.where(strict, sc[0:C, 0:LANES], 0.0)
        st["lak_mrk"] = jnp.concatenate(
            [jnp.where(strict, sc[0:C, LANES:2 * LANES], 0.0).astype(BF16),
             jnp.where(incl, sc[C:2 * C, LANES:2 * LANES], 0.0).astype(BF16)], axis=0)
        st["mrb"] = jnp.where(incl, sc[C:2 * C, 0:LANES], 0.0).astype(BF16)
        st["inv"] = (jnp.where(j == row, 1.0, 0.0)
                     - jnp.where((j // 2) == (row // 2), lab, 0.0))
        lab_b = lab.astype(BF16)
        st["lab2"] = jnp.concatenate([lab_b, lab_b], axis=0)

    tok_r = sq_r % HEAD
    tok_c = sq_c % HEAD

    def double_left(st, s):
        keep = same_head & ((tok_r // (2 * s)) == (tok_c // (2 * s))) & ((tok_r // s) != (tok_c // s))
        st["inv_b"] = st["inv"].astype(BF16)
        st["inv_off"] = jnp.dot(st["inv_b"], jnp.where(keep, st["lab2"], jnp.zeros_like(st["lab2"])),
                                preferred_element_type=F32)

    def double_right(st):
        inv2 = jnp.concatenate([st["inv_b"], st["inv_b"]], axis=0)
        st["inv"] = st["inv"] - jnp.dot(st["inv_off"].astype(BF16),
                                        jnp.where(same_head, inv2, jnp.zeros_like(inv2)),
                                        preferred_element_type=F32)

    def values(st):
        xv = _apply(st["lak_mrk"], st["v"])
        st["x_b"] = xv[0:C].astype(BF16)
        st["mrk_v"] = xv[C:2 * C]
        st["lhs_t"] = jnp.concatenate([st["kh"], -st["bh"]], axis=0).T.astype(BF16)

    def solve(st):
        uw = _apply(st["inv"], jnp.concatenate([st["at"], st["x_b"]], axis=1))
        st["uw"] = uw.astype(BF16)

    def outputs(st):
        uw, v = st["uw"], st["v"]
        mu = _apply(st["mrb"], uw)
        st["q"] = (st["rt"] - mu[:, 0:LANES]).astype(BF16)
        st["y0"] = st["mrk_v"] - mu[:, LANES:2 * LANES]
        rhs = jnp.concatenate(
            [jnp.concatenate([v, jnp.zeros_like(v)], axis=1),
             jnp.concatenate([uw[:, LANES:2 * LANES], uw[:, 0:LANES]], axis=1)], axis=0)
        gb = jnp.dot(st["lhs_t"], rhs, preferred_element_type=F32)
        st["g_c"] = jnp.where(same_head, gb[:, 0:LANES], 0.0)
        st["a_c"] = jnp.where(same_head, gb[:, LANES:2 * LANES], 0.0).astype(BF16)
        st["p_col"] = jnp.sum(jnp.where(eye, st["p_end"], 0.0), axis=1, keepdims=True)

    def advance(st):
        p = st["p"]
        h0 = h_ref[p]
        qa = jnp.dot(jnp.concatenate([st["q"], st["a_c"]], axis=0), h0.astype(BF16),
                     preferred_element_type=F32)
        st["y"] = qa[0:C] + st["y0"]
        h_ref[p] = st["p_col"] * h0 + qa[C:C + LANES] + st["g_c"]

    def normalize(st):
        sl, y = st["sl"], st["y"]
        mean = _head_sum(y, first) * (1.0 / HEAD)
        yc = y - mean
        var = _head_sum(yc * yc, first) * (1.0 / HEAD)
        yn = yc * lax.rsqrt(var + GN_EPS) * gw_ref[:, sl] + gb_ref[:, sl]
        merged = (yn + st["bonus"]) * gg_ref[st["rows"], sl] + gyb_ref[st["rows"], sl]
        y_ref[st["rows"], sl] = merged.astype(BF16)

    def program(chunks):
        units = []

        def s_prefix():
            units.extend(prefix(q, p) for q in chunks for p in range(n_pairs))

        def each(fn, *args):
            return lambda: [fn(st, *args) for st in units]

        stages = [s_prefix, each(scores), each(values)]
        s = 2
        while s < C:
            stages += [each(double_left, s), each(double_right)]
            s *= 2
        return stages + [each(solve), each(outputs), each(advance), each(normalize)]

    n_chunks = r_ref.shape[0] // C
    programs = [program(range(g, g + group)) for g in range(0, n_chunks, group)]
    for t in range(len(programs[0]) + lag * (len(programs) - 1)):
        for g, stages in enumerate(programs):
            if 0 <= t - g * lag < len(stages):
                stages[t - g * lag]()

    @pl.when(c == pl.num_programs(1) - 1)
    def _():
        for p in range(n_pairs):
            s_pair = h_ref[p].T
            st_ref[0, 2 * p] = s_pair[0:HEAD, 0:HEAD]
            st_ref[0, 2 * p + 1] = pltpu.roll(s_pair, HEAD, 1)[HEAD:2 * HEAD, 0:HEAD]


def _wkv(r, k, v, al, ld, gg, gyb, kk, ka, rk, gw, gb, *, batch, seq):
    m, d = r.shape
    rows = WKV_CHUNKS_PER_STEP * CHUNK
    nc = seq // rows
    n_pairs = d // LANES
    tok = pl.BlockSpec((rows, d), lambda b, c: (b * nc + c, 0))
    par = pl.BlockSpec((1, d), lambda b, c: (0, 0))
    return pl.pallas_call(
        functools.partial(_wkv_kernel, group=WKV_CHUNKS_PER_GROUP, lag=WKV_STAGE_LAG),
        grid=(batch, nc),
        in_specs=[tok] * 7 + [par] * 5,
        out_specs=[tok, pl.BlockSpec((1, 2 * n_pairs, HEAD, HEAD), lambda b, c: (b, 0, 0, 0))],
        out_shape=[jax.ShapeDtypeStruct((m, d), BF16),
                   jax.ShapeDtypeStruct((batch, 2 * n_pairs, HEAD, HEAD), F32)],
        scratch_shapes=[pltpu.VMEM((n_pairs, LANES, LANES), F32)],
        compiler_params=pltpu.CompilerParams(
            dimension_semantics=("arbitrary", "arbitrary"), vmem_limit_bytes=VMEM_LIMIT),
        name="wkv",
    )(r, k, v, al, ld, gg, gyb, kk, ka, rk, gw, gb)


def _wkv_step_kernel(r_ref, k_ref, v_ref, al_ref, ld_ref, gg_ref, gyb_ref, kk_ref, ka_ref,
                     rk_ref, gw_ref, gb_ref, s_ref, y_ref, so_ref):
    hb, _, _, nb = s_ref.shape
    r_t, k_t, v_t, al_t, ld_t = [ref[...].T for ref in (r_ref, k_ref, v_ref, al_ref, ld_ref)]
    diag = (lax.broadcasted_iota(jnp.int32, (HEAD, HEAD, nb), 0)
            == lax.broadcasted_iota(jnp.int32, (HEAD, HEAD, nb), 1))
    ys = []
    for i in range(hb):
        rows = slice(i * HEAD, (i + 1) * HEAD)
        r, k, v, al = r_t[rows], k_t[rows], v_t[rows], al_t[rows]
        decay = jnp.exp(ld_t[rows])
        kkr = k * kk_ref[rows, :]
        nrm = jnp.sqrt(jnp.sum(kkr * kkr, axis=0, keepdims=True))
        kk = kkr / jnp.maximum(nrm, NORM_EPS)
        kmod = k * (1.0 + (al - 1.0) * ka_ref[rows, :])
        bv = kk * al
        s = s_ref[i]
        sa = jnp.sum(s * kk[None], axis=1, keepdims=True)
        v3 = jnp.sum(jnp.where(diag, v[None], 0.0), axis=1, keepdims=True)
        s_new = s * decay[None] - sa * bv[None] + v3 * kmod[None]
        so_ref[i] = s_new
        y3 = jnp.sum(s_new * r[None], axis=1, keepdims=True)
        y = jnp.sum(jnp.where(diag, y3, 0.0), axis=0)
        mean = jnp.mean(y, axis=0, keepdims=True)
        yc = y - mean
        var = jnp.mean(yc * yc, axis=0, keepdims=True)
        yn = yc * lax.rsqrt(var + GN_EPS) * gw_ref[rows, :] + gb_ref[rows, :]
        bonus = jnp.sum(r * kmod * rk_ref[rows, :], axis=0, keepdims=True) * v
        ys.append(yn + bonus)
    ya = jnp.concatenate(ys, axis=0).T
    y_ref[...] = (ya * gg_ref[...] + gyb_ref[...]).astype(BF16)


def _wkv_step(r, k, v, al, ld, gg, gyb, kk, ka, rk, gw, gb, s0, hb):
    nh, _, _, b = s0.shape
    d = r.shape[1]
    tok = pl.BlockSpec((b, hb * HEAD), lambda i: (0, i))
    par = pl.BlockSpec((hb * HEAD, 1), lambda i: (i, 0))
    st = pl.BlockSpec((hb, HEAD, HEAD, b), lambda i: (i, 0, 0, 0))
    return pl.pallas_call(
        _wkv_step_kernel,
        grid=(nh // hb,),
        in_specs=[tok] * 7 + [par] * 5 + [st],
        out_specs=[tok, st],
        out_shape=[jax.ShapeDtypeStruct((b, d), BF16), jax.ShapeDtypeStruct(s0.shape, F32)],
        compiler_params=pltpu.CompilerParams(
            dimension_semantics=("arbitrary",), vmem_limit_bytes=VMEM_LIMIT),
        name="wkv_step",
    )(r, k, v, al, ld, gg, gyb, kk, ka, rk, gw, gb, s0)


def _tail_kernel(x_ref, mg_ref, p_ref, xs_ref, mgs_ref, ps_ref, wo_ref, nf_ref, wg_ref, wu_ref,
                 wd_ref, npl_ref, wpg_ref, wpp_ref, nfin_ref, o_ref, os_ref, *, final):
    def rows_through(x_ref, mg_ref, p_ref, o_ref):
        parts = _row_parts(x_ref.shape[0])
        xs = [x_ref[rows, :] + jnp.dot(mg_ref[rows, :], wo_ref[...], preferred_element_type=F32)
              for rows in parts]
        xs = _swiglu_half(xs, nf_ref[...], wg_ref, wu_ref, wd_ref)
        hps = [_rms(x, npl_ref[...]).astype(BF16) for x in xs]
        ple_gates = [jnp.dot(hp, wpg_ref[...], preferred_element_type=F32) for hp in hps]
        ples = [_bdot(p_ref[rows, :], wpp_ref[...]) for rows in parts]
        for rows, x, gate, ple in zip(parts, xs, ple_gates, ples):
            x = x + _sigmoid(gate) * ple
            o_ref[rows, :] = _rms(x, nfin_ref[...]) if final else x

    last = pl.num_programs(0) - 1

    @pl.when(pl.program_id(0) < last)
    def _():
        rows_through(x_ref, mg_ref, p_ref, o_ref)

    @pl.when(pl.program_id(0) == last)
    def _():
        rows_through(xs_ref, mgs_ref, ps_ref, os_ref)


def _tail(x, merged, p, xs, merged_s, ps, wo, nf, wg, wu, wd, npl, wpg, wpp, nfin, *, final, tm):
    m, d = x.shape
    n = m // tm
    row = lambda w: pl.BlockSpec((tm, w), lambda i: (jnp.minimum(i, n - 1), 0))
    whole = lambda a: pl.BlockSpec(a.shape, lambda i: (0, 0))
    weights = [wo, nf, wg, wu, wd, npl, wpg, wpp, nfin]
    return pl.pallas_call(
        functools.partial(_tail_kernel, final=final),
        grid=(n + 1,),
        in_specs=([row(d)] * 2 + [row(p.shape[1])] + [whole(a) for a in (xs, merged_s, ps)]
                  + [_const_spec(w.shape) for w in weights]),
        out_specs=[row(d), whole(xs)],
        out_shape=[jax.ShapeDtypeStruct((m, d), F32), jax.ShapeDtypeStruct(xs.shape, F32)],
        compiler_params=pltpu.CompilerParams(
            dimension_semantics=("arbitrary",), vmem_limit_bytes=VMEM_LIMIT),
        name="tail",
    )(x, merged, p, xs, merged_s, ps, *weights)


CAST_IN_FFN = ("w_in",)
CAST_IN_MIX = ("w_out", "ffn2_gate", "ffn2_up", "ffn2_down", "ple_gate")


def _mixer(x, x1, s0, sh0, cv0, w, *, decode, batch, seq):
    d = x.shape[1]
    if not decode:
        x1, cast_out = _ffn(x, w["norm_ffn1"], w["ffn1_gate"], w["ffn1_up"], w["ffn1_down"],
                            tm=512, cast=[w[name] for name in CAST_IN_FFN])
        w = dict(w, **dict(zip(CAST_IN_FFN, cast_out)))
    if decode:
        sh_in, c0a, c0b = sh0, cv0[:, 0], cv0[:, 1]
    else:
        sh_in, c0a, c0b = sh0[:, None], cv0[:, 0:1], cv0[:, 1:2]
    r, k, v, al, ld, gg, gyb, sh_new, cv_new, *cast_out = _mix_in(
        x1, w["norm_mix"], w["w_in"], w["mu"], sh_in, c0a, c0b, w["wdec"], w["wicl"],
        w["wgat"], w["decay_bias"], w["iclr_bias"], w["conv_w"],
        decode=decode, batch=batch, seq=seq, tm=min(256, x.shape[0]),
        cast=() if decode else [w[name] for name in CAST_IN_MIX])
    w = dict(w, **dict(zip(CAST_IN_MIX, cast_out)))
    pars = [w["k_k"], w["k_a"], w["r_k"], w["gn_w"], w["gn_b"]]
    if decode:
        merged, s_new = _wkv_step(r, k, v, al, ld, gg, gyb, *[a.reshape(d, 1) for a in pars],
                                  jnp.transpose(s0, (1, 2, 3, 0)), hb=2)
        s_new = jnp.transpose(s_new, (3, 0, 1, 2))
        cv_new = jnp.stack([cv0[:, 1], cv_new], axis=1)
    else:
        merged, s_new = _wkv(r, k, v, al, ld, gg, gyb, *pars, batch=batch, seq=seq)
        sh_new = sh_new[:, 0]
    return x1, merged, s_new, sh_new, cv_new, w


def kernel(x_prompt, x_sample, p_prompt, p_sample, state_wkv, state_shift, state_conv,
           norm_ffn1, w_ffn1_gate, w_ffn1_up, w_ffn1_down, norm_mix, w_in, mu_shift,
           w_decay_up, decay_bias, w_iclr_up, iclr_bias, w_gate_up, k_k, k_a, r_k, gn_w,
           gn_b, conv_w, w_out, norm_ffn2, w_ffn2_gate, w_ffn2_up, w_ffn2_down, norm_ple,
           w_ple_gate, w_ple_proj, norm_final):
    depth = w_in.shape[0]
    bp, tp, d = x_prompt.shape
    bs, ts, _ = x_sample.shape
    assert ts == 1 and tp % (WKV_CHUNKS_PER_STEP * CHUNK) == 0
    shift_cols = mu_shift.shape[1]
    n_dec, n_icl, n_gat = w_decay_up.shape[1], w_iclr_up.shape[1], w_gate_up.shape[1]
    assert 3 * d + n_dec + n_icl + n_gat == shift_cols

    xp = x_prompt.reshape(bp * tp, d)
    xs = x_sample.reshape(bs, d)
    vec = lambda a: a.reshape(1, -1)
    outs = [[] for _ in range(6)]
    for i in range(depth):
        n_tail = shift_cols - 3 * d
        pad = lambda a, lo: jnp.zeros((n_tail, d), BF16).at[lo:lo + a.shape[0]].set(a.astype(BF16))
        xs1, ffn1_gate, ffn1_up, ffn1_down = _ffn_stream(
            xs, vec(norm_ffn1[i]), w_ffn1_gate[i], w_ffn1_up[i], w_ffn1_down[i])
        w = dict(
            norm_ffn1=vec(norm_ffn1[i]), ffn1_gate=ffn1_gate, ffn1_up=ffn1_up,
            ffn1_down=ffn1_down,
            norm_mix=vec(norm_mix[i]), w_in=w_in[i], mu=vec(mu_shift[i]),
            wdec=pad(w_decay_up[i], 0), wicl=pad(w_iclr_up[i], n_dec),
            wgat=pad(w_gate_up[i], n_dec + n_icl),
            decay_bias=vec(decay_bias[i]), iclr_bias=vec(iclr_bias[i]), conv_w=conv_w[i],
            k_k=vec(k_k[i]), k_a=vec(k_a[i]), r_k=vec(r_k[i]), gn_w=vec(gn_w[i]),
            gn_b=vec(gn_b[i]), w_out=w_out[i], norm_ffn2=vec(norm_ffn2[i]),
            ffn2_gate=w_ffn2_gate[i], ffn2_up=w_ffn2_up[i],
            ffn2_down=w_ffn2_down[i], norm_ple=vec(norm_ple[i]),
            ple_gate=w_ple_gate[i], ple_proj=w_ple_proj[i].astype(BF16),
            norm_final=vec(norm_final))
        xp1, merged_p, s_p, sh_p, cv_p, w = _mixer(
            xp, None, None, jnp.zeros((bp, shift_cols), F32), jnp.zeros((bp, 2, d), F32), w,
            decode=False, batch=bp, seq=tp)
        _, merged_s, s_s, sh_s, cv_s, _ = _mixer(
            xs, xs1, state_wkv[i], state_shift[i], state_conv[i], w,
            decode=True, batch=bs, seq=1)
        xp, xs = _tail(xp1, merged_p, p_prompt[i].reshape(bp * tp, -1),
                       xs1, merged_s, p_sample[i].reshape(bs, -1),
                       w["w_out"], w["norm_ffn2"], w["ffn2_gate"], w["ffn2_up"], w["ffn2_down"],
                       w["norm_ple"], w["ple_gate"], w["ple_proj"], w["norm_final"],
                       final=i == depth - 1, tm=512)
        for lst, val in zip(outs, (s_p, sh_p, cv_p, s_s, sh_s, cv_s)):
            lst.append(val)
    stk = [jnp.stack(o, 0) for o in outs]
    return (xp.reshape(bp, tp, d), xs.reshape(bs, ts, d), stk[0], stk[1], stk[2],
            stk[3], stk[4], stk[5])
```
